```python
import math
import jax, jax.numpy as jnp
from jax import lax
import numpy as np

D_MODEL = 2048
BATCH = 4
SEQ = 2048
DEPTH = 4

CHUNK = 64
Q_BLOCK = 128
N_MIXERS = 2
ROPE_THETA = 500000.0
RMS_EPS = 1e-6
N_MOD = 6

DIFF_HEAD_DIM = 128
DIFF_HEADS = D_MODEL // (2 * DIFF_HEAD_DIM)
DIFF_ROT = DIFF_HEAD_DIM // 4

MLA_V = 128
MLA_HEADS = D_MODEL // MLA_V
MLA_Q_RANK = 512
MLA_KV_RANK = 512
MLA_NOPE = 128
MLA_ROPE = 64

N_EXPERTS = 32
TOP_K = 4
D_FF_EXPERT = D_MODEL // 2
SWIGLU_LIMIT = 7.0
SWIGLU_ALPHA = 1.702
EXPERT_BLOCK = 128

N_DIFF_LAYERS = (DEPTH + 1) // 2
N_MLA_LAYERS = DEPTH // 2

kernel_name = "hybrid_diffattn_mla_moe_adaln_chunk_causal"


def _rms_norm(x, g):
    xf = x.astype(jnp.float32)
    y = xf * lax.rsqrt(jnp.mean(xf * xf, axis=-1, keepdims=True) + RMS_EPS)
    return (y * g.astype(jnp.float32)).astype(x.dtype)


def _rope_tables(positions, rot_dim):
    inv_freq = ROPE_THETA ** (-jnp.arange(0, rot_dim, 2, dtype=jnp.float32) / rot_dim)
    ang = positions.astype(jnp.float32)[..., None] * inv_freq
    return jnp.cos(ang), jnp.sin(ang)


def _apply_rope(x, cos, sin):
    half = x.shape[-1] // 2
    x1 = x[..., :half].astype(jnp.float32)
    x2 = x[..., half:].astype(jnp.float32)
    cs = cos[:, :, None, :]
    sn = sin[:, :, None, :]
    return jnp.concatenate([x1 * cs - x2 * sn, x1 * sn + x2 * cs], axis=-1).astype(x.dtype)


def _partial_rope(x, cos, sin, rot_dim):
    return jnp.concatenate([_apply_rope(x[..., :rot_dim], cos, sin), x[..., rot_dim:]], axis=-1)


def _chunk_mask(start, end):
    q_chunk = jnp.arange(start, end) // CHUNK
    k_chunk = jnp.arange(end) // CHUNK
    return k_chunk[None, :] <= q_chunk[:, None]


def _masked_softmax(scores, mask):
    s = jnp.where(mask, scores.astype(jnp.float32), -jnp.inf)
    return jax.nn.softmax(s, axis=-1)


def _sweep_query_blocks(block_fn, seq):
    outs = [block_fn(s0, min(s0 + Q_BLOCK, seq)) for s0 in range(0, seq, Q_BLOCK)]
    return jnp.concatenate(outs, axis=1)


def _diff_attention(u, w_in, lam, subln_g, w_out, cos, sin, lambda_init):
    B, S, _ = u.shape
    H, Dh = DIFF_HEADS, DIFF_HEAD_DIM
    q, k, v = jnp.split(u @ w_in, 3, axis=-1)
    q = _partial_rope(q.reshape(B, S, 2 * H, Dh), cos, sin, DIFF_ROT).reshape(B, S, H, 2, Dh)
    k = _partial_rope(k.reshape(B, S, 2 * H, Dh), cos, sin, DIFF_ROT).reshape(B, S, H, 2, Dh)
    v = v.reshape(B, S, H, 2 * Dh)
    q1, q2 = q[..., 0, :], q[..., 1, :]
    k1, k2 = k[..., 0, :], k[..., 1, :]
    lam = lam.astype(jnp.float32)
    lam_full = jnp.exp(jnp.sum(lam[0] * lam[1])) - jnp.exp(jnp.sum(lam[2] * lam[3])) + lambda_init
    scale = Dh ** -0.5

    def block(start, end):
        mask = _chunk_mask(start, end)
        p1 = _masked_softmax(jnp.einsum('bqhd,bkhd->bhqk', q1[:, start:end], k1[:, :end]) * scale, mask)
        p2 = _masked_softmax(jnp.einsum('bqhd,bkhd->bhqk', q2[:, start:end], k2[:, :end]) * scale, mask)
        p = (p1 - lam_full * p2).astype(v.dtype)
        return jnp.einsum('bhqk,bkhd->bqhd', p, v[:, :end])

    o = _sweep_query_blocks(block, S)
    o = _rms_norm(o, subln_g) * (1.0 - lambda_init)
    return o.reshape(B, S, H * 2 * Dh) @ w_out


def _mla(u, w_in, q_norm_g, kv_norm_g, w_uq, w_ukv, w_out, cos, sin):
    B, S, _ = u.shape
    H = MLA_HEADS
    lat = u @ w_in
    q_lat = lat[..., :MLA_Q_RANK]
    kv_lat = lat[..., MLA_Q_RANK:MLA_Q_RANK + MLA_KV_RANK]
    k_rope = lat[..., MLA_Q_RANK + MLA_KV_RANK:]
    q = (_rms_norm(q_lat, q_norm_g) @ w_uq).reshape(B, S, H, MLA_NOPE + MLA_ROPE)
    q_nope = q[..., :MLA_NOPE]
    q_rope = _apply_rope(q[..., MLA_NOPE:], cos, sin)
    kv = (_rms_norm(kv_lat, kv_norm_g) @ w_ukv).reshape(B, S, H, MLA_NOPE + MLA_V)
    k_nope = kv[..., :MLA_NOPE]
    v = kv[..., MLA_NOPE:]
    k_rope = _apply_rope(k_rope[:, :, None, :], cos, sin)[:, :, 0, :]
    scale = (MLA_NOPE + MLA_ROPE) ** -0.5

    def block(start, end):
        mask = _chunk_mask(start, end)
        s = (jnp.einsum('bqhd,bkhd->bhqk', q_nope[:, start:end], k_nope[:, :end])
             + jnp.einsum('bqhr,bkr->bhqk', q_rope[:, start:end], k_rope[:, :end]))
        p = _masked_softmax(s * scale, mask).astype(v.dtype)
        return jnp.einsum('bhqk,bkhd->bqhd', p, v[:, :end])

    o = _sweep_query_blocks(block, S)
    return o.reshape(B, S, H * MLA_V) @ w_out


def _moe(u, w_router, b_router, w_gu, b_gu, w_down, b_down):
    B, S, D = u.shape
    xf = u.reshape(-1, D)
    N = xf.shape[0]
    logits = (xf @ w_router).astype(jnp.float32) + b_router.astype(jnp.float32)
    top_val, top_idx = lax.top_k(logits, TOP_K)
    gate_w = jax.nn.softmax(top_val, axis=-1)
    A = N * TOP_K
    flat_e = top_idx.reshape(-1)
    flat_tok = jnp.repeat(jnp.arange(N, dtype=jnp.int32), TOP_K)
    flat_w = gate_w.reshape(-1)
    order = jnp.argsort(flat_e)
    sorted_e = flat_e[order]
    counts = jnp.bincount(flat_e, length=N_EXPERTS)
    padded = (counts + EXPERT_BLOCK - 1) // EXPERT_BLOCK * EXPERT_BLOCK
    group_start = jnp.cumsum(counts) - counts
    pad_end = jnp.cumsum(padded)
    pad_start = pad_end - padded
    dest = pad_start[sorted_e] + (jnp.arange(A, dtype=jnp.int32) - group_start[sorted_e])
    n_blocks = -(-(A + N_EXPERTS * (EXPERT_BLOCK - 1)) // EXPERT_BLOCK)
    P = n_blocks * EXPERT_BLOCK
    row_tok = jnp.zeros((P,), jnp.int32).at[dest].set(flat_tok[order])
    row_w = jnp.zeros((P,), jnp.float32).at[dest].set(flat_w[order])
    block_e = jnp.minimum(
        jnp.searchsorted(pad_end, jnp.arange(n_blocks, dtype=jnp.int32) * EXPERT_BLOCK, side='right'),
        N_EXPERTS - 1)
    xb = xf[row_tok].reshape(n_blocks, EXPERT_BLOCK, D)

    def expert_block(args):
        xblk, e = args
        gu = xblk @ w_gu[e] + b_gu[e]
        gate = jnp.minimum(gu[..., :D_FF_EXPERT], SWIGLU_LIMIT)
        up = jnp.clip(gu[..., D_FF_EXPERT:], -SWIGLU_LIMIT, SWIGLU_LIMIT)
        glu = gate * jax.nn.sigmoid(gate * SWIGLU_ALPHA)
        return ((up + 1.0) * glu) @ w_down[e] + b_down[e]

    yb = lax.map(expert_block, (xb, block_e)).reshape(P, D)
    y = jnp.zeros_like(xf).at[row_tok].add((yb * row_w[:, None].astype(yb.dtype)).astype(xf.dtype))
    return y.reshape(B, S, D)


def setup_inputs(seed: int = 0) -> dict:
    key = jax.random.key(seed)
    ks = jax.random.split(key, 32)
    f32 = jnp.float32
    D = D_MODEL
    ND, NM = N_DIFF_LAYERS, N_MLA_LAYERS
    qkv_w = 3 * 2 * DIFF_HEADS * DIFF_HEAD_DIM
    mla_in = MLA_Q_RANK + MLA_KV_RANK + MLA_ROPE

    def nrm(k, shape, scale):
        return jax.random.normal(k, shape, f32) * scale

    offset = jax.random.randint(ks[2], (BATCH, 1), 0, 64, dtype=jnp.int32) * CHUNK
    positions = offset + jnp.arange(SEQ, dtype=jnp.int32)[None, :]
    return {
        "x": nrm(ks[0], (BATCH, SEQ, D), 1.0),
        "c": nrm(ks[1], (BATCH, D), 1.0),
        "positions": positions,
        "ada_w": nrm(ks[3], (DEPTH, D, N_MOD * D), 0.5 * D ** -0.5),
        "ada_b": nrm(ks[4], (DEPTH, N_MOD * D), 0.01),
        "mix_norm_g": 1.0 + nrm(ks[5], (DEPTH, D), 0.01),
        "ffn_norm_g": 1.0 + nrm(ks[6], (DEPTH, D), 0.01),
        "final_norm_g": 1.0 + nrm(ks[7], (D,), 0.01),
        "diff_w_in": nrm(ks[8], (ND, D, qkv_w), D ** -0.5),
        "diff_lambda": nrm(ks[9], (ND, 4, DIFF_HEAD_DIM), 0.1),
        "diff_subln_g": 1.0 + nrm(ks[10], (ND, 2 * DIFF_HEAD_DIM), 0.01),
        "diff_w_out": nrm(ks[11], (ND, DIFF_HEADS * 2 * DIFF_HEAD_DIM, D), (DIFF_HEADS * 2 * DIFF_HEAD_DIM) ** -0.5),
        "mla_w_in": nrm(ks[12], (NM, D, mla_in), D ** -0.5),
        "mla_q_norm_g": 1.0 + nrm(ks[13], (NM, MLA_Q_RANK), 0.01),
        "mla_kv_norm_g": 1.0 + nrm(ks[14], (NM, MLA_KV_RANK), 0.01),
        "mla_w_uq": nrm(ks[15], (NM, MLA_Q_RANK, MLA_HEADS * (MLA_NOPE + MLA_ROPE)), MLA_Q_RANK ** -0.5),
        "mla_w_ukv": nrm(ks[16], (NM, MLA_KV_RANK, MLA_HEADS * (MLA_NOPE + MLA_V)), MLA_KV_RANK ** -0.5),
        "mla_w_out": nrm(ks[17], (NM, MLA_HEADS * MLA_V, D), (MLA_HEADS * MLA_V) ** -0.5),
        "moe_w_router": nrm(ks[18], (DEPTH, D, N_EXPERTS), D ** -0.5),
        "moe_b_router": nrm(ks[19], (DEPTH, N_EXPERTS), 0.01),
        "moe_w_gate_up": nrm(ks[20], (DEPTH, N_EXPERTS, D, 2 * D_FF_EXPERT), D ** -0.5),
        "moe_b_gate_up": nrm(ks[21], (DEPTH, N_EXPERTS, 2 * D_FF_EXPERT), 0.01),
        "moe_w_down": nrm(ks[22], (DEPTH, N_EXPERTS, D_FF_EXPERT, D), D_FF_EXPERT ** -0.5),
        "moe_b_down": nrm(ks[23], (DEPTH, N_EXPERTS, D), 0.01),
    }


def reference(x, c, positions, ada_w, ada_b, mix_norm_g, ffn_norm_g, final_norm_g,
              diff_w_in, diff_lambda, diff_subln_g, diff_w_out,
              mla_w_in, mla_q_norm_g, mla_kv_norm_g, mla_w_uq, mla_w_ukv, mla_w_out,
              moe_w_router, moe_b_router, moe_w_gate_up, moe_b_gate_up, moe_w_down, moe_b_down):
    cos_d, sin_d = _rope_tables(positions, DIFF_ROT)
    cos_m, sin_m = _rope_tables(positions, MLA_ROPE)
    c_act = jax.nn.silu(c)
    h = x
    for i in range(DEPTH):
        mod = c_act @ ada_w[i] + ada_b[i]
        shift_a, scale_a, gate_a, shift_f, scale_f, gate_f = [m[:, None, :] for m in jnp.split(mod, N_MOD, axis=-1)]
        u = _rms_norm(h, mix_norm_g[i]) * (1.0 + scale_a) + shift_a
        j = i // N_MIXERS
        if i % N_MIXERS == 0:
            lambda_init = 0.8 - 0.6 * math.exp(-0.3 * i)
            y = _diff_attention(u, diff_w_in[j], diff_lambda[j], diff_subln_g[j], diff_w_out[j],
                                cos_d, sin_d, lambda_init)
        else:
            y = _mla(u, mla_w_in[j], mla_q_norm_g[j], mla_kv_norm_g[j], mla_w_uq[j], mla_w_ukv[j],
                     mla_w_out[j], cos_m, sin_m)
        h = h + gate_a * y
        u = _rms_norm(h, ffn_norm_g[i]) * (1.0 + scale_f) + shift_f
        h = h + gate_f * _moe(u, moe_w_router[i], moe_b_router[i], moe_w_gate_up[i], moe_b_gate_up[i],
                              moe_w_down[i], moe_b_down[i])
    return _rms_norm(h, final_norm_g)
```

```python
import functools
import math

import jax
import jax.numpy as jnp
from jax import lax
from jax.experimental import pallas as pl
from jax.experimental.pallas import tpu as pltpu

F32 = jnp.float32
BF16 = jnp.bfloat16

CHUNK = 64
N_MIXERS = 2
ROPE_THETA = 500000.0
RMS_EPS = 1e-6
N_MOD = 6
DIFF_HEAD_DIM = 128
MLA_V = 128
MLA_Q_RANK = 512
MLA_KV_RANK = 512
MLA_NOPE = 128
MLA_ROPE = 64
N_EXPERTS = 32
TOP_K = 4
SWIGLU_LIMIT = 7.0
SWIGLU_ALPHA = 1.702

LANES = 128
SUBLANES = 8
V7X_VMEM_BYTES = 64 * 1024 * 1024
VMEM_LIMIT = 56 * 1024 * 1024

NEG_BIG = -1e30

ATT_T = 256
PAIR = 256
EXPERT_TM = 512
EXPERT_TF = 256
ROUTE_TB = 512
DISPATCH_TT = 512
COMBINE_TT = 128


def _cparams(sem, **kw):
    return pltpu.CompilerParams(dimension_semantics=sem, vmem_limit_bytes=VMEM_LIMIT, **kw)


def _sigmoid(x):
    return 1.0 / (1.0 + jnp.exp(-x))


def _rms(x, g):
    ms = jnp.mean(x * x, axis=-1, keepdims=True)
    return x * lax.rsqrt(ms + RMS_EPS) * g


def _norm_mod(x, g, scale, shift):
    return _rms(x, g) * (1.0 + scale) + shift


def _rope128(x, c, sa, sb, half):
    return x * c + pltpu.roll(x, LANES - half, 1) * sa + pltpu.roll(x, half, 1) * sb


MOD_TN = 512
MOD_KC = 64


def _mod_kernel(cb_ref, w_ref, b_ref, o_ref, cs_ref):
    nb = cb_ref.shape[0]
    d = cb_ref.shape[1]
    nj = MOD_TN // LANES

    @pl.when((pl.program_id(0) == 0) & (pl.program_id(1) == 0))
    def _():
        c = cb_ref[...]
        cs_ref[...] = c * _sigmoid(c)

    def body(kc, accs):
        k0 = pl.multiple_of(kc * MOD_KC, MOD_KC)
        w = w_ref[0, pl.ds(k0, MOD_KC), :]
        out = []
        for b in range(nb):
            cb = cs_ref[b, pl.ds(k0, MOD_KC), :].reshape(MOD_KC // SUBLANES, SUBLANES, LANES)
            row = []
            for jj in range(nj):
                w3 = w[:, jj * LANES:(jj + 1) * LANES].reshape(MOD_KC // SUBLANES, SUBLANES, LANES)
                row.append(accs[b][jj] + jnp.sum(w3 * cb, axis=0))
            out.append(tuple(row))
        return tuple(out)

    zero = jnp.zeros((SUBLANES, LANES), F32)
    init = tuple(tuple(zero for _ in range(nj)) for _ in range(nb))
    accs = lax.fori_loop(0, d // MOD_KC, body, init)
    for b in range(nb):
        for jj in range(nj):
            r = jnp.sum(accs[b][jj], axis=0, keepdims=True)
            o_ref[0, b:b + 1, jj * LANES:(jj + 1) * LANES] = r + b_ref[0, :, jj * LANES:(jj + 1) * LANES]


def _mod_call(c, ada_w, ada_b):
    depth, d, n6 = ada_w.shape
    nb = c.shape[0]
    cb = jnp.broadcast_to(c[:, :, None], (nb, d, LANES))
    return pl.pallas_call(
        _mod_kernel,
        grid=(depth, n6 // MOD_TN),
        in_specs=[
            pl.BlockSpec((nb, d, LANES), lambda l, j: (0, 0, 0)),
            pl.BlockSpec((1, d, MOD_TN), lambda l, j: (l, 0, j)),
            pl.BlockSpec((1, 1, MOD_TN), lambda l, j: (l, 0, j)),
        ],
        out_specs=pl.BlockSpec((1, nb, MOD_TN), lambda l, j: (l, 0, j)),
        out_shape=jax.ShapeDtypeStruct((depth, nb, n6), F32),
        scratch_shapes=[pltpu.VMEM((nb, d, LANES), F32)],
        compiler_params=_cparams(("arbitrary", "arbitrary")),
        name="adaln_mod",
    )(cb, ada_w, ada_b.reshape(depth, 1, n6))


DIN_TM = 512
DIN_TN = 1024


def _diff_in_kernel(h_ref, g_ref, sc_ref, sh_ref, w_ref, tab_ref, o_ref, u_ref, *, nqb):
    j = pl.program_id(1)
    nh = DIN_TN // PAIR

    @pl.when(j == 0)
    def _():
        u_ref[...] = _norm_mod(h_ref[...], g_ref[...], sc_ref[0], sh_ref[0]).astype(BF16)

    acc = jnp.dot(u_ref[...], w_ref[0], preferred_element_type=F32)

    @pl.when(j < 2 * nqb)
    def _():
        c = tab_ref[0, 0]
        sa = tab_ref[0, 1]
        sb = tab_ref[0, 2]
        for hh in range(nh):
            for half in range(2):
                lo = (2 * hh + half) * LANES
                y = _rope128(acc[:, lo:lo + LANES], c, sa, sb, DIFF_HEAD_DIM // 8)
                o_ref[0, hh, :, half * LANES:(half + 1) * LANES] = y.astype(BF16)

    @pl.when(j >= 2 * nqb)
    def _():
        for hh in range(nh):
            o_ref[0, hh] = acc[:, hh * PAIR:(hh + 1) * PAIR].astype(BF16)


def _diff_in_call(h, g, scale, shift, w_bf, layer, tab, seq):
    n, d = h.shape
    nout = w_bf.shape[2]
    third = nout // 3
    nqb = third // DIN_TN
    heads = third // PAIR
    hpb = DIN_TN // PAIR
    tiles_per_b = seq // DIN_TM
    return pl.pallas_call(
        functools.partial(_diff_in_kernel, nqb=nqb),
        grid=(n // DIN_TM, nout // DIN_TN),
        in_specs=[
            pl.BlockSpec((DIN_TM, d), lambda i, j: (i, 0)),
            pl.BlockSpec((1, d), lambda i, j: (0, 0)),
            pl.BlockSpec((1, 1, d), lambda i, j: (i // tiles_per_b, 0, 0)),
            pl.BlockSpec((1, 1, d), lambda i, j: (i // tiles_per_b, 0, 0)),
            pl.BlockSpec((1, d, DIN_TN), lambda i, j: (layer, 0, j)),
            pl.BlockSpec((1, 3, DIN_TM, LANES), lambda i, j: (jnp.minimum(j // nqb, 1), 0, i, 0)),
        ],
        out_specs=pl.BlockSpec((1, hpb, DIN_TM, PAIR), lambda i, j: (j // nqb, j % nqb, i, 0)),
        out_shape=jax.ShapeDtypeStruct((3, heads, n, PAIR), BF16),
        scratch_shapes=[pltpu.VMEM((DIN_TM, d), BF16)],
        compiler_params=_cparams(("parallel", "arbitrary")),
        name="diff_in",
    )(h, g, scale, shift, w_bf, tab)


def _chunk_mask():
    r = lax.broadcasted_iota(jnp.int32, (ATT_T, ATT_T), 0) // CHUNK
    c = lax.broadcasted_iota(jnp.int32, (ATT_T, ATT_T), 1) // CHUNK
    return c <= r


def _softmax_step(s, m, l, acc_ref, v):
    m_new = jnp.maximum(m, jnp.max(s, axis=-1, keepdims=True))
    alpha = jnp.exp(m - m_new)
    p = jnp.exp(s - m_new)
    l_new = alpha * l + jnp.sum(p, axis=-1, keepdims=True)
    acc_ref[...] = alpha * acc_ref[...] + jnp.dot(p.astype(BF16), v, preferred_element_type=F32)
    return m_new, l_new


def _qk(q, k):
    return lax.dot_general(q, k, (((1,), (1,)), ((), ())), preferred_element_type=F32)


def _diff_attn_kernel(q_ref, k_ref, v_ref, lam_ref, g_ref, o_ref, acc1_ref, acc2_ref, *, lambda_init):
    qi = pl.program_id(1)
    heads = q_ref.shape[1]
    lam = lam_ref[...]
    lam_full = (jnp.exp(jnp.sum(lam[0:1] * lam[1:2], axis=-1, keepdims=True))
                - jnp.exp(jnp.sum(lam[2:3] * lam[3:4], axis=-1, keepdims=True)) + lambda_init)
    mask = _chunk_mask()
    m0 = jnp.full((ATT_T, 1), NEG_BIG, F32)
    l0 = jnp.zeros((ATT_T, 1), F32)

    def head_body(h, carry):
        q1 = q_ref[0, h, :, 0:LANES]
        q2 = q_ref[0, h, :, LANES:PAIR]
        acc1_ref[...] = jnp.zeros_like(acc1_ref)
        acc2_ref[...] = jnp.zeros_like(acc2_ref)

        def step(kt, st, masked):
            m1, l1, m2, l2 = st
            k0 = pl.multiple_of(kt * ATT_T, ATT_T)
            k1 = k_ref[0, h, pl.ds(k0, ATT_T), 0:LANES]
            k2 = k_ref[0, h, pl.ds(k0, ATT_T), LANES:PAIR]
            v = v_ref[0, h, pl.ds(k0, ATT_T), :]
            s1 = _qk(q1, k1)
            s2 = _qk(q2, k2)
            if masked:
                s1 = jnp.where(mask, s1, NEG_BIG)
                s2 = jnp.where(mask, s2, NEG_BIG)
            m1, l1 = _softmax_step(s1, m1, l1, acc1_ref, v)
            m2, l2 = _softmax_step(s2, m2, l2, acc2_ref, v)
            return m1, l1, m2, l2

        st = lax.fori_loop(0, qi, lambda kt, st: step(kt, st, False), (m0, l0, m0, l0))
        _, l1, _, l2 = step(qi, st, True)
        o = acc1_ref[...] / l1 - lam_full * (acc2_ref[...] / l2)
        o = _rms(o, g_ref[...]) * (1.0 - lambda_init)
        o_ref[h] = o.astype(BF16)
        return carry

    lax.fori_loop(0, heads, head_body, 0)


def _diff_attn_call(qkv, lam, g, lambda_init, batch, seq):
    _, heads, n, _ = qkv.shape
    nq = seq // ATT_T
    kv_spec = lambda which: pl.BlockSpec((1, heads, seq, PAIR), lambda b, qi: (which, 0, b, 0),
                                         pipeline_mode=pl.Buffered(1))
    return pl.pallas_call(
        functools.partial(_diff_attn_kernel, lambda_init=lambda_init),
        grid=(batch, nq),
        in_specs=[
            pl.BlockSpec((1, heads, ATT_T, PAIR), lambda b, qi: (0, 0, b * nq + qi, 0)),
            kv_spec(1),
            kv_spec(2),
            pl.BlockSpec((4, DIFF_HEAD_DIM), lambda b, qi: (0, 0)),
            pl.BlockSpec((1, PAIR), lambda b, qi: (0, 0)),
        ],
        out_specs=pl.BlockSpec((heads, ATT_T, PAIR), lambda b, qi: (0, b * nq + qi, 0)),
        out_shape=jax.ShapeDtypeStruct((heads, n, PAIR), BF16),
        scratch_shapes=[pltpu.VMEM((ATT_T, PAIR), F32), pltpu.VMEM((ATT_T, PAIR), F32)],
        compiler_params=_cparams(("parallel", "arbitrary")),
        name="diff_attn",
    )(qkv, qkv, qkv, lam, g)


MLA_TM = 256


def _mla_in_kernel(h_ref, g_ref, sc_ref, sh_ref, win_ref, gq_ref, gkv_ref, wuq_ref, wukv_ref, tab_ref,
                   proj_ref, kr_ref, *, scale):
    u = _norm_mod(h_ref[...], g_ref[...], sc_ref[0], sh_ref[0]).astype(BF16)
    lat = jnp.dot(u, win_ref[0], preferred_element_type=F32)
    qn = _rms(lat[:, 0:MLA_Q_RANK], gq_ref[...]).astype(BF16)
    kvn = _rms(lat[:, MLA_Q_RANK:MLA_Q_RANK + MLA_KV_RANK], gkv_ref[...]).astype(BF16)
    half = MLA_ROPE // 2
    kr = lat[:, MLA_Q_RANK + MLA_KV_RANK:MLA_Q_RANK + MLA_KV_RANK + LANES]
    kr_ref[...] = _rope128(kr, tab_ref[1, 0], tab_ref[1, 1], tab_ref[1, 2], half).astype(BF16)

    npair = proj_ref.shape[1]
    width = npair * PAIR
    q = jnp.dot(qn, wuq_ref[0], preferred_element_type=F32)
    c = tab_ref[0, 0]
    sa = tab_ref[0, 1]
    sb = tab_ref[0, 2]
    for gp in range(npair):
        proj_ref[0, gp] = (q[:, gp * PAIR:(gp + 1) * PAIR] * scale).astype(BF16)
        for a in range(2):
            lo = width + gp * PAIR + a * LANES
            y = _rope128(q[:, lo:lo + LANES], c, sa, sb, half)
            proj_ref[1, gp, :, a * LANES:(a + 1) * LANES] = y.astype(BF16)
    kv = jnp.dot(kvn, wukv_ref[0], preferred_element_type=F32)
    for gp in range(npair):
        proj_ref[2, gp] = kv[:, gp * PAIR:(gp + 1) * PAIR].astype(BF16)
        proj_ref[3, gp] = kv[:, width + gp * PAIR:width + (gp + 1) * PAIR].astype(BF16)


def _mla_in_call(h, g, scale_a, shift_a, w_in, gq, gkv, w_uq, w_ukv, layer, tab, seq, scale):
    n, d = h.shape
    lat_w = w_in.shape[2]
    up_w = w_uq.shape[2]
    npair = up_w // (2 * PAIR)
    tiles_per_b = seq // MLA_TM
    const = lambda shape: pl.BlockSpec(shape, lambda i: (layer,) + (0,) * (len(shape) - 1),
                                       pipeline_mode=pl.Buffered(1))
    return pl.pallas_call(
        functools.partial(_mla_in_kernel, scale=scale),
        grid=(n // MLA_TM,),
        in_specs=[
            pl.BlockSpec((MLA_TM, d), lambda i: (i, 0)),
            pl.BlockSpec((1, d), lambda i: (0, 0)),
            pl.BlockSpec((1, 1, d), lambda i: (i // tiles_per_b, 0, 0)),
            pl.BlockSpec((1, 1, d), lambda i: (i // tiles_per_b, 0, 0)),
            const((1, d, lat_w)),
            pl.BlockSpec((1, MLA_Q_RANK), lambda i: (0, 0)),
            pl.BlockSpec((1, MLA_KV_RANK), lambda i: (0, 0)),
            const((1, MLA_Q_RANK, up_w)),
            const((1, MLA_KV_RANK, up_w)),
            pl.BlockSpec((2, 3, MLA_TM, LANES), lambda i: (0, 0, i, 0)),
        ],
        out_specs=[
            pl.BlockSpec((4, npair, MLA_TM, PAIR), lambda i: (0, 0, i, 0)),
            pl.BlockSpec((MLA_TM, LANES), lambda i: (i, 0)),
        ],
        out_shape=[
            jax.ShapeDtypeStruct((4, npair, n, PAIR), BF16),
            jax.ShapeDtypeStruct((n, LANES), BF16),
        ],
        compiler_params=_cparams(("parallel",)),
        name="mla_in",
    )(h, g, scale_a, shift_a, w_in, gq, gkv, w_uq, w_ukv, tab)


def _mla_attn_kernel(q_ref, kv_ref, kr_ref, o_ref, acc_ref):
    qi = pl.program_id(1)
    npair = q_ref.shape[1]
    mask = _chunk_mask()
    m0 = jnp.full((ATT_T, 1), NEG_BIG, F32)
    l0 = jnp.zeros((ATT_T, 1), F32)

    def pair_body(gp, carry):
        for a in range(2):
            sl = slice(a * LANES, (a + 1) * LANES)
            qc = jnp.concatenate([q_ref[0, gp, :, sl], q_ref[1, gp, :, sl]], axis=-1)
            acc_ref[...] = jnp.zeros_like(acc_ref)

            def step(kt, st, masked, sl=sl, qc=qc):
                m, l = st
                k0 = pl.multiple_of(kt * ATT_T, ATT_T)
                kc = jnp.concatenate([kv_ref[0, gp, pl.ds(k0, ATT_T), sl],
                                      kr_ref[pl.ds(k0, ATT_T), :]], axis=-1)
                v = kv_ref[1, gp, pl.ds(k0, ATT_T), sl]
                s = _qk(qc, kc)
                if masked:
                    s = jnp.where(mask, s, NEG_BIG)
                return _softmax_step(s, m, l, acc_ref, v)

            st = lax.fori_loop(0, qi, lambda kt, st, step=step: step(kt, st, False), (m0, l0))
            _, l = step(qi, st, True)
            o_ref[gp, :, sl] = (acc_ref[...] / l).astype(BF16)
        return carry

    lax.fori_loop(0, npair, pair_body, 0)


def _mla_attn_call(proj, kr, batch, seq):
    _, npair, n, _ = proj.shape
    nq = seq // ATT_T
    return pl.pallas_call(
        _mla_attn_kernel,
        grid=(batch, nq),
        in_specs=[
            pl.BlockSpec((2, npair, ATT_T, PAIR), lambda b, qi: (0, 0, b * nq + qi, 0)),
            pl.BlockSpec((2, npair, seq, PAIR), lambda b, qi: (1, 0, b, 0), pipeline_mode=pl.Buffered(1)),
            pl.BlockSpec((seq, LANES), lambda b, qi: (b, 0), pipeline_mode=pl.Buffered(1)),
        ],
        out_specs=pl.BlockSpec((npair, ATT_T, PAIR), lambda b, qi: (0, b * nq + qi, 0)),
        out_shape=jax.ShapeDtypeStruct((npair, n, PAIR), BF16),
        scratch_shapes=[pltpu.VMEM((ATT_T, LANES), F32)],
        compiler_params=_cparams(("parallel", "arbitrary")),
        name="mla_attn",
    )(proj, proj, kr)


AOUT_TM = 256


def _attn_out_kernel(o_ref, w_ref, h_ref, gate_ref, g_ref, sc_ref, sh_ref, wr_ref, br_ref,
                     hn_ref, u_ref, idx_ref, gw_ref):
    ns = o_ref.shape[0]
    y = jnp.dot(o_ref[0], w_ref[0, 0:PAIR, :], preferred_element_type=F32)
    for s in range(1, ns):
        y = y + jnp.dot(o_ref[s], w_ref[0, s * PAIR:(s + 1) * PAIR, :], preferred_element_type=F32)
    hn = h_ref[...] + gate_ref[0] * y
    hn_ref[...] = hn
    u = _norm_mod(hn, g_ref[...], sc_ref[0], sh_ref[0])
    u_ref[...] = u
    logits = jnp.dot(u, wr_ref[0], preferred_element_type=F32, precision=lax.Precision.HIGHEST) + br_ref[0]
    tm, ne = logits.shape
    eio = lax.broadcasted_iota(jnp.int32, (tm, ne), 1)
    lane = lax.broadcasted_iota(jnp.int32, (tm, LANES), 1)
    idx_out = jnp.zeros((tm, LANES), jnp.int32)
    val_out = jnp.zeros((tm, LANES), F32)
    l = logits
    v0 = None
    den = None
    for k in range(TOP_K):
        m = jnp.max(l, axis=-1, keepdims=True)
        i = jnp.min(jnp.where(l == m, eio, ne), axis=-1, keepdims=True)
        l = jnp.where(eio == i, -jnp.inf, l)
        if k == 0:
            v0 = m
        e = jnp.exp(m - v0)
        den = e if den is None else den + e
        idx_out = jnp.where(lane == k, i, idx_out)
        val_out = jnp.where(lane == k, e, val_out)
    idx_ref[...] = idx_out
    gw_ref[...] = val_out / den


def _attn_out_call(o, w_out, layer, h, gate, g, scale, shift, w_router, b_router, seq):
    ns, n, _ = o.shape
    d = h.shape[1]
    ne = w_router.shape[2]
    tiles_per_b = seq // AOUT_TM
    perb = lambda: pl.BlockSpec((1, 1, d), lambda i: (i // tiles_per_b, 0, 0))
    return pl.pallas_call(
        _attn_out_kernel,
        grid=(n // AOUT_TM,),
        in_specs=[
            pl.BlockSpec((ns, AOUT_TM, PAIR), lambda i: (0, i, 0)),
            pl.BlockSpec((1, d, d), lambda i: (layer, 0, 0), pipeline_mode=pl.Buffered(1)),
            pl.BlockSpec((AOUT_TM, d), lambda i: (i, 0)),
            perb(),
            pl.BlockSpec((1, d), lambda i: (0, 0)),
            perb(),
            perb(),
            pl.BlockSpec((1, d, ne), lambda i: (0, 0, 0)),
            pl.BlockSpec((1, 1, ne), lambda i: (0, 0, 0)),
        ],
        out_specs=[
            pl.BlockSpec((AOUT_TM, d), lambda i: (i, 0)),
            pl.BlockSpec((AOUT_TM, d), lambda i: (i, 0)),
            pl.BlockSpec((AOUT_TM, LANES), lambda i: (i, 0)),
            pl.BlockSpec((AOUT_TM, LANES), lambda i: (i, 0)),
        ],
        out_shape=[
            jax.ShapeDtypeStruct((n, d), F32),
            jax.ShapeDtypeStruct((n, d), F32),
            jax.ShapeDtypeStruct((n, LANES), jnp.int32),
            jax.ShapeDtypeStruct((n, LANES), F32),
        ],
        compiler_params=_cparams(("parallel",)),
        name="attn_out",
    )(o, w_out, h, gate, g, scale, shift, w_router, b_router)


def _route_kernel(idx_ref, dest_ref, pt_ref, carry_ref, ps_ref):
    ph = pl.program_id(0)
    t = pl.program_id(1)
    nt = pl.num_programs(1)
    tb = idx_ref.shape[0]
    idx = idx_ref[...]
    eio = lax.broadcasted_iota(jnp.int32, (tb, LANES), 1)
    sel = [eio == idx[:, k:k + 1] for k in range(TOP_K)]
    oh = jnp.where(sel[0], 1.0, 0.0)
    for k in range(1, TOP_K):
        oh = oh + jnp.where(sel[k], 1.0, 0.0)

    @pl.when(t == 0)
    def _():
        carry_ref[...] = jnp.zeros_like(carry_ref)

    @pl.when((ph == 0) & (t == nt - 1))
    def _():
        cnt = carry_ref[...] + jnp.sum(oh, axis=0, keepdims=True)
        tiles = jnp.floor((cnt + (EXPERT_TM - 1)) * (1.0 / EXPERT_TM))
        r = lax.broadcasted_iota(jnp.int32, (LANES, LANES), 0)
        c = lax.broadcasted_iota(jnp.int32, (LANES, LANES), 1)
        upper = jnp.where(r < c, 1.0, 0.0).astype(BF16)
        t8 = jnp.broadcast_to(tiles, (SUBLANES, LANES))
        start = jnp.dot(t8.astype(BF16), upper, preferred_element_type=F32)
        ps_ref[...] = start[0:1] * EXPERT_TM
        pt_ref[...] = t8

    @pl.when(ph == 1)
    def _():
        r = lax.broadcasted_iota(jnp.int32, (tb, tb), 0)
        c = lax.broadcasted_iota(jnp.int32, (tb, tb), 1)
        lower = jnp.where(c < r, 1.0, 0.0).astype(BF16)
        before = jnp.dot(lower, oh.astype(BF16), preferred_element_type=F32)
        base = before + carry_ref[...] + ps_ref[...]
        out = jnp.zeros((tb, LANES), jnp.int32)
        for k in range(TOP_K):
            dk = jnp.sum(jnp.where(sel[k], base, 0.0), axis=-1, keepdims=True)
            out = jnp.where(eio == k, dk.astype(jnp.int32), out)
        dest_ref[...] = out

    carry_ref[...] += jnp.sum(oh, axis=0, keepdims=True)


def _route_call(idx):
    n = idx.shape[0]
    return pl.pallas_call(
        _route_kernel,
        grid=(2, n // ROUTE_TB),
        in_specs=[pl.BlockSpec((ROUTE_TB, LANES), lambda p, t: (t, 0))],
        out_specs=[
            pl.BlockSpec((ROUTE_TB, LANES), lambda p, t: (p * t, 0)),
            pl.BlockSpec((SUBLANES, LANES), lambda p, t: (0, 0)),
        ],
        out_shape=[
            jax.ShapeDtypeStruct((n, LANES), jnp.int32),
            jax.ShapeDtypeStruct((SUBLANES, LANES), F32),
        ],
        scratch_shapes=[pltpu.VMEM((1, LANES), F32), pltpu.VMEM((1, LANES), F32)],
        compiler_params=_cparams(("arbitrary", "arbitrary")),
        name="route",
    )(idx)


def _dispatch_kernel(dest_ref, u_ref, xin_ref, xb_ref, sem):
    del xin_ref
    i = pl.program_id(0)
    tt = u_ref.shape[0]

    def row_copy(t, d):
        return pltpu.make_async_copy(u_ref.at[pl.ds(t, 1)], xb_ref.at[pl.ds(d, 1)], sem)

    def issue(t, carry):
        for k in range(TOP_K):
            row_copy(t, dest_ref[(i * tt + t) * TOP_K + k]).start()
        return carry

    def drain(t, carry):
        for k in range(TOP_K):
            row_copy(t, dest_ref[(i * tt + t) * TOP_K + k]).wait()
        return carry

    lax.fori_loop(0, tt, issue, 0)
    lax.fori_loop(0, tt, drain, 0)


def _dispatch_call(dest_flat, u, rows):
    n, d = u.shape
    zeros = jnp.zeros((rows, d), u.dtype)
    return pl.pallas_call(
        _dispatch_kernel,
        grid_spec=pltpu.PrefetchScalarGridSpec(
            num_scalar_prefetch=1,
            grid=(n // DISPATCH_TT,),
            in_specs=[
                pl.BlockSpec((DISPATCH_TT, d), lambda i, dest: (i, 0)),
                pl.BlockSpec(memory_space=pl.ANY),
            ],
            out_specs=pl.BlockSpec(memory_space=pl.ANY),
            scratch_shapes=[pltpu.SemaphoreType.DMA(())],
        ),
        out_shape=jax.ShapeDtypeStruct((rows, d), u.dtype),
        input_output_aliases={2: 0},
        compiler_params=_cparams(("arbitrary",), has_side_effects=True),
        name="dispatch",
    )(dest_flat, u, zeros)


def _expert_kernel(te_ref, nu_ref, x_ref, wg_ref, wu_ref, bg_ref, bu_ref, wd_ref, bd_ref, o_ref, xbf_ref):
    del te_ref
    i = pl.program_id(0)
    j = pl.program_id(1)

    @pl.when(i < nu_ref[0])
    def _():
        @pl.when(j == 0)
        def _():
            xbf_ref[...] = x_ref[...].astype(BF16)

        x = xbf_ref[...]
        gate = jnp.dot(x, wg_ref[0, 0].astype(BF16), preferred_element_type=F32) + bg_ref[0, 0]
        up = jnp.dot(x, wu_ref[0, 0].astype(BF16), preferred_element_type=F32) + bu_ref[0, 0]
        gate = jnp.minimum(gate, SWIGLU_LIMIT)
        up = jnp.clip(up, -SWIGLU_LIMIT, SWIGLU_LIMIT)
        glu = gate * _sigmoid(gate * SWIGLU_ALPHA)
        a = ((up + 1.0) * glu).astype(BF16)
        y = jnp.dot(a, wd_ref[0, 0].astype(BF16), preferred_element_type=F32)

        @pl.when(j == 0)
        def _():
            o_ref[...] = y + bd_ref[0, 0]

        @pl.when(j > 0)
        def _():
            o_ref[...] += y

    @pl.when((i >= nu_ref[0]) & (j == 0))
    def _():
        o_ref[...] = jnp.zeros_like(o_ref)


def _expert_call(tile_e, n_used, xb, w_gu, b_gu, w_down, b_down, layer):
    rows, d = xb.shape
    ne = w_gu.shape[1]
    f = w_down.shape[2]
    nf = f // EXPERT_TF
    n_tiles = rows // EXPERT_TM

    def tile(i, nu):
        return jnp.minimum(i, nu[0] - 1)

    def fcol(i, j, nu):
        return jnp.where(i < nu[0], j, nf - 1)

    return pl.pallas_call(
        _expert_kernel,
        grid_spec=pltpu.PrefetchScalarGridSpec(
            num_scalar_prefetch=2,
            grid=(n_tiles, nf),
            in_specs=[
                pl.BlockSpec((EXPERT_TM, d), lambda i, j, te, nu: (tile(i, nu), 0)),
                pl.BlockSpec((1, 1, d, EXPERT_TF), lambda i, j, te, nu: (layer, te[tile(i, nu)], 0, fcol(i, j, nu))),
                pl.BlockSpec((1, 1, d, EXPERT_TF),
                             lambda i, j, te, nu: (layer, te[tile(i, nu)], 0, nf + fcol(i, j, nu))),
                pl.BlockSpec((1, 1, 1, EXPERT_TF), lambda i, j, te, nu: (layer, te[tile(i, nu)], 0, fcol(i, j, nu))),
                pl.BlockSpec((1, 1, 1, EXPERT_TF),
                             lambda i, j, te, nu: (layer, te[tile(i, nu)], 0, nf + fcol(i, j, nu))),
                pl.BlockSpec((1, 1, EXPERT_TF, d), lambda i, j, te, nu: (layer, te[tile(i, nu)], fcol(i, j, nu), 0)),
                pl.BlockSpec((1, 1, 1, d), lambda i, j, te, nu: (layer, te[tile(i, nu)], 0, 0)),
            ],
            out_specs=pl.BlockSpec((EXPERT_TM, d), lambda i, j, te, nu: (i, 0)),
            scratch_shapes=[pltpu.VMEM((EXPERT_TM, d), BF16)],
        ),
        out_shape=jax.ShapeDtypeStruct((rows, d), F32),
        compiler_params=_cparams(("arbitrary", "arbitrary")),
        name="experts",
    )(tile_e, n_used, xb, w_gu, w_gu, b_gu.reshape(b_gu.shape[0], ne, 1, 2 * f),
      b_gu.reshape(b_gu.shape[0], ne, 1, 2 * f), w_down, b_down.reshape(b_down.shape[0], ne, 1, d))


def _combine_kernel(dest_ref, yb_ref, gw_ref, h_ref, gate_ref, g_ref, o_ref, buf_ref, sem, *, final):
    i = pl.program_id(0)
    tt = h_ref.shape[0]

    def row_copy(t, k, d):
        return pltpu.make_async_copy(yb_ref.at[pl.ds(d, 1)], buf_ref.at[k, pl.ds(t, 1)], sem)

    def issue(t, carry):
        for k in range(TOP_K):
            row_copy(t, k, dest_ref[(i * tt + t) * TOP_K + k]).start()
        return carry

    def drain(t, carry):
        for k in range(TOP_K):
            row_copy(t, k, dest_ref[(i * tt + t) * TOP_K + k]).wait()
        return carry

    lax.fori_loop(0, tt, issue, 0)
    lax.fori_loop(0, tt, drain, 0)
    gw = gw_ref[...]
    y = gw[:, 0:1] * buf_ref[0]
    for k in range(1, TOP_K):
        y = y + gw[:, k:k + 1] * buf_ref[k]
    hn = h_ref[...] + gate_ref[0] * y
    if final:
        hn = _rms(hn, g_ref[...])
    o_ref[...] = hn


def _combine_call(dest_flat, yb, gw, h, gate, g_final, seq, final):
    n, d = h.shape
    tiles_per_b = seq // COMBINE_TT
    return pl.pallas_call(
        functools.partial(_combine_kernel, final=final),
        grid_spec=pltpu.PrefetchScalarGridSpec(
            num_scalar_prefetch=1,
            grid=(n // COMBINE_TT,),
            in_specs=[
                pl.BlockSpec(memory_space=pl.ANY),
                pl.BlockSpec((COMBINE_TT, LANES), lambda i, dest: (i, 0)),
                pl.BlockSpec((COMBINE_TT, d), lambda i, dest: (i, 0)),
                pl.BlockSpec((1, 1, d), lambda i, dest: (i // tiles_per_b, 0, 0)),
                pl.BlockSpec((1, d), lambda i, dest: (0, 0)),
            ],
            out_specs=pl.BlockSpec((COMBINE_TT, d), lambda i, dest: (i, 0)),
            scratch_shapes=[pltpu.VMEM((TOP_K, COMBINE_TT, d), F32), pltpu.SemaphoreType.DMA(())],
        ),
        out_shape=jax.ShapeDtypeStruct((n, d), F32),
        compiler_params=_cparams(("arbitrary",)),
        name="combine",
    )(dest_flat, yb, gw, h, gate, g_final)


def _rope_tables(positions, rot_dim, scale):
    half = rot_dim // 2
    inv_freq = ROPE_THETA ** (-jnp.arange(0, rot_dim, 2, dtype=F32) / rot_dim)
    ang = positions.reshape(-1).astype(F32)[:, None] * inv_freq
    cos, sin = jnp.cos(ang), jnp.sin(ang)
    n = ang.shape[0]
    c = jnp.concatenate([cos, cos, jnp.ones((n, LANES - rot_dim), F32)], axis=1)
    sa = jnp.concatenate([-sin, jnp.zeros((n, LANES - half), F32)], axis=1)
    sb = jnp.concatenate([jnp.zeros((n, half), F32), sin, jnp.zeros((n, LANES - rot_dim), F32)], axis=1)
    t = jnp.stack([c, sa, sb])
    return jnp.stack([t * scale, t])


def _moe(layer, h, u, idx, gw, gate_f, g_final, final, seq, w_gu, b_gu, w_down, b_down):
    n, d = h.shape
    rows = -(-(n * TOP_K + N_EXPERTS * (EXPERT_TM - 1)) // EXPERT_TM) * EXPERT_TM
    n_tiles = rows // EXPERT_TM
    dest, ptiles = _route_call(idx)
    dest_flat = dest[:, :TOP_K].reshape(-1)
    tile_end = jnp.cumsum(ptiles[0, :N_EXPERTS].astype(jnp.int32))
    tile_e = jnp.minimum(jnp.searchsorted(tile_end, jnp.arange(n_tiles, dtype=jnp.int32), side='right'),
                         N_EXPERTS - 1).astype(jnp.int32)
    n_used = tile_end[-1:].astype(jnp.int32)
    xb = _dispatch_call(dest_flat, u, rows)
    yb = _expert_call(tile_e, n_used, xb, w_gu, b_gu, w_down, b_down, layer)
    return _combine_call(dest_flat, yb, gw, h, gate_f, g_final, seq, final)


def kernel(x, c, positions, ada_w, ada_b, mix_norm_g, ffn_norm_g, final_norm_g, diff_w_in, diff_lambda, diff_subln_g, diff_w_out, mla_w_in, mla_q_norm_g, mla_kv_norm_g, mla_w_uq, mla_w_ukv, mla_w_out, moe_w_router, moe_b_router, moe_w_gate_up, moe_b_gate_up, moe_w_down, moe_b_down):
    batch, seq, d = x.shape
    depth = ada_w.shape[0]
    n = batch * seq
    assert seq % ATT_T == 0 and ATT_T % CHUNK == 0 and d % PAIR == 0
    assert n % DISPATCH_TT == 0 and n % ROUTE_TB == 0 and seq % DIN_TM == 0

    diff_scale = DIFF_HEAD_DIM ** -0.5
    mla_scale = (MLA_NOPE + MLA_ROPE) ** -0.5
    tab_d = _rope_tables(positions, DIFF_HEAD_DIM // 4, diff_scale)
    tab_m = _rope_tables(positions, MLA_ROPE, mla_scale)

    mod = _mod_call(c, ada_w, ada_b)
    mod = mod.reshape(depth, batch, N_MOD, 1, d)

    diff_w_in_bf = diff_w_in.astype(BF16)
    diff_w_out_bf = diff_w_out.astype(BF16)
    mla_w_out_bf = mla_w_out.astype(BF16)
    nm = mla_w_in.shape[0]
    heads_m = mla_w_out.shape[1] // MLA_V
    lat_pad = LANES - MLA_ROPE
    mla_w_in_bf = jnp.pad(mla_w_in, ((0, 0), (0, 0), (0, lat_pad))).astype(BF16)
    wq = mla_w_uq.reshape(nm, MLA_Q_RANK, heads_m, MLA_NOPE + MLA_ROPE)
    wq_rope = jnp.pad(wq[..., MLA_NOPE:], ((0, 0), (0, 0), (0, 0), (0, LANES - MLA_ROPE)))
    mla_w_uq_bf = jnp.concatenate([wq[..., :MLA_NOPE].reshape(nm, MLA_Q_RANK, -1),
                                   wq_rope.reshape(nm, MLA_Q_RANK, -1)], axis=-1).astype(BF16)
    wkv = mla_w_ukv.reshape(nm, MLA_KV_RANK, heads_m, MLA_NOPE + MLA_V)
    mla_w_ukv_bf = jnp.concatenate([wkv[..., :MLA_NOPE].reshape(nm, MLA_KV_RANK, -1),
                                    wkv[..., MLA_NOPE:].reshape(nm, MLA_KV_RANK, -1)], axis=-1).astype(BF16)

    h = x.reshape(n, d)
    for i in range(depth):
        shift_a, scale_a, gate_a, shift_f, scale_f, gate_f = [mod[i, :, m] for m in range(N_MOD)]
        g_mix = mix_norm_g[i].reshape(1, d)
        g_ffn = ffn_norm_g[i].reshape(1, d)
        j = i // N_MIXERS
        if i % N_MIXERS == 0:
            lambda_init = 0.8 - 0.6 * math.exp(-0.3 * i)
            qkv = _diff_in_call(h, g_mix, scale_a, shift_a, diff_w_in_bf, j, tab_d, seq)
            o = _diff_attn_call(qkv, diff_lambda[j], diff_subln_g[j].reshape(1, -1), lambda_init, batch, seq)
            w_out = diff_w_out_bf
        else:
            proj, kr = _mla_in_call(h, g_mix, scale_a, shift_a, mla_w_in_bf, mla_q_norm_g[j].reshape(1, -1),
                                    mla_kv_norm_g[j].reshape(1, -1), mla_w_uq_bf, mla_w_ukv_bf, j, tab_m, seq,
                                    mla_scale)
            o = _mla_attn_call(proj, kr, batch, seq)
            w_out = mla_w_out_bf
        h, u, idx, gw = _attn_out_call(o, w_out, j, h, gate_a, g_ffn, scale_f, shift_f,
                                       moe_w_router[i:i + 1], moe_b_router[i:i + 1].reshape(1, 1, -1), seq)
        h = _moe(i, h, u, idx, gw, gate_f, final_norm_g.reshape(1, d), i == depth - 1, seq,
                 moe_w_gate_up, moe_b_gate_up, moe_w_down, moe_b_down)
    return h.reshape(batch, seq, d)
```

```python
import functools
import math

import jax
import jax.numpy as jnp
from jax import lax
from jax.experimental import pallas as pl
from jax.experimental.pallas import tpu as pltpu

F32 = jnp.float32
BF16 = jnp.bfloat16

CHUNK = 64
N_MIXERS = 2
ROPE_THETA = 500000.0
RMS_EPS = 1e-6
N_MOD = 6
DIFF_HEAD_DIM = 128
MLA_V = 128
MLA_Q_RANK = 512
MLA_KV_RANK = 512
MLA_NOPE = 128
MLA_ROPE = 64
N_EXPERTS = 32
TOP_K = 4
SWIGLU_LIMIT = 7.0
SWIGLU_ALPHA = 1.702

LANES = 128
SUBLANES = 8
V7X_VMEM_BYTES = 64 * 1024 * 1024
VMEM_LIMIT = 56 * 1024 * 1024

NEG_BIG = -1e30

ATT_T = 256
PAIR = 256
EXPERT_TM = 512
EXPERT_TF = 256
ROUTE_TB = 512
DISPATCH_TT = 512
COMBINE_TT = 128


def _cparams(sem, **kw):
    return pltpu.CompilerParams(dimension_semantics=sem, vmem_limit_bytes=VMEM_LIMIT, **kw)


def _sigmoid(x):
    return 1.0 / (1.0 + jnp.exp(-x))


def _rms(x, g):
    ms = jnp.mean(x * x, axis=-1, keepdims=True)
    return x * lax.rsqrt(ms + RMS_EPS) * g


def _norm_mod(x, g, scale, shift):
    return _rms(x, g) * (1.0 + scale) + shift


def _rope128(x, c, sa, sb, half):
    return x * c + pltpu.roll(x, LANES - half, 1) * sa + pltpu.roll(x, half, 1) * sb


MOD_TN = 512
MOD_KC = 64


def _mod_kernel(cb_ref, w_ref, b_ref, o_ref, cs_ref):
    nb = cb_ref.shape[0]
    d = cb_ref.shape[1]
    nj = MOD_TN // LANES

    @pl.when((pl.program_id(0) == 0) & (pl.program_id(1) == 0))
    def _():
        c = cb_ref[...]
        cs_ref[...] = c * _sigmoid(c)

    def body(kc, accs):
        k0 = pl.multiple_of(kc * MOD_KC, MOD_KC)
        w = w_ref[0, pl.ds(k0, MOD_KC), :]
        out = []
        for b in range(nb):
            cb = cs_ref[b, pl.ds(k0, MOD_KC), :].reshape(MOD_KC // SUBLANES, SUBLANES, LANES)
            row = []
            for jj in range(nj):
                w3 = w[:, jj * LANES:(jj + 1) * LANES].reshape(MOD_KC // SUBLANES, SUBLANES, LANES)
                row.append(accs[b][jj] + jnp.sum(w3 * cb, axis=0))
            out.append(tuple(row))
        return tuple(out)

    zero = jnp.zeros((SUBLANES, LANES), F32)
    init = tuple(tuple(zero for _ in range(nj)) for _ in range(nb))
    accs = lax.fori_loop(0, d // MOD_KC, body, init)
    for b in range(nb):
        for jj in range(nj):
            r = jnp.sum(accs[b][jj], axis=0, keepdims=True)
            o_ref[0, b:b + 1, jj * LANES:(jj + 1) * LANES] = r + b_ref[0, :, jj * LANES:(jj + 1) * LANES]


def _mod_call(c, ada_w, ada_b):
    depth, d, n6 = ada_w.shape
    nb = c.shape[0]
    cb = jnp.broadcast_to(c[:, :, None], (nb, d, LANES))
    return pl.pallas_call(
        _mod_kernel,
        grid=(depth, n6 // MOD_TN),
        in_specs=[
            pl.BlockSpec((nb, d, LANES), lambda l, j: (0, 0, 0)),
            pl.BlockSpec((1, d, MOD_TN), lambda l, j: (l, 0, j)),
            pl.BlockSpec((1, 1, MOD_TN), lambda l, j: (l, 0, j)),
        ],
        out_specs=pl.BlockSpec((1, nb, MOD_TN), lambda l, j: (l, 0, j)),
        out_shape=jax.ShapeDtypeStruct((depth, nb, n6), F32),
        scratch_shapes=[pltpu.VMEM((nb, d, LANES), F32)],
        compiler_params=_cparams(("arbitrary", "arbitrary")),
        name="adaln_mod",
    )(cb, ada_w, ada_b.reshape(depth, 1, n6))


DIN_TM = 512
DIN_TN = 1024


def _diff_in_kernel(h_ref, g_ref, sc_ref, sh_ref, w_ref, tab_ref, o_ref, u_ref, *, nqb):
    j = pl.program_id(1)
    nh = DIN_TN // PAIR

    @pl.when(j == 0)
    def _():
        u_ref[...] = _norm_mod(h_ref[...], g_ref[...], sc_ref[0], sh_ref[0]).astype(BF16)

    acc = jnp.dot(u_ref[...], w_ref[0], preferred_element_type=F32)

    @pl.when(j < 2 * nqb)
    def _():
        c = tab_ref[0, 0]
        sa = tab_ref[0, 1]
        sb = tab_ref[0, 2]
        for hh in range(nh):
            for half in range(2):
                lo = (2 * hh + half) * LANES
                y = _rope128(acc[:, lo:lo + LANES], c, sa, sb, DIFF_HEAD_DIM // 8)
                o_ref[0, hh, :, half * LANES:(half + 1) * LANES] = y.astype(BF16)

    @pl.when(j >= 2 * nqb)
    def _():
        for hh in range(nh):
            o_ref[0, hh] = acc[:, hh * PAIR:(hh + 1) * PAIR].astype(BF16)


def _diff_in_call(h, g, scale, shift, w_bf, layer, tab, seq):
    n, d = h.shape
    nout = w_bf.shape[2]
    third = nout // 3
    nqb = third // DIN_TN
    heads = third // PAIR
    hpb = DIN_TN // PAIR
    tiles_per_b = seq // DIN_TM
    return pl.pallas_call(
        functools.partial(_diff_in_kernel, nqb=nqb),
        grid=(n // DIN_TM, nout // DIN_TN),
        in_specs=[
            pl.BlockSpec((DIN_TM, d), lambda i, j: (i, 0)),
            pl.BlockSpec((1, d), lambda i, j: (0, 0)),
            pl.BlockSpec((1, 1, d), lambda i, j: (i // tiles_per_b, 0, 0)),
            pl.BlockSpec((1, 1, d), lambda i, j: (i // tiles_per_b, 0, 0)),
            pl.BlockSpec((1, d, DIN_TN), lambda i, j: (layer, 0, j)),
            pl.BlockSpec((1, 3, DIN_TM, LANES), lambda i, j: (jnp.minimum(j // nqb, 1), 0, i, 0)),
        ],
        out_specs=pl.BlockSpec((1, hpb, DIN_TM, PAIR), lambda i, j: (j // nqb, j % nqb, i, 0)),
        out_shape=jax.ShapeDtypeStruct((3, heads, n, PAIR), BF16),
        scratch_shapes=[pltpu.VMEM((DIN_TM, d), BF16)],
        compiler_params=_cparams(("parallel", "arbitrary")),
        name="diff_in",
    )(h, g, scale, shift, w_bf, tab)


def _chunk_mask():
    r = lax.broadcasted_iota(jnp.int32, (ATT_T, ATT_T), 0) // CHUNK
    c = lax.broadcasted_iota(jnp.int32, (ATT_T, ATT_T), 1) // CHUNK
    return c <= r


def _softmax_step(s, m, l, acc_ref, v):
    m_new = jnp.maximum(m, jnp.max(s, axis=-1, keepdims=True))
    alpha = jnp.exp(m - m_new)
    p = jnp.exp(s - m_new)
    l_new = alpha * l + jnp.sum(p, axis=-1, keepdims=True)
    acc_ref[...] = alpha * acc_ref[...] + jnp.dot(p.astype(BF16), v, preferred_element_type=F32)
    return m_new, l_new


def _qk(q, k):
    return lax.dot_general(q, k, (((1,), (1,)), ((), ())), preferred_element_type=F32)


DIFF_HEADS_PER_STEP = 2


def _diff_attn_kernel(q_ref, k_ref, v_ref, lam_ref, g_ref, o_ref, acc_ref, *, lambda_init):
    qi = pl.program_id(1)
    heads = q_ref.shape[1]
    hps = DIFF_HEADS_PER_STEP
    lam = lam_ref[...]
    lam_full = (jnp.exp(jnp.sum(lam[0:1] * lam[1:2], axis=-1, keepdims=True))
                - jnp.exp(jnp.sum(lam[2:3] * lam[3:4], axis=-1, keepdims=True)) + lambda_init)
    mask = _chunk_mask()
    m0 = jnp.full((ATT_T, 1), NEG_BIG, F32)
    l0 = jnp.zeros((ATT_T, 1), F32)

    def group_body(gi, carry):
        acc_ref[...] = jnp.zeros_like(acc_ref)

        def step(kt, st, masked):
            k0 = pl.multiple_of(kt * ATT_T, ATT_T)
            new = []
            for n in range(hps):
                h = gi * hps + n
                v = v_ref[0, h, pl.ds(k0, ATT_T), :]
                for c in range(2):
                    sl = slice(c * LANES, (c + 1) * LANES)
                    s = _qk(q_ref[0, h, :, sl], k_ref[0, h, pl.ds(k0, ATT_T), sl])
                    if masked:
                        s = jnp.where(mask, s, NEG_BIG)
                    e = 2 * n + c
                    m, l = _softmax_step(s, st[2 * e], st[2 * e + 1], acc_ref.at[e], v)
                    new += [m, l]
            return tuple(new)

        st = lax.fori_loop(0, qi, lambda kt, st: step(kt, st, False), (m0, l0) * (2 * hps))
        st = step(qi, st, True)
        for n in range(hps):
            l1 = st[2 * (2 * n) + 1]
            l2 = st[2 * (2 * n + 1) + 1]
            o = acc_ref[2 * n] / l1 - lam_full * (acc_ref[2 * n + 1] / l2)
            o = _rms(o, g_ref[...]) * (1.0 - lambda_init)
            o_ref[gi * hps + n] = o.astype(BF16)
        return carry

    lax.fori_loop(0, heads // hps, group_body, 0)


def _diff_attn_call(qkv, lam, g, lambda_init, batch, seq):
    _, heads, n, _ = qkv.shape
    nq = seq // ATT_T
    kv_spec = lambda which: pl.BlockSpec((1, heads, seq, PAIR), lambda b, qi: (which, 0, b, 0),
                                         pipeline_mode=pl.Buffered(1))
    return pl.pallas_call(
        functools.partial(_diff_attn_kernel, lambda_init=lambda_init),
        grid=(batch, nq),
        in_specs=[
            pl.BlockSpec((1, heads, ATT_T, PAIR), lambda b, qi: (0, 0, b * nq + qi, 0)),
            kv_spec(1),
            kv_spec(2),
            pl.BlockSpec((4, DIFF_HEAD_DIM), lambda b, qi: (0, 0)),
            pl.BlockSpec((1, PAIR), lambda b, qi: (0, 0)),
        ],
        out_specs=pl.BlockSpec((heads, ATT_T, PAIR), lambda b, qi: (0, b * nq + qi, 0)),
        out_shape=jax.ShapeDtypeStruct((heads, n, PAIR), BF16),
        scratch_shapes=[pltpu.VMEM((2 * DIFF_HEADS_PER_STEP, ATT_T, PAIR), F32)],
        compiler_params=_cparams(("parallel", "arbitrary")),
        name="diff_attn",
    )(qkv, qkv, qkv, lam, g)


MLA_TM = 256


def _mla_in_kernel(h_ref, g_ref, sc_ref, sh_ref, win_ref, gq_ref, gkv_ref, wuq_ref, wukv_ref, tab_ref,
                   proj_ref, kr_ref, *, scale):
    u = _norm_mod(h_ref[...], g_ref[...], sc_ref[0], sh_ref[0]).astype(BF16)
    lat = jnp.dot(u, win_ref[0], preferred_element_type=F32)
    qn = _rms(lat[:, 0:MLA_Q_RANK], gq_ref[...]).astype(BF16)
    kvn = _rms(lat[:, MLA_Q_RANK:MLA_Q_RANK + MLA_KV_RANK], gkv_ref[...]).astype(BF16)
    half = MLA_ROPE // 2
    kr = lat[:, MLA_Q_RANK + MLA_KV_RANK:MLA_Q_RANK + MLA_KV_RANK + LANES]
    kr_ref[...] = _rope128(kr, tab_ref[1, 0], tab_ref[1, 1], tab_ref[1, 2], half).astype(BF16)

    npair = proj_ref.shape[1]
    width = npair * PAIR
    q = jnp.dot(qn, wuq_ref[0], preferred_element_type=F32)
    c = tab_ref[0, 0]
    sa = tab_ref[0, 1]
    sb = tab_ref[0, 2]
    for gp in range(npair):
        proj_ref[0, gp] = (q[:, gp * PAIR:(gp + 1) * PAIR] * scale).astype(BF16)
        for a in range(2):
            lo = width + gp * PAIR + a * LANES
            y = _rope128(q[:, lo:lo + LANES], c, sa, sb, half)
            proj_ref[1, gp, :, a * LANES:(a + 1) * LANES] = y.astype(BF16)
    kv = jnp.dot(kvn, wukv_ref[0], preferred_element_type=F32)
    for gp in range(npair):
        proj_ref[2, gp] = kv[:, gp * PAIR:(gp + 1) * PAIR].astype(BF16)
        proj_ref[3, gp] = kv[:, width + gp * PAIR:width + (gp + 1) * PAIR].astype(BF16)


def _mla_in_call(h, g, scale_a, shift_a, w_in, gq, gkv, w_uq, w_ukv, layer, tab, seq, scale):
    n, d = h.shape
    lat_w = w_in.shape[2]
    up_w = w_uq.shape[2]
    npair = up_w // (2 * PAIR)
    tiles_per_b = seq // MLA_TM
    const = lambda shape: pl.BlockSpec(shape, lambda i: (layer,) + (0,) * (len(shape) - 1),
                                       pipeline_mode=pl.Buffered(1))
    return pl.pallas_call(
        functools.partial(_mla_in_kernel, scale=scale),
        grid=(n // MLA_TM,),
        in_specs=[
            pl.BlockSpec((MLA_TM, d), lambda i: (i, 0)),
            pl.BlockSpec((1, d), lambda i: (0, 0)),
            pl.BlockSpec((1, 1, d), lambda i: (i // tiles_per_b, 0, 0)),
            pl.BlockSpec((1, 1, d), lambda i: (i // tiles_per_b, 0, 0)),
            const((1, d, lat_w)),
            pl.BlockSpec((1, MLA_Q_RANK), lambda i: (0, 0)),
            pl.BlockSpec((1, MLA_KV_RANK), lambda i: (0, 0)),
            const((1, MLA_Q_RANK, up_w)),
            const((1, MLA_KV_RANK, up_w)),
            pl.BlockSpec((2, 3, MLA_TM, LANES), lambda i: (0, 0, i, 0)),
        ],
        out_specs=[
            pl.BlockSpec((4, npair, MLA_TM, PAIR), lambda i: (0, 0, i, 0)),
            pl.BlockSpec((MLA_TM, LANES), lambda i: (i, 0)),
        ],
        out_shape=[
            jax.ShapeDtypeStruct((4, npair, n, PAIR), BF16),
            jax.ShapeDtypeStruct((n, LANES), BF16),
        ],
        compiler_params=_cparams(("parallel",)),
        name="mla_in",
    )(h, g, scale_a, shift_a, w_in, gq, gkv, w_uq, w_ukv, tab)


MLA_PAIRS_PER_STEP = 2


def _mla_attn_kernel(q_ref, kv_ref, kr_ref, o_ref, qc_ref, acc_ref):
    qi = pl.program_id(1)
    npair = q_ref.shape[1]
    pps = MLA_PAIRS_PER_STEP
    mask = _chunk_mask()
    m0 = jnp.full((ATT_T, 1), NEG_BIG, F32)
    l0 = jnp.zeros((ATT_T, 1), F32)
    lane_sl = [slice(a * LANES, (a + 1) * LANES) for a in range(2)]

    def group_body(gi, carry):
        acc_ref[...] = jnp.zeros_like(acc_ref)
        for p in range(pps):
            for a in range(2):
                gp = gi * pps + p
                qc_ref[2 * p + a] = jnp.concatenate([q_ref[0, gp, :, lane_sl[a]], q_ref[1, gp, :, lane_sl[a]]],
                                                    axis=-1)

        def step(kt, st, masked):
            k0 = pl.multiple_of(kt * ATT_T, ATT_T)
            kr = kr_ref[pl.ds(k0, ATT_T), :]
            new = []
            for p in range(pps):
                for a in range(2):
                    gp = gi * pps + p
                    e = 2 * p + a
                    kc = jnp.concatenate([kv_ref[0, gp, pl.ds(k0, ATT_T), lane_sl[a]], kr], axis=-1)
                    v = kv_ref[1, gp, pl.ds(k0, ATT_T), lane_sl[a]]
                    s = _qk(qc_ref[e], kc)
                    if masked:
                        s = jnp.where(mask, s, NEG_BIG)
                    m, l = _softmax_step(s, st[2 * e], st[2 * e + 1], acc_ref.at[e], v)
                    new += [m, l]
            return tuple(new)

        st = lax.fori_loop(0, qi, lambda kt, st: step(kt, st, False), (m0, l0) * (2 * pps))
        st = step(qi, st, True)
        for p in range(pps):
            for a in range(2):
                e = 2 * p + a
                o_ref[gi * pps + p, :, lane_sl[a]] = (acc_ref[e] / st[2 * e + 1]).astype(BF16)
        return carry

    lax.fori_loop(0, npair // pps, group_body, 0)


def _mla_attn_call(proj, kr, batch, seq):
    _, npair, n, _ = proj.shape
    nq = seq // ATT_T
    return pl.pallas_call(
        _mla_attn_kernel,
        grid=(batch, nq),
        in_specs=[
            pl.BlockSpec((2, npair, ATT_T, PAIR), lambda b, qi: (0, 0, b * nq + qi, 0)),
            pl.BlockSpec((2, npair, seq, PAIR), lambda b, qi: (1, 0, b, 0), pipeline_mode=pl.Buffered(1)),
            pl.BlockSpec((seq, LANES), lambda b, qi: (b, 0), pipeline_mode=pl.Buffered(1)),
        ],
        out_specs=pl.BlockSpec((npair, ATT_T, PAIR), lambda b, qi: (0, b * nq + qi, 0)),
        out_shape=jax.ShapeDtypeStruct((npair, n, PAIR), BF16),
        scratch_shapes=[pltpu.VMEM((2 * MLA_PAIRS_PER_STEP, ATT_T, PAIR), BF16),
                        pltpu.VMEM((2 * MLA_PAIRS_PER_STEP, ATT_T, LANES), F32)],
        compiler_params=_cparams(("parallel", "arbitrary")),
        name="mla_attn",
    )(proj, proj, kr)


AOUT_TM = 512


def _attn_out_kernel(o_ref, w_ref, h_ref, gate_ref, g_ref, sc_ref, sh_ref, wr_ref, br_ref,
                     hn_ref, u_ref, idx_ref, gw_ref):
    ns = o_ref.shape[0]
    y = jnp.dot(o_ref[0], w_ref[0, 0:PAIR, :], preferred_element_type=F32)
    for s in range(1, ns):
        y = y + jnp.dot(o_ref[s], w_ref[0, s * PAIR:(s + 1) * PAIR, :], preferred_element_type=F32)
    hn = h_ref[...] + gate_ref[0] * y
    hn_ref[...] = hn
    u = _norm_mod(hn, g_ref[...], sc_ref[0], sh_ref[0])
    u_ref[...] = u
    u_hi = u.astype(BF16)
    u_lo = (u - u_hi.astype(F32)).astype(BF16)
    logits = jnp.dot(jnp.concatenate([u_hi, u_lo, u_hi], axis=-1), wr_ref[0], preferred_element_type=F32) + br_ref[0]
    tm, ne = logits.shape
    eio = lax.broadcasted_iota(jnp.int32, (tm, ne), 1)
    lane = lax.broadcasted_iota(jnp.int32, (tm, LANES), 1)
    idx_out = jnp.zeros((tm, LANES), jnp.int32)
    val_out = jnp.zeros((tm, LANES), F32)
    l = logits
    v0 = None
    den = None
    for k in range(TOP_K):
        m = jnp.max(l, axis=-1, keepdims=True)
        i = jnp.min(jnp.where(l == m, eio, ne), axis=-1, keepdims=True)
        l = jnp.where(eio == i, -jnp.inf, l)
        if k == 0:
            v0 = m
        e = jnp.exp(m - v0)
        den = e if den is None else den + e
        idx_out = jnp.where(lane == k, i, idx_out)
        val_out = jnp.where(lane == k, e, val_out)
    idx_ref[...] = idx_out
    gw_ref[...] = val_out / den


def _attn_out_call(o, w_out, layer, h, gate, g, scale, shift, w_router, b_router, seq):
    ns, n, _ = o.shape
    d = h.shape[1]
    ne = w_router.shape[2]
    tiles_per_b = seq // AOUT_TM
    perb = lambda: pl.BlockSpec((1, 1, d), lambda i: (i // tiles_per_b, 0, 0))
    return pl.pallas_call(
        _attn_out_kernel,
        grid=(n // AOUT_TM,),
        in_specs=[
            pl.BlockSpec((ns, AOUT_TM, PAIR), lambda i: (0, i, 0)),
            pl.BlockSpec((1, d, d), lambda i: (layer, 0, 0), pipeline_mode=pl.Buffered(1)),
            pl.BlockSpec((AOUT_TM, d), lambda i: (i, 0)),
            perb(),
            pl.BlockSpec((1, d), lambda i: (0, 0)),
            perb(),
            perb(),
            pl.BlockSpec((1, 3 * d, ne), lambda i: (0, 0, 0)),
            pl.BlockSpec((1, 1, ne), lambda i: (0, 0, 0)),
        ],
        out_specs=[
            pl.BlockSpec((AOUT_TM, d), lambda i: (i, 0)),
            pl.BlockSpec((AOUT_TM, d), lambda i: (i, 0)),
            pl.BlockSpec((AOUT_TM, LANES), lambda i: (i, 0)),
            pl.BlockSpec((AOUT_TM, LANES), lambda i: (i, 0)),
        ],
        out_shape=[
            jax.ShapeDtypeStruct((n, d), F32),
            jax.ShapeDtypeStruct((n, d), F32),
            jax.ShapeDtypeStruct((n, LANES), jnp.int32),
            jax.ShapeDtypeStruct((n, LANES), F32),
        ],
        compiler_params=_cparams(("parallel",)),
        name="attn_out",
    )(o, w_out, h, gate, g, scale, shift, w_router, b_router)


def _route_kernel(idx_ref, dest_ref, pt_ref, carry_ref, ps_ref):
    ph = pl.program_id(0)
    t = pl.program_id(1)
    nt = pl.num_programs(1)
    tb = idx_ref.shape[0]
    idx = idx_ref[...]
    eio = lax.broadcasted_iota(jnp.int32, (tb, LANES), 1)
    sel = [eio == idx[:, k:k + 1] for k in range(TOP_K)]
    oh = jnp.where(sel[0], 1.0, 0.0)
    for k in range(1, TOP_K):
        oh = oh + jnp.where(sel[k], 1.0, 0.0)

    @pl.when(t == 0)
    def _():
        carry_ref[...] = jnp.zeros_like(carry_ref)

    @pl.when((ph == 0) & (t == nt - 1))
    def _():
        cnt = carry_ref[...] + jnp.sum(oh, axis=0, keepdims=True)
        tiles = jnp.floor((cnt + (EXPERT_TM - 1)) * (1.0 / EXPERT_TM))
        r = lax.broadcasted_iota(jnp.int32, (LANES, LANES), 0)
        c = lax.broadcasted_iota(jnp.int32, (LANES, LANES), 1)
        upper = jnp.where(r < c, 1.0, 0.0).astype(BF16)
        t8 = jnp.broadcast_to(tiles, (SUBLANES, LANES))
        start = jnp.dot(t8.astype(BF16), upper, preferred_element_type=F32)
        ps_ref[...] = start[0:1] * EXPERT_TM
        row = lax.broadcasted_iota(jnp.int32, (SUBLANES, LANES), 0)
        pt_ref[...] = jnp.where(row == 0, t8, jnp.broadcast_to(cnt, (SUBLANES, LANES)))

    @pl.when(ph == 1)
    def _():
        r = lax.broadcasted_iota(jnp.int32, (tb, tb), 0)
        c = lax.broadcasted_iota(jnp.int32, (tb, tb), 1)
        lower = jnp.where(c < r, 1.0, 0.0).astype(BF16)
        before = jnp.dot(lower, oh.astype(BF16), preferred_element_type=F32)
        base = before + carry_ref[...] + ps_ref[...]
        out = jnp.zeros((tb, LANES), jnp.int32)
        for k in range(TOP_K):
            dk = jnp.sum(jnp.where(sel[k], base, 0.0), axis=-1, keepdims=True)
            out = jnp.where(eio == k, dk.astype(jnp.int32), out)
        dest_ref[...] = out

    carry_ref[...] += jnp.sum(oh, axis=0, keepdims=True)


def _route_call(idx):
    n = idx.shape[0]
    return pl.pallas_call(
        _route_kernel,
        grid=(2, n // ROUTE_TB),
        in_specs=[pl.BlockSpec((ROUTE_TB, LANES), lambda p, t: (t, 0))],
        out_specs=[
            pl.BlockSpec((ROUTE_TB, LANES), lambda p, t: (p * t, 0)),
            pl.BlockSpec((SUBLANES, LANES), lambda p, t: (0, 0)),
        ],
        out_shape=[
            jax.ShapeDtypeStruct((n, LANES), jnp.int32),
            jax.ShapeDtypeStruct((SUBLANES, LANES), F32),
        ],
        scratch_shapes=[pltpu.VMEM((1, LANES), F32), pltpu.VMEM((1, LANES), F32)],
        compiler_params=_cparams(("arbitrary", "arbitrary")),
        name="route",
    )(idx)


DISPATCH_ZROWS = 256
assert EXPERT_TM % DISPATCH_ZROWS == 0


def _dispatch_kernel(dest_ref, pstart_ref, cnt_ref, pend_ref, u_ref, xb_ref, zbuf_ref, sem, zsem):
    i = pl.program_id(0)
    tt = u_ref.shape[0]
    rows = xb_ref.shape[0]
    ne = pstart_ref.shape[0]

    def zero_copy(pos, nrows):
        return pltpu.make_async_copy(zbuf_ref.at[pl.ds(0, nrows)], xb_ref.at[pl.ds(pos, nrows)], zsem)

    def pad_fill(act):
        def expert_body(e, carry):
            pos = pstart_ref[e] + cnt_ref[e]
            npad = pend_ref[e] - pos
            head = jnp.minimum(npad, (-pos) & (SUBLANES - 1))

            def head_body(r, c):
                act(zero_copy(pos + r, 1))
                return c

            lax.fori_loop(0, head, head_body, 0)
            base = pos + head
            rem = npad - head
            bit = DISPATCH_ZROWS
            while bit >= SUBLANES:
                off = rem - (rem & (2 * bit - 1))

                @pl.when((rem & bit) != 0)
                def _(off=off, bit=bit):
                    act(zero_copy(pl.multiple_of(base + off, SUBLANES), bit))

                bit //= 2
            return carry

        lax.fori_loop(0, ne, expert_body, 0)
        tail0 = pend_ref[ne - 1]

        def tail_body(b, c):
            act(zero_copy(pl.multiple_of(tail0 + b * DISPATCH_ZROWS, DISPATCH_ZROWS), DISPATCH_ZROWS))
            return c

        lax.fori_loop(0, (rows - tail0) // DISPATCH_ZROWS, tail_body, 0)

    @pl.when(i == 0)
    def _():
        zbuf_ref[...] = jnp.zeros_like(zbuf_ref)
        pad_fill(lambda c: c.start())

    def row_copy(t, d):
        return pltpu.make_async_copy(u_ref.at[pl.ds(t, 1)], xb_ref.at[pl.ds(d, 1)], sem)

    def issue(t, carry):
        for k in range(TOP_K):
            row_copy(t, dest_ref[(i * tt + t) * TOP_K + k]).start()
        return carry

    def drain(t, carry):
        for k in range(TOP_K):
            row_copy(t, dest_ref[(i * tt + t) * TOP_K + k]).wait()
        return carry

    lax.fori_loop(0, tt, issue, 0)
    lax.fori_loop(0, tt, drain, 0)

    @pl.when(i == 0)
    def _():
        pad_fill(lambda c: c.wait())


def _dispatch_call(dest_flat, pstart, cnt, pend, u, rows):
    n, d = u.shape
    return pl.pallas_call(
        _dispatch_kernel,
        grid_spec=pltpu.PrefetchScalarGridSpec(
            num_scalar_prefetch=4,
            grid=(n // DISPATCH_TT,),
            in_specs=[pl.BlockSpec((DISPATCH_TT, d), lambda i, *_: (i, 0))],
            out_specs=pl.BlockSpec(memory_space=pl.ANY),
            scratch_shapes=[pltpu.VMEM((DISPATCH_ZROWS, d), u.dtype),
                            pltpu.SemaphoreType.DMA(()), pltpu.SemaphoreType.DMA(())],
        ),
        out_shape=jax.ShapeDtypeStruct((rows, d), u.dtype),
        compiler_params=_cparams(("arbitrary",), has_side_effects=True),
        name="dispatch",
    )(dest_flat, pstart, cnt, pend, u)


def _expert_kernel(te_ref, first_ref, nu_ref, x_ref, wg_ref, wu_ref, bg_ref, bu_ref, wd_ref, bd_ref, o_ref,
                   xbf_ref, wgb_ref, wub_ref, wdb_ref):
    del te_ref
    i = pl.program_id(0)
    j = pl.program_id(1)

    @pl.when(i < nu_ref[0])
    def _():
        @pl.when(j == 0)
        def _():
            xbf_ref[...] = x_ref[...].astype(BF16)

        @pl.when(first_ref[i] == 1)
        def _():
            wgb_ref[j] = wg_ref[0, 0].astype(BF16)
            wub_ref[j] = wu_ref[0, 0].astype(BF16)
            wdb_ref[j] = wd_ref[0, 0].astype(BF16)

        x = xbf_ref[...]
        gate = jnp.dot(x, wgb_ref[j], preferred_element_type=F32) + bg_ref[0, 0]
        up = jnp.dot(x, wub_ref[j], preferred_element_type=F32) + bu_ref[0, 0]
        gate = jnp.minimum(gate, SWIGLU_LIMIT)
        up = jnp.clip(up, -SWIGLU_LIMIT, SWIGLU_LIMIT)
        glu = gate * _sigmoid(gate * SWIGLU_ALPHA)
        a = ((up + 1.0) * glu).astype(BF16)
        y = jnp.dot(a, wdb_ref[j], preferred_element_type=F32)

        @pl.when(j == 0)
        def _():
            o_ref[...] = y + bd_ref[0, 0]

        @pl.when(j > 0)
        def _():
            o_ref[...] += y

    @pl.when((i >= nu_ref[0]) & (j == 0))
    def _():
        o_ref[...] = jnp.zeros_like(o_ref)


def _expert_call(tile_e, first, n_used, xb, w_gu, b_gu, w_down, b_down, layer):
    rows, d = xb.shape
    ne = w_gu.shape[1]
    f = w_down.shape[2]
    nf = f // EXPERT_TF
    n_tiles = rows // EXPERT_TM

    def tile(i, nu):
        return jnp.minimum(i, nu[0] - 1)

    def fcol(i, j, nu):
        return jnp.where(i < nu[0], j, nf - 1)

    def wcol(i, j, fi, nu):
        return jnp.where((i < nu[0]) & (fi[tile(i, nu)] == 1), j, nf - 1)

    return pl.pallas_call(
        _expert_kernel,
        grid_spec=pltpu.PrefetchScalarGridSpec(
            num_scalar_prefetch=3,
            grid=(n_tiles, nf),
            in_specs=[
                pl.BlockSpec((EXPERT_TM, d), lambda i, j, te, fi, nu: (tile(i, nu), 0)),
                pl.BlockSpec((1, 1, d, EXPERT_TF),
                             lambda i, j, te, fi, nu: (layer, te[tile(i, nu)], 0, wcol(i, j, fi, nu))),
                pl.BlockSpec((1, 1, d, EXPERT_TF),
                             lambda i, j, te, fi, nu: (layer, te[tile(i, nu)], 0, nf + wcol(i, j, fi, nu))),
                pl.BlockSpec((1, 1, 1, EXPERT_TF),
                             lambda i, j, te, fi, nu: (layer, te[tile(i, nu)], 0, fcol(i, j, nu))),
                pl.BlockSpec((1, 1, 1, EXPERT_TF),
                             lambda i, j, te, fi, nu: (layer, te[tile(i, nu)], 0, nf + fcol(i, j, nu))),
                pl.BlockSpec((1, 1, EXPERT_TF, d),
                             lambda i, j, te, fi, nu: (layer, te[tile(i, nu)], wcol(i, j, fi, nu), 0)),
                pl.BlockSpec((1, 1, 1, d), lambda i, j, te, fi, nu: (layer, te[tile(i, nu)], 0, 0)),
            ],
            out_specs=pl.BlockSpec((EXPERT_TM, d), lambda i, j, te, fi, nu: (i, 0)),
            scratch_shapes=[pltpu.VMEM((EXPERT_TM, d), BF16),
                            pltpu.VMEM((nf, d, EXPERT_TF), BF16),
                            pltpu.VMEM((nf, d, EXPERT_TF), BF16),
                            pltpu.VMEM((nf, EXPERT_TF, d), BF16)],
        ),
        out_shape=jax.ShapeDtypeStruct((rows, d), F32),
        compiler_params=_cparams(("arbitrary", "arbitrary")),
        name="experts",
    )(tile_e, first, n_used, xb, w_gu, w_gu, b_gu.reshape(b_gu.shape[0], ne, 1, 2 * f),
      b_gu.reshape(b_gu.shape[0], ne, 1, 2 * f), w_down, b_down.reshape(b_down.shape[0], ne, 1, d))


def _combine_kernel(dest_ref, yb_ref, gw_ref, h_ref, gate_ref, g_ref, o_ref, buf_ref, sem, *, final):
    i = pl.program_id(0)
    nsteps = pl.num_programs(0)
    tt = h_ref.shape[0]
    slot = i % 2

    def row_copy(tile, sl, t, k):
        d = dest_ref[(tile * tt + t) * TOP_K + k]
        return pltpu.make_async_copy(yb_ref.at[pl.ds(d, 1)], buf_ref.at[sl, k, pl.ds(t, 1)], sem.at[sl])

    def issue(tile, sl):
        def body(t, carry):
            for k in range(TOP_K):
                row_copy(tile, sl, t, k).start()
            return carry
        lax.fori_loop(0, tt, body, 0)

    def drain(tile, sl):
        def body(t, carry):
            for k in range(TOP_K):
                row_copy(tile, sl, t, k).wait()
            return carry
        lax.fori_loop(0, tt, body, 0)

    @pl.when(i == 0)
    def _():
        issue(0, 0)

    @pl.when(i + 1 < nsteps)
    def _():
        issue(i + 1, 1 - slot)

    drain(i, slot)
    gw = gw_ref[...]
    y = gw[:, 0:1] * buf_ref[slot, 0]
    for k in range(1, TOP_K):
        y = y + gw[:, k:k + 1] * buf_ref[slot, k]
    hn = h_ref[...] + gate_ref[0] * y
    if final:
        hn = _rms(hn, g_ref[...])
    o_ref[...] = hn


def _combine_call(dest_flat, yb, gw, h, gate, g_final, seq, final):
    n, d = h.shape
    tiles_per_b = seq // COMBINE_TT
    return pl.pallas_call(
        functools.partial(_combine_kernel, final=final),
        grid_spec=pltpu.PrefetchScalarGridSpec(
            num_scalar_prefetch=1,
            grid=(n // COMBINE_TT,),
            in_specs=[
                pl.BlockSpec(memory_space=pl.ANY),
                pl.BlockSpec((COMBINE_TT, LANES), lambda i, dest: (i, 0)),
                pl.BlockSpec((COMBINE_TT, d), lambda i, dest: (i, 0)),
                pl.BlockSpec((1, 1, d), lambda i, dest: (i // tiles_per_b, 0, 0)),
                pl.BlockSpec((1, d), lambda i, dest: (0, 0)),
            ],
            out_specs=pl.BlockSpec((COMBINE_TT, d), lambda i, dest: (i, 0)),
            scratch_shapes=[pltpu.VMEM((2, TOP_K, COMBINE_TT, d), F32), pltpu.SemaphoreType.DMA((2,))],
        ),
        out_shape=jax.ShapeDtypeStruct((n, d), F32),
        compiler_params=_cparams(("arbitrary",)),
        name="combine",
    )(dest_flat, yb, gw, h, gate, g_final)


def _rope_tables(positions, rot_dim, scale):
    half = rot_dim // 2
    inv_freq = ROPE_THETA ** (-jnp.arange(0, rot_dim, 2, dtype=F32) / rot_dim)
    ang = positions.reshape(-1).astype(F32)[:, None] * inv_freq
    cos, sin = jnp.cos(ang), jnp.sin(ang)
    n = ang.shape[0]
    c = jnp.concatenate([cos, cos, jnp.ones((n, LANES - rot_dim), F32)], axis=1)
    sa = jnp.concatenate([-sin, jnp.zeros((n, LANES - half), F32)], axis=1)
    sb = jnp.concatenate([jnp.zeros((n, half), F32), sin, jnp.zeros((n, LANES - rot_dim), F32)], axis=1)
    t = jnp.stack([c, sa, sb])
    return jnp.stack([t * scale, t])


def _moe(layer, h, u, idx, gw, gate_f, g_final, final, seq, w_gu, b_gu, w_down, b_down):
    n, d = h.shape
    rows = -(-(n * TOP_K + N_EXPERTS * (EXPERT_TM - 1)) // EXPERT_TM) * EXPERT_TM
    n_tiles = rows // EXPERT_TM
    dest, stats = _route_call(idx)
    dest_flat = dest[:, :TOP_K].reshape(-1)
    cnt = stats[1, :N_EXPERTS].astype(jnp.int32)
    tile_end = jnp.cumsum(stats[0, :N_EXPERTS].astype(jnp.int32))
    pend = tile_end * EXPERT_TM
    pstart = jnp.concatenate([jnp.zeros((1,), jnp.int32), pend[:-1]])
    tile_e = jnp.minimum(jnp.searchsorted(tile_end, jnp.arange(n_tiles, dtype=jnp.int32), side='right'),
                         N_EXPERTS - 1).astype(jnp.int32)
    first = jnp.concatenate([jnp.ones((1,), jnp.int32), (tile_e[1:] != tile_e[:-1]).astype(jnp.int32)])
    n_used = tile_end[-1:].astype(jnp.int32)
    xb = _dispatch_call(dest_flat, pstart, cnt, pend, u, rows)
    yb = _expert_call(tile_e, first, n_used, xb, w_gu, b_gu, w_down, b_down, layer)
    return _combine_call(dest_flat, yb, gw, h, gate_f, g_final, seq, final)


def kernel(x, c, positions, ada_w, ada_b, mix_norm_g, ffn_norm_g, final_norm_g, diff_w_in, diff_lambda, diff_subln_g, diff_w_out, mla_w_in, mla_q_norm_g, mla_kv_norm_g, mla_w_uq, mla_w_ukv, mla_w_out, moe_w_router, moe_b_router, moe_w_gate_up, moe_b_gate_up, moe_w_down, moe_b_down):
    batch, seq, d = x.shape
    depth = ada_w.shape[0]
    n = batch * seq
    assert seq % ATT_T == 0 and ATT_T % CHUNK == 0 and d % PAIR == 0
    assert n % DISPATCH_TT == 0 and n % ROUTE_TB == 0 and seq % DIN_TM == 0

    diff_scale = DIFF_HEAD_DIM ** -0.5
    mla_scale = (MLA_NOPE + MLA_ROPE) ** -0.5
    tab_d = _rope_tables(positions, DIFF_HEAD_DIM // 4, diff_scale)
    tab_m = _rope_tables(positions, MLA_ROPE, mla_scale)

    mod = _mod_call(c, ada_w, ada_b)
    mod = mod.reshape(depth, batch, N_MOD, 1, d)

    diff_w_in_bf = diff_w_in.astype(BF16)
    diff_w_out_bf = diff_w_out.astype(BF16)
    mla_w_out_bf = mla_w_out.astype(BF16)
    nm = mla_w_in.shape[0]
    heads_m = mla_w_out.shape[1] // MLA_V
    lat_pad = LANES - MLA_ROPE
    mla_w_in_bf = jnp.pad(mla_w_in, ((0, 0), (0, 0), (0, lat_pad))).astype(BF16)
    wq = mla_w_uq.reshape(nm, MLA_Q_RANK, heads_m, MLA_NOPE + MLA_ROPE)
    wq_rope = jnp.pad(wq[..., MLA_NOPE:], ((0, 0), (0, 0), (0, 0), (0, LANES - MLA_ROPE)))
    mla_w_uq_bf = jnp.concatenate([wq[..., :MLA_NOPE].reshape(nm, MLA_Q_RANK, -1),
                                   wq_rope.reshape(nm, MLA_Q_RANK, -1)], axis=-1).astype(BF16)
    wkv = mla_w_ukv.reshape(nm, MLA_KV_RANK, heads_m, MLA_NOPE + MLA_V)
    mla_w_ukv_bf = jnp.concatenate([wkv[..., :MLA_NOPE].reshape(nm, MLA_KV_RANK, -1),
                                    wkv[..., MLA_NOPE:].reshape(nm, MLA_KV_RANK, -1)], axis=-1).astype(BF16)

    h = x.reshape(n, d)
    for i in range(depth):
        shift_a, scale_a, gate_a, shift_f, scale_f, gate_f = [mod[i, :, m] for m in range(N_MOD)]
        g_mix = mix_norm_g[i].reshape(1, d)
        g_ffn = ffn_norm_g[i].reshape(1, d)
        j = i // N_MIXERS
        if i % N_MIXERS == 0:
            lambda_init = 0.8 - 0.6 * math.exp(-0.3 * i)
            qkv = _diff_in_call(h, g_mix, scale_a, shift_a, diff_w_in_bf, j, tab_d, seq)
            o = _diff_attn_call(qkv, diff_lambda[j], diff_subln_g[j].reshape(1, -1), lambda_init, batch, seq)
            w_out = diff_w_out_bf
        else:
            proj, kr = _mla_in_call(h, g_mix, scale_a, shift_a, mla_w_in_bf, mla_q_norm_g[j].reshape(1, -1),
                                    mla_kv_norm_g[j].reshape(1, -1), mla_w_uq_bf, mla_w_ukv_bf, j, tab_m, seq,
                                    mla_scale)
            o = _mla_attn_call(proj, kr, batch, seq)
            w_out = mla_w_out_bf
        wr_hi = moe_w_router[i:i + 1].astype(BF16)
        wr_lo = (moe_w_router[i:i + 1] - wr_hi.astype(F32)).astype(BF16)
        wr3 = jnp.concatenate([wr_hi, wr_hi, wr_lo], axis=1)
        h, u, idx, gw = _attn_out_call(o, w_out, j, h, gate_a, g_ffn, scale_f, shift_f,
                                       wr3, moe_b_router[i:i + 1].reshape(1, 1, -1), seq)
        h = _moe(i, h, u, idx, gw, gate_f, final_norm_g.reshape(1, d), i == depth - 1, seq,
                 moe_w_gate_up, moe_b_gate_up, moe_w_down, moe_b_down)
    return h.reshape(batch, seq, d)
```

```python
import functools
import math

import jax
import jax.numpy as jnp
from jax import lax
from jax.experimental import pallas as pl
from jax.experimental.pallas import tpu as pltpu

F32 = jnp.float32
BF16 = jnp.bfloat16

CHUNK = 64
N_MIXERS = 2
ROPE_THETA = 500000.0
RMS_EPS = 1e-6
N_MOD = 6
DIFF_HEAD_DIM = 128
MLA_V = 128
MLA_Q_RANK = 512
MLA_KV_RANK = 512
MLA_NOPE = 128
MLA_ROPE = 64
N_EXPERTS = 32
TOP_K = 4
SWIGLU_LIMIT = 7.0
SWIGLU_ALPHA = 1.702

LANES = 128
SUBLANES = 8
V7X_VMEM_BYTES = 64 * 1024 * 1024
VMEM_LIMIT = 56 * 1024 * 1024

NEG_BIG = -1e30

ATT_T = 256
PAIR = 256
EXPERT_TM = 512
EXPERT_TF = 256
ROUTE_TB = 512
DISPATCH_TT = 512
COMBINE_TT = 256


def _cparams(sem, **kw):
    return pltpu.CompilerParams(dimension_semantics=sem, vmem_limit_bytes=VMEM_LIMIT, **kw)


def _sigmoid(x):
    return 1.0 / (1.0 + jnp.exp(-x))


def _rms(x, g):
    ms = jnp.mean(x * x, axis=-1, keepdims=True)
    return x * lax.rsqrt(ms + RMS_EPS) * g


def _norm_mod(x, g, scale, shift):
    return _rms(x, g) * (1.0 + scale) + shift


def _pack_halves(x):
    w = x.shape[-1] // 2
    lo = lax.bitcast_convert_type(x[:, :w].astype(BF16).astype(F32), jnp.uint32)
    hi = lax.bitcast_convert_type(x[:, w:].astype(BF16).astype(F32), jnp.uint32)
    return (lo >> 16) | (hi & jnp.uint32(0xFFFF0000))


def _unpack_halves(words):
    lo = lax.bitcast_convert_type(words << 16, F32)
    hi = lax.bitcast_convert_type(words & jnp.uint32(0xFFFF0000), F32)
    return lo, hi


def _rope128(x, c, sa, sb, half):
    return x * c + pltpu.roll(x, LANES - half, 1) * sa + pltpu.roll(x, half, 1) * sb


MOD_TN = 512
MOD_KC = 64


def _mod_kernel(cb_ref, w_ref, b_ref, o_ref, cs_ref):
    nb = cb_ref.shape[0]
    d = cb_ref.shape[1]
    nj = MOD_TN // LANES

    @pl.when((pl.program_id(0) == 0) & (pl.program_id(1) == 0))
    def _():
        c = cb_ref[...]
        cs_ref[...] = c * _sigmoid(c)

    def body(kc, accs):
        k0 = pl.multiple_of(kc * MOD_KC, MOD_KC)
        w = w_ref[0, pl.ds(k0, MOD_KC), :]
        out = []
        for b in range(nb):
            cb = cs_ref[b, pl.ds(k0, MOD_KC), :].reshape(MOD_KC // SUBLANES, SUBLANES, LANES)
            row = []
            for jj in range(nj):
                w3 = w[:, jj * LANES:(jj + 1) * LANES].reshape(MOD_KC // SUBLANES, SUBLANES, LANES)
                row.append(accs[b][jj] + jnp.sum(w3 * cb, axis=0))
            out.append(tuple(row))
        return tuple(out)

    zero = jnp.zeros((SUBLANES, LANES), F32)
    init = tuple(tuple(zero for _ in range(nj)) for _ in range(nb))
    accs = lax.fori_loop(0, d // MOD_KC, body, init)
    for b in range(nb):
        for jj in range(nj):
            r = jnp.sum(accs[b][jj], axis=0, keepdims=True)
            o_ref[0, b:b + 1, jj * LANES:(jj + 1) * LANES] = r + b_ref[0, :, jj * LANES:(jj + 1) * LANES]


def _mod_call(c, ada_w, ada_b):
    depth, d, n6 = ada_w.shape
    nb = c.shape[0]
    cb = jnp.broadcast_to(c[:, :, None], (nb, d, LANES))
    return pl.pallas_call(
        _mod_kernel,
        grid=(depth, n6 // MOD_TN),
        in_specs=[
            pl.BlockSpec((nb, d, LANES), lambda l, j: (0, 0, 0)),
            pl.BlockSpec((1, d, MOD_TN), lambda l, j: (l, 0, j)),
            pl.BlockSpec((1, 1, MOD_TN), lambda l, j: (l, 0, j)),
        ],
        out_specs=pl.BlockSpec((1, nb, MOD_TN), lambda l, j: (l, 0, j)),
        out_shape=jax.ShapeDtypeStruct((depth, nb, n6), F32),
        scratch_shapes=[pltpu.VMEM((nb, d, LANES), F32)],
        compiler_params=_cparams(("arbitrary", "arbitrary")),
        name="adaln_mod",
    )(cb, ada_w, ada_b.reshape(depth, 1, n6))


DIN_TM = 512
DIN_TN = 1024


def _diff_in_kernel(h_ref, g_ref, sc_ref, sh_ref, w_ref, tab_ref, o_ref, u_ref, *, nqb):
    j = pl.program_id(1)
    nh = DIN_TN // PAIR

    @pl.when(j == 0)
    def _():
        u_ref[...] = _norm_mod(h_ref[...], g_ref[...], sc_ref[0], sh_ref[0]).astype(BF16)

    def slab(hh):
        return jnp.dot(u_ref[...], w_ref[0, :, hh * PAIR:(hh + 1) * PAIR], preferred_element_type=F32)

    @pl.when(j < 2 * nqb)
    def _():
        c = tab_ref[0, 0]
        sa = tab_ref[0, 1]
        sb = tab_ref[0, 2]
        for hh in range(nh):
            acc = slab(hh)
            for half in range(2):
                y = _rope128(acc[:, half * LANES:(half + 1) * LANES], c, sa, sb, DIFF_HEAD_DIM // 8)
                o_ref[0, hh, :, half * LANES:(half + 1) * LANES] = y.astype(BF16)

    @pl.when(j >= 2 * nqb)
    def _():
        for hh in range(nh):
            o_ref[0, hh] = slab(hh).astype(BF16)


def _diff_in_call(h, g, scale, shift, w_bf, layer, tab, seq):
    n, d = h.shape
    nout = w_bf.shape[2]
    third = nout // 3
    nqb = third // DIN_TN
    heads = third // PAIR
    hpb = DIN_TN // PAIR
    tiles_per_b = seq // DIN_TM
    return pl.pallas_call(
        functools.partial(_diff_in_kernel, nqb=nqb),
        grid=(n // DIN_TM, nout // DIN_TN),
        in_specs=[
            pl.BlockSpec((DIN_TM, d), lambda i, j: (i, 0)),
            pl.BlockSpec((1, d), lambda i, j: (0, 0)),
            pl.BlockSpec((1, 1, d), lambda i, j: (i // tiles_per_b, 0, 0)),
            pl.BlockSpec((1, 1, d), lambda i, j: (i // tiles_per_b, 0, 0)),
            pl.BlockSpec((1, d, DIN_TN), lambda i, j: (layer, 0, j)),
            pl.BlockSpec((1, 3, DIN_TM, LANES), lambda i, j: (jnp.minimum(j // nqb, 1), 0, i, 0)),
        ],
        out_specs=pl.BlockSpec((1, hpb, DIN_TM, PAIR), lambda i, j: (j // nqb, j % nqb, i, 0)),
        out_shape=jax.ShapeDtypeStruct((3, heads, n, PAIR), BF16),
        scratch_shapes=[pltpu.VMEM((DIN_TM, d), BF16)],
        compiler_params=_cparams(("parallel", "arbitrary")),
        name="diff_in",
    )(h, g, scale, shift, w_bf, tab)


def _chunk_mask():
    r = lax.broadcasted_iota(jnp.int32, (ATT_T, ATT_T), 0) // CHUNK
    c = lax.broadcasted_iota(jnp.int32, (ATT_T, ATT_T), 1) // CHUNK
    return c <= r


def _softmax_step(s, m, l, acc_ref, v):
    m_new = jnp.maximum(m, jnp.max(s, axis=-1, keepdims=True))
    alpha = jnp.exp(m - m_new)
    p = jnp.exp(s - m_new)
    l_new = alpha * l + jnp.sum(p, axis=-1, keepdims=True)
    acc_ref[...] = alpha * acc_ref[...] + jnp.dot(p.astype(BF16), v, preferred_element_type=F32)
    return m_new, l_new


def _qk(q, k):
    return lax.dot_general(q, k, (((1,), (1,)), ((), ())), preferred_element_type=F32)


DIFF_HEADS_PER_STEP = 2


def _diff_attn_kernel(q_ref, k_ref, v_ref, lam_ref, g_ref, o_ref, acc_ref, *, lambda_init):
    qi = pl.program_id(1)
    heads = q_ref.shape[1]
    hps = DIFF_HEADS_PER_STEP
    lam = lam_ref[...]
    lam_full = (jnp.exp(jnp.sum(lam[0:1] * lam[1:2], axis=-1, keepdims=True))
                - jnp.exp(jnp.sum(lam[2:3] * lam[3:4], axis=-1, keepdims=True)) + lambda_init)
    mask = _chunk_mask()
    m0 = jnp.full((ATT_T, 1), NEG_BIG, F32)
    l0 = jnp.zeros((ATT_T, 1), F32)

    def group_body(gi, carry):
        acc_ref[...] = jnp.zeros_like(acc_ref)

        def step(kt, st, masked):
            k0 = pl.multiple_of(kt * ATT_T, ATT_T)
            new = []
            for n in range(hps):
                h = gi * hps + n
                v = v_ref[0, h, pl.ds(k0, ATT_T), :]
                for c in range(2):
                    sl = slice(c * LANES, (c + 1) * LANES)
                    s = _qk(q_ref[0, h, :, sl], k_ref[0, h, pl.ds(k0, ATT_T), sl])
                    if masked:
                        s = jnp.where(mask, s, NEG_BIG)
                    e = 2 * n + c
                    m, l = _softmax_step(s, st[2 * e], st[2 * e + 1], acc_ref.at[e], v)
                    new += [m, l]
            return tuple(new)

        st = lax.fori_loop(0, qi, lambda kt, st: step(kt, st, False), (m0, l0) * (2 * hps))
        st = step(qi, st, True)
        for n in range(hps):
            l1 = st[2 * (2 * n) + 1]
            l2 = st[2 * (2 * n + 1) + 1]
            o = acc_ref[2 * n] / l1 - lam_full * (acc_ref[2 * n + 1] / l2)
            o = _rms(o, g_ref[...]) * (1.0 - lambda_init)
            o_ref[gi * hps + n] = o.astype(BF16)
        return carry

    lax.fori_loop(0, heads // hps, group_body, 0)


def _diff_attn_call(qkv, lam, g, lambda_init, batch, seq):
    _, heads, n, _ = qkv.shape
    nq = seq // ATT_T
    kv_spec = lambda which: pl.BlockSpec((1, heads, seq, PAIR), lambda b, qi: (which, 0, b, 0),
                                         pipeline_mode=pl.Buffered(1))
    return pl.pallas_call(
        functools.partial(_diff_attn_kernel, lambda_init=lambda_init),
        grid=(batch, nq),
        in_specs=[
            pl.BlockSpec((1, heads, ATT_T, PAIR), lambda b, qi: (0, 0, b * nq + qi, 0)),
            kv_spec(1),
            kv_spec(2),
            pl.BlockSpec((4, DIFF_HEAD_DIM), lambda b, qi: (0, 0)),
            pl.BlockSpec((1, PAIR), lambda b, qi: (0, 0)),
        ],
        out_specs=pl.BlockSpec((heads, ATT_T, PAIR), lambda b, qi: (0, b * nq + qi, 0)),
        out_shape=jax.ShapeDtypeStruct((heads, n, PAIR), BF16),
        scratch_shapes=[pltpu.VMEM((2 * DIFF_HEADS_PER_STEP, ATT_T, PAIR), F32)],
        compiler_params=_cparams(("parallel", "arbitrary")),
        name="diff_attn",
    )(qkv, qkv, qkv, lam, g)


MLA_TM = 256


def _mla_in_kernel(h_ref, g_ref, sc_ref, sh_ref, win_ref, gq_ref, gkv_ref, wuq_ref, wukv_ref, tab_ref,
                   proj_ref, kr_ref, *, scale):
    u = _norm_mod(h_ref[...], g_ref[...], sc_ref[0], sh_ref[0]).astype(BF16)
    lat = jnp.dot(u, win_ref[0], preferred_element_type=F32)
    qn = _rms(lat[:, 0:MLA_Q_RANK], gq_ref[...]).astype(BF16)
    kvn = _rms(lat[:, MLA_Q_RANK:MLA_Q_RANK + MLA_KV_RANK], gkv_ref[...]).astype(BF16)
    half = MLA_ROPE // 2
    kr = lat[:, MLA_Q_RANK + MLA_KV_RANK:MLA_Q_RANK + MLA_KV_RANK + LANES]
    kr_ref[...] = _rope128(kr, tab_ref[1, 0], tab_ref[1, 1], tab_ref[1, 2], half).astype(BF16)

    npair = proj_ref.shape[1]
    width = npair * PAIR
    q = jnp.dot(qn, wuq_ref[0], preferred_element_type=F32)
    c = tab_ref[0, 0]
    sa = tab_ref[0, 1]
    sb = tab_ref[0, 2]
    for gp in range(npair):
        proj_ref[0, gp] = (q[:, gp * PAIR:(gp + 1) * PAIR] * scale).astype(BF16)
        for a in range(2):
            lo = width + gp * PAIR + a * LANES
            y = _rope128(q[:, lo:lo + LANES], c, sa, sb, half)
            proj_ref[1, gp, :, a * LANES:(a + 1) * LANES] = y.astype(BF16)
    kv = jnp.dot(kvn, wukv_ref[0], preferred_element_type=F32)
    for gp in range(npair):
        proj_ref[2, gp] = kv[:, gp * PAIR:(gp + 1) * PAIR].astype(BF16)
        proj_ref[3, gp] = kv[:, width + gp * PAIR:width + (gp + 1) * PAIR].astype(BF16)


def _mla_in_call(h, g, scale_a, shift_a, w_in, gq, gkv, w_uq, w_ukv, layer, tab, seq, scale):
    n, d = h.shape
    lat_w = w_in.shape[2]
    up_w = w_uq.shape[2]
    npair = up_w // (2 * PAIR)
    tiles_per_b = seq // MLA_TM
    const = lambda shape: pl.BlockSpec(shape, lambda i: (layer,) + (0,) * (len(shape) - 1),
                                       pipeline_mode=pl.Buffered(1))
    return pl.pallas_call(
        functools.partial(_mla_in_kernel, scale=scale),
        grid=(n // MLA_TM,),
        in_specs=[
            pl.BlockSpec((MLA_TM, d), lambda i: (i, 0)),
            pl.BlockSpec((1, d), lambda i: (0, 0)),
            pl.BlockSpec((1, 1, d), lambda i: (i // tiles_per_b, 0, 0)),
            pl.BlockSpec((1, 1, d), lambda i: (i // tiles_per_b, 0, 0)),
            const((1, d, lat_w)),
            pl.BlockSpec((1, MLA_Q_RANK), lambda i: (0, 0)),
            pl.BlockSpec((1, MLA_KV_RANK), lambda i: (0, 0)),
            const((1, MLA_Q_RANK, up_w)),
            const((1, MLA_KV_RANK, up_w)),
            pl.BlockSpec((2, 3, MLA_TM, LANES), lambda i: (0, 0, i, 0)),
        ],
        out_specs=[
            pl.BlockSpec((4, npair, MLA_TM, PAIR), lambda i: (0, 0, i, 0)),
            pl.BlockSpec((MLA_TM, LANES), lambda i: (i, 0)),
        ],
        out_shape=[
            jax.ShapeDtypeStruct((4, npair, n, PAIR), BF16),
            jax.ShapeDtypeStruct((n, LANES), BF16),
        ],
        compiler_params=_cparams(("parallel",)),
        name="mla_in",
    )(h, g, scale_a, shift_a, w_in, gq, gkv, w_uq, w_ukv, tab)


MLA_PAIRS_PER_STEP = 2


def _mla_attn_kernel(q_ref, kv_ref, kr_ref, o_ref, qc_ref, acc_ref):
    qi = pl.program_id(1)
    npair = q_ref.shape[1]
    pps = MLA_PAIRS_PER_STEP
    mask = _chunk_mask()
    m0 = jnp.full((ATT_T, 1), NEG_BIG, F32)
    l0 = jnp.zeros((ATT_T, 1), F32)
    lane_sl = [slice(a * LANES, (a + 1) * LANES) for a in range(2)]

    def group_body(gi, carry):
        acc_ref[...] = jnp.zeros_like(acc_ref)
        for p in range(pps):
            for a in range(2):
                gp = gi * pps + p
                qc_ref[2 * p + a] = jnp.concatenate([q_ref[0, gp, :, lane_sl[a]], q_ref[1, gp, :, lane_sl[a]]],
                                                    axis=-1)

        def step(kt, st, masked):
            k0 = pl.multiple_of(kt * ATT_T, ATT_T)
            kr = kr_ref[pl.ds(k0, ATT_T), :]
            new = []
            for p in range(pps):
                for a in range(2):
                    gp = gi * pps + p
                    e = 2 * p + a
                    kc = jnp.concatenate([kv_ref[0, gp, pl.ds(k0, ATT_T), lane_sl[a]], kr], axis=-1)
                    v = kv_ref[1, gp, pl.ds(k0, ATT_T), lane_sl[a]]
                    s = _qk(qc_ref[e], kc)
                    if masked:
                        s = jnp.where(mask, s, NEG_BIG)
                    m, l = _softmax_step(s, st[2 * e], st[2 * e + 1], acc_ref.at[e], v)
                    new += [m, l]
            return tuple(new)

        st = lax.fori_loop(0, qi, lambda kt, st: step(kt, st, False), (m0, l0) * (2 * pps))
        st = step(qi, st, True)
        for p in range(pps):
            for a in range(2):
                e = 2 * p + a
                o_ref[gi * pps + p, :, lane_sl[a]] = (acc_ref[e] / st[2 * e + 1]).astype(BF16)
        return carry

    lax.fori_loop(0, npair // pps, group_body, 0)


def _mla_attn_call(proj, kr, batch, seq):
    _, npair, n, _ = proj.shape
    nq = seq // ATT_T
    return pl.pallas_call(
        _mla_attn_kernel,
        grid=(batch, nq),
        in_specs=[
            pl.BlockSpec((2, npair, ATT_T, PAIR), lambda b, qi: (0, 0, b * nq + qi, 0)),
            pl.BlockSpec((2, npair, seq, PAIR), lambda b, qi: (1, 0, b, 0), pipeline_mode=pl.Buffered(1)),
            pl.BlockSpec((seq, LANES), lambda b, qi: (b, 0), pipeline_mode=pl.Buffered(1)),
        ],
        out_specs=pl.BlockSpec((npair, ATT_T, PAIR), lambda b, qi: (0, b * nq + qi, 0)),
        out_shape=jax.ShapeDtypeStruct((npair, n, PAIR), BF16),
        scratch_shapes=[pltpu.VMEM((2 * MLA_PAIRS_PER_STEP, ATT_T, PAIR), BF16),
                        pltpu.VMEM((2 * MLA_PAIRS_PER_STEP, ATT_T, LANES), F32)],
        compiler_params=_cparams(("parallel", "arbitrary")),
        name="mla_attn",
    )(proj, proj, kr)


AOUT_TM = 512


def _attn_out_kernel(o_ref, w_ref, h_ref, gate_ref, g_ref, sc_ref, sh_ref, wr_ref, br_ref,
                     hn_ref, u_ref, idx_ref, gw_ref):
    ns = o_ref.shape[0]
    y = jnp.dot(o_ref[0], w_ref[0, 0:PAIR, :], preferred_element_type=F32)
    for s in range(1, ns):
        y = y + jnp.dot(o_ref[s], w_ref[0, s * PAIR:(s + 1) * PAIR, :], preferred_element_type=F32)
    hn = h_ref[...] + gate_ref[0] * y
    hn_ref[...] = hn
    u = _norm_mod(hn, g_ref[...], sc_ref[0], sh_ref[0])
    u_ref[...] = _pack_halves(u)
    u_hi = u.astype(BF16)
    u_lo = (u - u_hi.astype(F32)).astype(BF16)
    logits = jnp.dot(jnp.concatenate([u_hi, u_lo, u_hi], axis=-1), wr_ref[0], preferred_element_type=F32) + br_ref[0]
    tm, ne = logits.shape
    eio = lax.broadcasted_iota(jnp.int32, (tm, ne), 1)
    lane = lax.broadcasted_iota(jnp.int32, (tm, LANES), 1)
    idx_out = jnp.zeros((tm, LANES), jnp.int32)
    val_out = jnp.zeros((tm, LANES), F32)
    l = logits
    v0 = None
    den = None
    for k in range(TOP_K):
        m = jnp.max(l, axis=-1, keepdims=True)
        i = jnp.min(jnp.where(l == m, eio, ne), axis=-1, keepdims=True)
        l = jnp.where(eio == i, -jnp.inf, l)
        if k == 0:
            v0 = m
        e = jnp.exp(m - v0)
        den = e if den is None else den + e
        idx_out = jnp.where(lane == k, i, idx_out)
        val_out = jnp.where(lane == k, e, val_out)
    idx_ref[...] = idx_out
    gw_ref[...] = val_out / den


def _attn_out_call(o, w_out, layer, h, gate, g, scale, shift, w_router, b_router, seq):
    ns, n, _ = o.shape
    d = h.shape[1]
    ne = w_router.shape[2]
    tiles_per_b = seq // AOUT_TM
    perb = lambda: pl.BlockSpec((1, 1, d), lambda i: (i // tiles_per_b, 0, 0))
    return pl.pallas_call(
        _attn_out_kernel,
        grid=(n // AOUT_TM,),
        in_specs=[
            pl.BlockSpec((ns, AOUT_TM, PAIR), lambda i: (0, i, 0)),
            pl.BlockSpec((1, d, d), lambda i: (layer, 0, 0), pipeline_mode=pl.Buffered(1)),
            pl.BlockSpec((AOUT_TM, d), lambda i: (i, 0)),
            perb(),
            pl.BlockSpec((1, d), lambda i: (0, 0)),
            perb(),
            perb(),
            pl.BlockSpec((1, 3 * d, ne), lambda i: (0, 0, 0)),
            pl.BlockSpec((1, 1, ne), lambda i: (0, 0, 0)),
        ],
        out_specs=[
            pl.BlockSpec((AOUT_TM, d), lambda i: (i, 0)),
            pl.BlockSpec((AOUT_TM, d // 2), lambda i: (i, 0)),
            pl.BlockSpec((AOUT_TM, LANES), lambda i: (i, 0)),
            pl.BlockSpec((AOUT_TM, LANES), lambda i: (i, 0)),
        ],
        out_shape=[
            jax.ShapeDtypeStruct((n, d), F32),
            jax.ShapeDtypeStruct((n, d // 2), jnp.uint32),
            jax.ShapeDtypeStruct((n, LANES), jnp.int32),
            jax.ShapeDtypeStruct((n, LANES), F32),
        ],
        compiler_params=_cparams(("parallel",)),
        name="attn_out",
    )(o, w_out, h, gate, g, scale, shift, w_router, b_router)


def _route_kernel(idx_ref, dest_ref, pt_ref, carry_ref, ps_ref):
    ph = pl.program_id(0)
    t = pl.program_id(1)
    nt = pl.num_programs(1)
    tb = idx_ref.shape[0]
    idx = idx_ref[...]
    eio = lax.broadcasted_iota(jnp.int32, (tb, LANES), 1)
    sel = [eio == idx[:, k:k + 1] for k in range(TOP_K)]
    oh = jnp.where(sel[0], 1.0, 0.0)
    for k in range(1, TOP_K):
        oh = oh + jnp.where(sel[k], 1.0, 0.0)

    @pl.when(t == 0)
    def _():
        carry_ref[...] = jnp.zeros_like(carry_ref)

    @pl.when((ph == 0) & (t == nt - 1))
    def _():
        cnt = carry_ref[...] + jnp.sum(oh, axis=0, keepdims=True)
        tiles = jnp.floor((cnt + (EXPERT_TM - 1)) * (1.0 / EXPERT_TM))
        r = lax.broadcasted_iota(jnp.int32, (LANES, LANES), 0)
        c = lax.broadcasted_iota(jnp.int32, (LANES, LANES), 1)
        upper = jnp.where(r < c, 1.0, 0.0).astype(BF16)
        t8 = jnp.broadcast_to(tiles, (SUBLANES, LANES))
        start = jnp.dot(t8.astype(BF16), upper, preferred_element_type=F32)
        ps_ref[...] = start[0:1] * EXPERT_TM
        row = lax.broadcasted_iota(jnp.int32, (SUBLANES, LANES), 0)
        pt_ref[...] = jnp.where(row == 0, t8, jnp.broadcast_to(cnt, (SUBLANES, LANES)))

    @pl.when(ph == 1)
    def _():
        r = lax.broadcasted_iota(jnp.int32, (tb, tb), 0)
        c = lax.broadcasted_iota(jnp.int32, (tb, tb), 1)
        lower = jnp.where(c < r, 1.0, 0.0).astype(BF16)
        before = jnp.dot(lower, oh.astype(BF16), preferred_element_type=F32)
        base = before + carry_ref[...] + ps_ref[...]
        out = jnp.zeros((tb, LANES), jnp.int32)
        for k in range(TOP_K):
            dk = jnp.sum(jnp.where(sel[k], base, 0.0), axis=-1, keepdims=True)
            out = jnp.where(eio == k, dk.astype(jnp.int32), out)
        dest_ref[...] = out

    carry_ref[...] += jnp.sum(oh, axis=0, keepdims=True)


def _route_call(idx):
    n = idx.shape[0]
    return pl.pallas_call(
        _route_kernel,
        grid=(2, n // ROUTE_TB),
        in_specs=[pl.BlockSpec((ROUTE_TB, LANES), lambda p, t: (t, 0))],
        out_specs=[
            pl.BlockSpec((ROUTE_TB, LANES), lambda p, t: (p * t, 0)),
            pl.BlockSpec((SUBLANES, LANES), lambda p, t: (0, 0)),
        ],
        out_shape=[
            jax.ShapeDtypeStruct((n, LANES), jnp.int32),
            jax.ShapeDtypeStruct((SUBLANES, LANES), F32),
        ],
        scratch_shapes=[pltpu.VMEM((1, LANES), F32), pltpu.VMEM((1, LANES), F32)],
        compiler_params=_cparams(("arbitrary", "arbitrary")),
        name="route",
    )(idx)


DISPATCH_ZROWS = 256
assert EXPERT_TM % DISPATCH_ZROWS == 0


def _dispatch_kernel(dest_ref, pstart_ref, cnt_ref, pend_ref, u_ref, xb_ref, zbuf_ref, sem, zsem):
    i = pl.program_id(0)
    tt = u_ref.shape[0]
    rows = xb_ref.shape[0]
    ne = pstart_ref.shape[0]

    def zero_copy(pos, nrows):
        return pltpu.make_async_copy(zbuf_ref.at[pl.ds(0, nrows)], xb_ref.at[pl.ds(pos, nrows)], zsem)

    def pad_fill(act):
        def expert_body(e, carry):
            pos = pstart_ref[e] + cnt_ref[e]
            npad = pend_ref[e] - pos
            head = jnp.minimum(npad, (-pos) & (SUBLANES - 1))

            def head_body(r, c):
                act(zero_copy(pos + r, 1))
                return c

            lax.fori_loop(0, head, head_body, 0)
            base = pos + head
            rem = npad - head
            bit = DISPATCH_ZROWS
            while bit >= SUBLANES:
                off = rem - (rem & (2 * bit - 1))

                @pl.when((rem & bit) != 0)
                def _(off=off, bit=bit):
                    act(zero_copy(pl.multiple_of(base + off, SUBLANES), bit))

                bit //= 2
            return carry

        lax.fori_loop(0, ne, expert_body, 0)
        tail0 = pend_ref[ne - 1]

        def tail_body(b, c):
            act(zero_copy(pl.multiple_of(tail0 + b * DISPATCH_ZROWS, DISPATCH_ZROWS), DISPATCH_ZROWS))
            return c

        lax.fori_loop(0, (rows - tail0) // DISPATCH_ZROWS, tail_body, 0)

    @pl.when(i == 0)
    def _():
        zbuf_ref[...] = jnp.zeros_like(zbuf_ref)
        pad_fill(lambda c: c.start())

    def row_copy(t, d):
        return pltpu.make_async_copy(u_ref.at[pl.ds(t, 1)], xb_ref.at[pl.ds(d, 1)], sem)

    def issue(t, carry):
        for k in range(TOP_K):
            row_copy(t, dest_ref[(i * tt + t) * TOP_K + k]).start()
        return carry

    def drain(t, carry):
        for k in range(TOP_K):
            row_copy(t, dest_ref[(i * tt + t) * TOP_K + k]).wait()
        return carry

    lax.fori_loop(0, tt, issue, 0)
    lax.fori_loop(0, tt, drain, 0)

    @pl.when(i == 0)
    def _():
        pad_fill(lambda c: c.wait())


def _dispatch_call(dest_flat, pstart, cnt, pend, u, rows):
    n, d = u.shape
    return pl.pallas_call(
        _dispatch_kernel,
        grid_spec=pltpu.PrefetchScalarGridSpec(
            num_scalar_prefetch=4,
            grid=(n // DISPATCH_TT,),
            in_specs=[pl.BlockSpec((DISPATCH_TT, d), lambda i, *_: (i, 0))],
            out_specs=pl.BlockSpec(memory_space=pl.ANY),
            scratch_shapes=[pltpu.VMEM((DISPATCH_ZROWS, d), u.dtype),
                            pltpu.SemaphoreType.DMA(()), pltpu.SemaphoreType.DMA(())],
        ),
        out_shape=jax.ShapeDtypeStruct((rows, d), u.dtype),
        compiler_params=_cparams(("arbitrary",), has_side_effects=True),
        name="dispatch",
    )(dest_flat, pstart, cnt, pend, u)


def _expert_kernel(te_ref, first_ref, nu_ref, x_ref, wg_ref, wu_ref, bg_ref, bu_ref, wd_ref, bd_ref, o_ref,
                   xbf_ref, wgb_ref, wub_ref, wdb_ref, act_ref):
    del te_ref
    i = pl.program_id(0)
    j = pl.program_id(1)
    nf = act_ref.shape[0]
    half = x_ref.shape[1]

    @pl.when(i < nu_ref[0])
    def _():
        @pl.when(j == 0)
        def _():
            lo, hi = _unpack_halves(x_ref[...])
            xbf_ref[:, 0:half] = lo.astype(BF16)
            xbf_ref[:, half:2 * half] = hi.astype(BF16)

        @pl.when(first_ref[i] == 1)
        def _():
            wgb_ref[j] = wg_ref[0, 0].astype(BF16)
            wub_ref[j] = wu_ref[0, 0].astype(BF16)
            wdb_ref[j] = wd_ref[0, 0].astype(BF16)

        x = xbf_ref[...]
        gate = jnp.dot(x, wgb_ref[j], preferred_element_type=F32) + bg_ref[0, 0]
        up = jnp.dot(x, wub_ref[j], preferred_element_type=F32) + bu_ref[0, 0]
        gate = jnp.minimum(gate, SWIGLU_LIMIT)
        up = jnp.clip(up, -SWIGLU_LIMIT, SWIGLU_LIMIT)
        glu = gate * _sigmoid(gate * SWIGLU_ALPHA)
        act_ref[j] = ((up + 1.0) * glu).astype(BF16)

        @pl.when(j == nf - 1)
        def _():
            y = jnp.dot(act_ref[0], wdb_ref[0], preferred_element_type=F32)
            for c in range(1, nf):
                y = y + jnp.dot(act_ref[c], wdb_ref[c], preferred_element_type=F32)
            o_ref[...] = _pack_halves(y + bd_ref[0, 0])

    @pl.when((i >= nu_ref[0]) & (j == 0))
    def _():
        o_ref[...] = jnp.zeros_like(o_ref)


def _expert_call(tile_e, first, n_used, xb, w_gu, b_gu, w_down, b_down, layer):
    rows, dw = xb.shape
    d = 2 * dw
    ne = w_gu.shape[1]
    f = w_down.shape[2]
    nf = f // EXPERT_TF
    n_tiles = rows // EXPERT_TM

    def tile(i, nu):
        return jnp.minimum(i, nu[0] - 1)

    def fcol(i, j, nu):
        return jnp.where(i < nu[0], j, nf - 1)

    def wcol(i, j, fi, nu):
        return jnp.where((i < nu[0]) & (fi[tile(i, nu)] == 1), j, nf - 1)

    return pl.pallas_call(
        _expert_kernel,
        grid_spec=pltpu.PrefetchScalarGridSpec(
            num_scalar_prefetch=3,
            grid=(n_tiles, nf),
            in_specs=[
                pl.BlockSpec((EXPERT_TM, dw), lambda i, j, te, fi, nu: (tile(i, nu), 0)),
                pl.BlockSpec((1, 1, d, EXPERT_TF),
                             lambda i, j, te, fi, nu: (layer, te[tile(i, nu)], 0, wcol(i, j, fi, nu))),
                pl.BlockSpec((1, 1, d, EXPERT_TF),
                             lambda i, j, te, fi, nu: (layer, te[tile(i, nu)], 0, nf + wcol(i, j, fi, nu))),
                pl.BlockSpec((1, 1, 1, EXPERT_TF),
                             lambda i, j, te, fi, nu: (layer, te[tile(i, nu)], 0, fcol(i, j, nu))),
                pl.BlockSpec((1, 1, 1, EXPERT_TF),
                             lambda i, j, te, fi, nu: (layer, te[tile(i, nu)], 0, nf + fcol(i, j, nu))),
                pl.BlockSpec((1, 1, EXPERT_TF, d),
                             lambda i, j, te, fi, nu: (layer, te[tile(i, nu)], wcol(i, j, fi, nu), 0)),
                pl.BlockSpec((1, 1, 1, d), lambda i, j, te, fi, nu: (layer, te[tile(i, nu)], 0, 0)),
            ],
            out_specs=pl.BlockSpec((EXPERT_TM, dw), lambda i, j, te, fi, nu: (i, 0)),
            scratch_shapes=[pltpu.VMEM((EXPERT_TM, d), BF16),
                            pltpu.VMEM((nf, d, EXPERT_TF), BF16),
                            pltpu.VMEM((nf, d, EXPERT_TF), BF16),
                            pltpu.VMEM((nf, EXPERT_TF, d), BF16),
                            pltpu.VMEM((nf, EXPERT_TM, EXPERT_TF), BF16)],
        ),
        out_shape=jax.ShapeDtypeStruct((rows, dw), jnp.uint32),
        compiler_params=_cparams(("arbitrary", "arbitrary")),
        name="experts",
    )(tile_e, first, n_used, xb, w_gu, w_gu, b_gu.reshape(b_gu.shape[0], ne, 1, 2 * f),
      b_gu.reshape(b_gu.shape[0], ne, 1, 2 * f), w_down, b_down.reshape(b_down.shape[0], ne, 1, d))


def _combine_kernel(dest_ref, yb_ref, gw_ref, h_ref, gate_ref, g_ref, o_ref, buf_ref, sem, *, final):
    i = pl.program_id(0)
    nsteps = pl.num_programs(0)
    tt = h_ref.shape[0]
    slot = i % 2

    def row_copy(tile, sl, t, k):
        d = dest_ref[(tile * tt + t) * TOP_K + k]
        return pltpu.make_async_copy(yb_ref.at[pl.ds(d, 1)], buf_ref.at[sl, k, pl.ds(t, 1)], sem.at[sl])

    def issue(tile, sl):
        def body(t, carry):
            for k in range(TOP_K):
                row_copy(tile, sl, t, k).start()
            return carry
        lax.fori_loop(0, tt, body, 0)

    def drain(tile, sl):
        def body(t, carry):
            for k in range(TOP_K):
                row_copy(tile, sl, t, k).wait()
            return carry
        lax.fori_loop(0, tt, body, 0)

    @pl.when(i == 0)
    def _():
        issue(0, 0)

    @pl.when(i + 1 < nsteps)
    def _():
        issue(i + 1, 1 - slot)

    drain(i, slot)
    gw = gw_ref[...]
    y_lo, y_hi = None, None
    for k in range(TOP_K):
        lo, hi = _unpack_halves(buf_ref[slot, k])
        wk = gw[:, k:k + 1]
        y_lo = wk * lo if y_lo is None else y_lo + wk * lo
        y_hi = wk * hi if y_hi is None else y_hi + wk * hi
    hn = h_ref[...] + gate_ref[0] * jnp.concatenate([y_lo, y_hi], axis=-1)
    if final:
        hn = _rms(hn, g_ref[...])
    o_ref[...] = hn


def _combine_call(dest_flat, yb, gw, h, gate, g_final, seq, final):
    n, d = h.shape
    tiles_per_b = seq // COMBINE_TT
    return pl.pallas_call(
        functools.partial(_combine_kernel, final=final),
        grid_spec=pltpu.PrefetchScalarGridSpec(
            num_scalar_prefetch=1,
            grid=(n // COMBINE_TT,),
            in_specs=[
                pl.BlockSpec(memory_space=pl.ANY),
                pl.BlockSpec((COMBINE_TT, LANES), lambda i, dest: (i, 0)),
                pl.BlockSpec((COMBINE_TT, d), lambda i, dest: (i, 0)),
                pl.BlockSpec((1, 1, d), lambda i, dest: (i // tiles_per_b, 0, 0)),
                pl.BlockSpec((1, d), lambda i, dest: (0, 0)),
            ],
            out_specs=pl.BlockSpec((COMBINE_TT, d), lambda i, dest: (i, 0)),
            scratch_shapes=[pltpu.VMEM((2, TOP_K, COMBINE_TT, d // 2), jnp.uint32),
                            pltpu.SemaphoreType.DMA((2,))],
        ),
        out_shape=jax.ShapeDtypeStruct((n, d), F32),
        compiler_params=_cparams(("arbitrary",)),
        name="combine",
    )(dest_flat, yb, gw, h, gate, g_final)


def _rope_tables(positions, rot_dim, scale):
    half = rot_dim // 2
    inv_freq = ROPE_THETA ** (-jnp.arange(0, rot_dim, 2, dtype=F32) / rot_dim)
    ang = positions.reshape(-1).astype(F32)[:, None] * inv_freq
    cos, sin = jnp.cos(ang), jnp.sin(ang)
    n = ang.shape[0]
    c = jnp.concatenate([cos, cos, jnp.ones((n, LANES - rot_dim), F32)], axis=1)
    sa = jnp.concatenate([-sin, jnp.zeros((n, LANES - half), F32)], axis=1)
    sb = jnp.concatenate([jnp.zeros((n, half), F32), sin, jnp.zeros((n, LANES - rot_dim), F32)], axis=1)
    t = jnp.stack([c, sa, sb])
    return jnp.stack([t * scale, t])


def _moe(layer, h, u, idx, gw, gate_f, g_final, final, seq, w_gu, b_gu, w_down, b_down):
    n, d = h.shape
    rows = -(-(n * TOP_K + N_EXPERTS * (EXPERT_TM - 1)) // EXPERT_TM) * EXPERT_TM
    n_tiles = rows // EXPERT_TM
    dest, stats = _route_call(idx)
    dest_flat = dest[:, :TOP_K].reshape(-1)
    cnt = stats[1, :N_EXPERTS].astype(jnp.int32)
    tile_end = jnp.cumsum(stats[0, :N_EXPERTS].astype(jnp.int32))
    pend = tile_end * EXPERT_TM
    pstart = jnp.concatenate([jnp.zeros((1,), jnp.int32), pend[:-1]])
    tile_e = jnp.minimum(jnp.searchsorted(tile_end, jnp.arange(n_tiles, dtype=jnp.int32), side='right'),
                         N_EXPERTS - 1).astype(jnp.int32)
    first = jnp.concatenate([jnp.ones((1,), jnp.int32), (tile_e[1:] != tile_e[:-1]).astype(jnp.int32)])
    n_used = tile_end[-1:].astype(jnp.int32)
    xb = _dispatch_call(dest_flat, pstart, cnt, pend, u, rows)
    yb = _expert_call(tile_e, first, n_used, xb, w_gu, b_gu, w_down, b_down, layer)
    return _combine_call(dest_flat, yb, gw, h, gate_f, g_final, seq, final)


def kernel(x, c, positions, ada_w, ada_b, mix_norm_g, ffn_norm_g, final_norm_g, diff_w_in, diff_lambda, diff_subln_g, diff_w_out, mla_w_in, mla_q_norm_g, mla_kv_norm_g, mla_w_uq, mla_w_ukv, mla_w_out, moe_w_router, moe_b_router, moe_w_gate_up, moe_b_gate_up, moe_w_down, moe_b_down):
    batch, seq, d = x.shape
    depth = ada_w.shape[0]
    n = batch * seq
    assert seq % ATT_T == 0 and ATT_T % CHUNK == 0 and d % PAIR == 0
    assert n % DISPATCH_TT == 0 and n % ROUTE_TB == 0 and seq % DIN_TM == 0

    diff_scale = DIFF_HEAD_DIM ** -0.5
    mla_scale = (MLA_NOPE + MLA_ROPE) ** -0.5
    tab_d = _rope_tables(positions, DIFF_HEAD_DIM // 4, diff_scale)
    tab_m = _rope_tables(positions, MLA_ROPE, mla_scale)

    mod = _mod_call(c, ada_w, ada_b)
    mod = mod.reshape(depth, batch, N_MOD, 1, d)

    diff_w_in_bf = diff_w_in.astype(BF16)
    diff_w_out_bf = diff_w_out.astype(BF16)
    mla_w_out_bf = mla_w_out.astype(BF16)
    nm = mla_w_in.shape[0]
    heads_m = mla_w_out.shape[1] // MLA_V
    lat_pad = LANES - MLA_ROPE
    mla_w_in_bf = jnp.pad(mla_w_in, ((0, 0), (0, 0), (0, lat_pad))).astype(BF16)
    wq = mla_w_uq.reshape(nm, MLA_Q_RANK, heads_m, MLA_NOPE + MLA_ROPE)
    wq_rope = jnp.pad(wq[..., MLA_NOPE:], ((0, 0), (0, 0), (0, 0), (0, LANES - MLA_ROPE)))
    mla_w_uq_bf = jnp.concatenate([wq[..., :MLA_NOPE].reshape(nm, MLA_Q_RANK, -1),
                                   wq_rope.reshape(nm, MLA_Q_RANK, -1)], axis=-1).astype(BF16)
    wkv = mla_w_ukv.reshape(nm, MLA_KV_RANK, heads_m, MLA_NOPE + MLA_V)
    mla_w_ukv_bf = jnp.concatenate([wkv[..., :MLA_NOPE].reshape(nm, MLA_KV_RANK, -1),
                                    wkv[..., MLA_NOPE:].reshape(nm, MLA_KV_RANK, -1)], axis=-1).astype(BF16)

    h = x.reshape(n, d)
    for i in range(depth):
        shift_a, scale_a, gate_a, shift_f, scale_f, gate_f = [mod[i, :, m] for m in range(N_MOD)]
        g_mix = mix_norm_g[i].reshape(1, d)
        g_ffn = ffn_norm_g[i].reshape(1, d)
        j = i // N_MIXERS
        if i % N_MIXERS == 0:
            lambda_init = 0.8 - 0.6 * math.exp(-0.3 * i)
            qkv = _diff_in_call(h, g_mix, scale_a, shift_a, diff_w_in_bf, j, tab_d, seq)
            o = _diff_attn_call(qkv, diff_lambda[j], diff_subln_g[j].reshape(1, -1), lambda_init, batch, seq)
            w_out = diff_w_out_bf
        else:
            proj, kr = _mla_in_call(h, g_mix, scale_a, shift_a, mla_w_in_bf, mla_q_norm_g[j].reshape(1, -1),
                                    mla_kv_norm_g[j].reshape(1, -1), mla_w_uq_bf, mla_w_ukv_bf, j, tab_m, seq,
                                    mla_scale)
            o = _mla_attn_call(proj, kr, batch, seq)
            w_out = mla_w_out_bf
        wr_hi = moe_w_router[i:i + 1].astype(BF16)
        wr_lo = (moe_w_router[i:i + 1] - wr_hi.astype(F32)).astype(BF16)
        wr3 = jnp.concatenate([wr_hi, wr_hi, wr_lo], axis=1)
        h, u, idx, gw = _attn_out_call(o, w_out, j, h, gate_a, g_ffn, scale_f, shift_f,
                                       wr3, moe_b_router[i:i + 1].reshape(1, 1, -1), seq)
        h = _moe(i, h, u, idx, gw, gate_f, final_norm_g.reshape(1, d), i == depth - 1, seq,
                 moe_w_gate_up, moe_b_gate_up, moe_w_down, moe_b_down)
    return h.reshape(batch, seq, d)
```

```python
import functools
import math

import jax
import jax.numpy as jnp
from jax import lax
from jax.experimental import pallas as pl
from jax.experimental.pallas import tpu as pltpu

F32 = jnp.float32
BF16 = jnp.bfloat16

CHUNK = 64
N_MIXERS = 2
ROPE_THETA = 500000.0
RMS_EPS = 1e-6
N_MOD = 6
DIFF_HEAD_DIM = 128
MLA_V = 128
MLA_Q_RANK = 512
MLA_KV_RANK = 512
MLA_NOPE = 128
MLA_ROPE = 64
N_EXPERTS = 32
TOP_K = 4
SWIGLU_LIMIT = 7.0
SWIGLU_ALPHA = 1.702

LANES = 128
SUBLANES = 8
V7X_VMEM_BYTES = 64 * 1024 * 1024
VMEM_LIMIT = 56 * 1024 * 1024

NEG_BIG = -1e30

ATT_T = 256
PAIR = 256
EXPERT_TM = 576
EXPERT_TF = 256
ROUTE_TB = 512
DISPATCH_TT = 512
COMBINE_TT = 256


def _cparams(sem, **kw):
    return pltpu.CompilerParams(dimension_semantics=sem, vmem_limit_bytes=VMEM_LIMIT, **kw)


def _sigmoid(x):
    return 1.0 / (1.0 + jnp.exp(-x))


def _rms(x, g):
    ms = jnp.mean(x * x, axis=-1, keepdims=True)
    return x * lax.rsqrt(ms + RMS_EPS) * g


def _norm_mod(x, g, scale, shift):
    return _rms(x, g) * (1.0 + scale) + shift


def _pack_halves(x):
    w = x.shape[-1] // 2
    lo = lax.bitcast_convert_type(x[:, :w].astype(BF16).astype(F32), jnp.uint32)
    hi = lax.bitcast_convert_type(x[:, w:].astype(BF16).astype(F32), jnp.uint32)
    return (lo >> 16) | (hi & jnp.uint32(0xFFFF0000))


def _unpack_halves(words):
    lo = lax.bitcast_convert_type(words << 16, F32)
    hi = lax.bitcast_convert_type(words & jnp.uint32(0xFFFF0000), F32)
    return lo, hi


def _rope128(x, c, sa, sb, half):
    return x * c + pltpu.roll(x, LANES - half, 1) * sa + pltpu.roll(x, half, 1) * sb


MOD_TN = 512
MOD_KC = 64


def _mod_kernel(cb_ref, w_ref, b_ref, o_ref, cs_ref):
    nb = cb_ref.shape[0]
    d = cb_ref.shape[1]
    nj = MOD_TN // LANES

    @pl.when((pl.program_id(0) == 0) & (pl.program_id(1) == 0))
    def _():
        c = cb_ref[...]
        cs_ref[...] = c * _sigmoid(c)

    def body(kc, accs):
        k0 = pl.multiple_of(kc * MOD_KC, MOD_KC)
        w = w_ref[0, pl.ds(k0, MOD_KC), :]
        out = []
        for b in range(nb):
            cb = cs_ref[b, pl.ds(k0, MOD_KC), :].reshape(MOD_KC // SUBLANES, SUBLANES, LANES)
            row = []
            for jj in range(nj):
                w3 = w[:, jj * LANES:(jj + 1) * LANES].reshape(MOD_KC // SUBLANES, SUBLANES, LANES)
                row.append(accs[b][jj] + jnp.sum(w3 * cb, axis=0))
            out.append(tuple(row))
        return tuple(out)

    zero = jnp.zeros((SUBLANES, LANES), F32)
    init = tuple(tuple(zero for _ in range(nj)) for _ in range(nb))
    accs = lax.fori_loop(0, d // MOD_KC, body, init)
    for b in range(nb):
        for jj in range(nj):
            r = jnp.sum(accs[b][jj], axis=0, keepdims=True)
            o_ref[0, b:b + 1, jj * LANES:(jj + 1) * LANES] = r + b_ref[0, :, jj * LANES:(jj + 1) * LANES]


def _mod_call(c, ada_w, ada_b):
    depth, d, n6 = ada_w.shape
    nb = c.shape[0]
    cb = jnp.broadcast_to(c[:, :, None], (nb, d, LANES))
    return pl.pallas_call(
        _mod_kernel,
        grid=(depth, n6 // MOD_TN),
        in_specs=[
            pl.BlockSpec((nb, d, LANES), lambda l, j: (0, 0, 0)),
            pl.BlockSpec((1, d, MOD_TN), lambda l, j: (l, 0, j)),
            pl.BlockSpec((1, 1, MOD_TN), lambda l, j: (l, 0, j)),
        ],
        out_specs=pl.BlockSpec((1, nb, MOD_TN), lambda l, j: (l, 0, j)),
        out_shape=jax.ShapeDtypeStruct((depth, nb, n6), F32),
        scratch_shapes=[pltpu.VMEM((nb, d, LANES), F32)],
        compiler_params=_cparams(("arbitrary", "arbitrary")),
        name="adaln_mod",
    )(cb, ada_w, ada_b.reshape(depth, 1, n6))


DIN_TM = 512
DIN_TN = 1024


def _diff_in_kernel(h_ref, g_ref, sc_ref, sh_ref, w_ref, tab_ref, o_ref, u_ref, *, nqb):
    j = pl.program_id(1)
    nh = DIN_TN // PAIR

    @pl.when(j == 0)
    def _():
        u_ref[...] = _norm_mod(h_ref[...], g_ref[...], sc_ref[0], sh_ref[0]).astype(BF16)

    def slab(hh):
        return jnp.dot(u_ref[...], w_ref[0, :, hh * PAIR:(hh + 1) * PAIR], preferred_element_type=F32)

    @pl.when(j < 2 * nqb)
    def _():
        c = tab_ref[0, 0]
        sa = tab_ref[0, 1]
        sb = tab_ref[0, 2]
        for hh in range(nh):
            acc = slab(hh)
            for half in range(2):
                y = _rope128(acc[:, half * LANES:(half + 1) * LANES], c, sa, sb, DIFF_HEAD_DIM // 8)
                o_ref[0, hh, :, half * LANES:(half + 1) * LANES] = y.astype(BF16)

    @pl.when(j >= 2 * nqb)
    def _():
        for hh in range(nh):
            o_ref[0, hh] = slab(hh).astype(BF16)


def _diff_in_call(h, g, scale, shift, w_bf, layer, tab, seq):
    n, d = h.shape
    nout = w_bf.shape[2]
    third = nout // 3
    nqb = third // DIN_TN
    heads = third // PAIR
    hpb = DIN_TN // PAIR
    tiles_per_b = seq // DIN_TM
    return pl.pallas_call(
        functools.partial(_diff_in_kernel, nqb=nqb),
        grid=(n // DIN_TM, nout // DIN_TN),
        in_specs=[
            pl.BlockSpec((DIN_TM, d), lambda i, j: (i, 0)),
            pl.BlockSpec((1, d), lambda i, j: (0, 0)),
            pl.BlockSpec((1, 1, d), lambda i, j: (i // tiles_per_b, 0, 0)),
            pl.BlockSpec((1, 1, d), lambda i, j: (i // tiles_per_b, 0, 0)),
            pl.BlockSpec((1, d, DIN_TN), lambda i, j: (layer, 0, j)),
            pl.BlockSpec((1, 3, DIN_TM, LANES), lambda i, j: (jnp.minimum(j // nqb, 1), 0, i, 0)),
        ],
        out_specs=pl.BlockSpec((1, hpb, DIN_TM, PAIR), lambda i, j: (j // nqb, j % nqb, i, 0)),
        out_shape=jax.ShapeDtypeStruct((3, heads, n, PAIR), BF16),
        scratch_shapes=[pltpu.VMEM((DIN_TM, d), BF16)],
        compiler_params=_cparams(("parallel", "arbitrary")),
        name="diff_in",
    )(h, g, scale, shift, w_bf, tab)


def _chunk_mask():
    r = lax.broadcasted_iota(jnp.int32, (ATT_T, ATT_T), 0) // CHUNK
    c = lax.broadcasted_iota(jnp.int32, (ATT_T, ATT_T), 1) // CHUNK
    return c <= r


def _softmax_step(s, m, l, acc_ref, v):
    m_new = jnp.maximum(m, jnp.max(s, axis=-1, keepdims=True))
    alpha = jnp.exp(m - m_new)
    p = jnp.exp(s - m_new)
    l_new = alpha * l + jnp.sum(p, axis=-1, keepdims=True)
    acc_ref[...] = alpha * acc_ref[...] + jnp.dot(p.astype(BF16), v, preferred_element_type=F32)
    return m_new, l_new


def _qk(q, k):
    return lax.dot_general(q, k, (((1,), (1,)), ((), ())), preferred_element_type=F32)


DIFF_HEADS_PER_STEP = 2


def _diff_attn_kernel(q_ref, k_ref, v_ref, lam_ref, g_ref, o_ref, acc_ref, *, lambda_init):
    qi = pl.program_id(1)
    heads = q_ref.shape[1]
    hps = DIFF_HEADS_PER_STEP
    lam = lam_ref[...]
    lam_full = (jnp.exp(jnp.sum(lam[0:1] * lam[1:2], axis=-1, keepdims=True))
                - jnp.exp(jnp.sum(lam[2:3] * lam[3:4], axis=-1, keepdims=True)) + lambda_init)
    mask = _chunk_mask()
    m0 = jnp.full((ATT_T, 1), NEG_BIG, F32)
    l0 = jnp.zeros((ATT_T, 1), F32)

    def group_body(gi, carry):
        acc_ref[...] = jnp.zeros_like(acc_ref)

        def step(kt, st, masked):
            k0 = pl.multiple_of(kt * ATT_T, ATT_T)
            new = []
            for n in range(hps):
                h = gi * hps + n
                v = v_ref[0, h, pl.ds(k0, ATT_T), :]
                for c in range(2):
                    sl = slice(c * LANES, (c + 1) * LANES)
                    s = _qk(q_ref[0, h, :, sl], k_ref[0, h, pl.ds(k0, ATT_T), sl])
                    if masked:
                        s = jnp.where(mask, s, NEG_BIG)
                    e = 2 * n + c
                    m, l = _softmax_step(s, st[2 * e], st[2 * e + 1], acc_ref.at[e], v)
                    new += [m, l]
            return tuple(new)

        st = lax.fori_loop(0, qi, lambda kt, st: step(kt, st, False), (m0, l0) * (2 * hps))
        st = step(qi, st, True)
        for n in range(hps):
            l1 = st[2 * (2 * n) + 1]
            l2 = st[2 * (2 * n + 1) + 1]
            o = acc_ref[2 * n] / l1 - lam_full * (acc_ref[2 * n + 1] / l2)
            o = _rms(o, g_ref[...]) * (1.0 - lambda_init)
            o_ref[gi * hps + n] = o.astype(BF16)
        return carry

    lax.fori_loop(0, heads // hps, group_body, 0)


def _diff_attn_call(qkv, lam, g, lambda_init, batch, seq):
    _, heads, n, _ = qkv.shape
    nq = seq // ATT_T
    kv_spec = lambda which: pl.BlockSpec((1, heads, seq, PAIR), lambda b, qi: (which, 0, b, 0),
                                         pipeline_mode=pl.Buffered(1))
    return pl.pallas_call(
        functools.partial(_diff_attn_kernel, lambda_init=lambda_init),
        grid=(batch, nq),
        in_specs=[
            pl.BlockSpec((1, heads, ATT_T, PAIR), lambda b, qi: (0, 0, b * nq + qi, 0)),
            kv_spec(1),
            kv_spec(2),
            pl.BlockSpec((4, DIFF_HEAD_DIM), lambda b, qi: (0, 0)),
            pl.BlockSpec((1, PAIR), lambda b, qi: (0, 0)),
        ],
        out_specs=pl.BlockSpec((heads, ATT_T, PAIR), lambda b, qi: (0, b * nq + qi, 0)),
        out_shape=jax.ShapeDtypeStruct((heads, n, PAIR), BF16),
        scratch_shapes=[pltpu.VMEM((2 * DIFF_HEADS_PER_STEP, ATT_T, PAIR), F32)],
        compiler_params=_cparams(("parallel", "arbitrary")),
        name="diff_attn",
    )(qkv, qkv, qkv, lam, g)


MLA_TM = 256


def _mla_in_kernel(h_ref, g_ref, sc_ref, sh_ref, win_ref, gq_ref, gkv_ref, wuq_ref, wukv_ref, tab_ref,
                   proj_ref, kr_ref, *, scale):
    u = _norm_mod(h_ref[...], g_ref[...], sc_ref[0], sh_ref[0]).astype(BF16)
    lat = jnp.dot(u, win_ref[0], preferred_element_type=F32)
    qn = _rms(lat[:, 0:MLA_Q_RANK], gq_ref[...]).astype(BF16)
    kvn = _rms(lat[:, MLA_Q_RANK:MLA_Q_RANK + MLA_KV_RANK], gkv_ref[...]).astype(BF16)
    half = MLA_ROPE // 2
    kr = lat[:, MLA_Q_RANK + MLA_KV_RANK:MLA_Q_RANK + MLA_KV_RANK + LANES]
    kr_ref[...] = _rope128(kr, tab_ref[1, 0], tab_ref[1, 1], tab_ref[1, 2], half).astype(BF16)

    npair = proj_ref.shape[1]
    width = npair * PAIR
    q = jnp.dot(qn, wuq_ref[0], preferred_element_type=F32)
    c = tab_ref[0, 0]
    sa = tab_ref[0, 1]
    sb = tab_ref[0, 2]
    for gp in range(npair):
        proj_ref[0, gp] = (q[:, gp * PAIR:(gp + 1) * PAIR] * scale).astype(BF16)
        for a in range(2):
            lo = width + gp * PAIR + a * LANES
            y = _rope128(q[:, lo:lo + LANES], c, sa, sb, half)
            proj_ref[1, gp, :, a * LANES:(a + 1) * LANES] = y.astype(BF16)
    kv = jnp.dot(kvn, wukv_ref[0], preferred_element_type=F32)
    for gp in range(npair):
        proj_ref[2, gp] = kv[:, gp * PAIR:(gp + 1) * PAIR].astype(BF16)
        proj_ref[3, gp] = kv[:, width + gp * PAIR:width + (gp + 1) * PAIR].astype(BF16)


def _mla_in_call(h, g, scale_a, shift_a, w_in, gq, gkv, w_uq, w_ukv, layer, tab, seq, scale):
    n, d = h.shape
    lat_w = w_in.shape[2]
    up_w = w_uq.shape[2]
    npair = up_w // (2 * PAIR)
    tiles_per_b = seq // MLA_TM
    const = lambda shape: pl.BlockSpec(shape, lambda i: (layer,) + (0,) * (len(shape) - 1),
                                       pipeline_mode=pl.Buffered(1))
    return pl.pallas_call(
        functools.partial(_mla_in_kernel, scale=scale),
        grid=(n // MLA_TM,),
        in_specs=[
            pl.BlockSpec((MLA_TM, d), lambda i: (i, 0)),
            pl.BlockSpec((1, d), lambda i: (0, 0)),
            pl.BlockSpec((1, 1, d), lambda i: (i // tiles_per_b, 0, 0)),
            pl.BlockSpec((1, 1, d), lambda i: (i // tiles_per_b, 0, 0)),
            const((1, d, lat_w)),
            pl.BlockSpec((1, MLA_Q_RANK), lambda i: (0, 0)),
            pl.BlockSpec((1, MLA_KV_RANK), lambda i: (0, 0)),
            const((1, MLA_Q_RANK, up_w)),
            const((1, MLA_KV_RANK, up_w)),
            pl.BlockSpec((2, 3, MLA_TM, LANES), lambda i: (0, 0, i, 0)),
        ],
        out_specs=[
            pl.BlockSpec((4, npair, MLA_TM, PAIR), lambda i: (0, 0, i, 0)),
            pl.BlockSpec((MLA_TM, LANES), lambda i: (i, 0)),
        ],
        out_shape=[
            jax.ShapeDtypeStruct((4, npair, n, PAIR), BF16),
            jax.ShapeDtypeStruct((n, LANES), BF16),
        ],
        compiler_params=_cparams(("parallel",)),
        name="mla_in",
    )(h, g, scale_a, shift_a, w_in, gq, gkv, w_uq, w_ukv, tab)


MLA_PAIRS_PER_STEP = 2


def _mla_attn_kernel(q_ref, kv_ref, kr_ref, o_ref, qc_ref, acc_ref):
    qi = pl.program_id(1)
    npair = q_ref.shape[1]
    pps = MLA_PAIRS_PER_STEP
    mask = _chunk_mask()
    m0 = jnp.full((ATT_T, 1), NEG_BIG, F32)
    l0 = jnp.zeros((ATT_T, 1), F32)
    lane_sl = [slice(a * LANES, (a + 1) * LANES) for a in range(2)]

    def group_body(gi, carry):
        acc_ref[...] = jnp.zeros_like(acc_ref)
        for p in range(pps):
            for a in range(2):
                gp = gi * pps + p
                qc_ref[2 * p + a] = jnp.concatenate([q_ref[0, gp, :, lane_sl[a]], q_ref[1, gp, :, lane_sl[a]]],
                                                    axis=-1)

        def step(kt, st, masked):
            k0 = pl.multiple_of(kt * ATT_T, ATT_T)
            kr = kr_ref[pl.ds(k0, ATT_T), :]
            new = []
            for p in range(pps):
                for a in range(2):
                    gp = gi * pps + p
                    e = 2 * p + a
                    kc = jnp.concatenate([kv_ref[0, gp, pl.ds(k0, ATT_T), lane_sl[a]], kr], axis=-1)
                    v = kv_ref[1, gp, pl.ds(k0, ATT_T), lane_sl[a]]
                    s = _qk(qc_ref[e], kc)
                    if masked:
                        s = jnp.where(mask, s, NEG_BIG)
                    m, l = _softmax_step(s, st[2 * e], st[2 * e + 1], acc_ref.at[e], v)
                    new += [m, l]
            return tuple(new)

        st = lax.fori_loop(0, qi, lambda kt, st: step(kt, st, False), (m0, l0) * (2 * pps))
        st = step(qi, st, True)
        for p in range(pps):
            for a in range(2):
                e = 2 * p + a
                o_ref[gi * pps + p, :, lane_sl[a]] = (acc_ref[e] / st[2 * e + 1]).astype(BF16)
        return carry

    lax.fori_loop(0, npair // pps, group_body, 0)


def _mla_attn_call(proj, kr, batch, seq):
    _, npair, n, _ = proj.shape
    nq = seq // ATT_T
    return pl.pallas_call(
        _mla_attn_kernel,
        grid=(batch, nq),
        in_specs=[
            pl.BlockSpec((2, npair, ATT_T, PAIR), lambda b, qi: (0, 0, b * nq + qi, 0)),
            pl.BlockSpec((2, npair, seq, PAIR), lambda b, qi: (1, 0, b, 0), pipeline_mode=pl.Buffered(1)),
            pl.BlockSpec((seq, LANES), lambda b, qi: (b, 0), pipeline_mode=pl.Buffered(1)),
        ],
        out_specs=pl.BlockSpec((npair, ATT_T, PAIR), lambda b, qi: (0, b * nq + qi, 0)),
        out_shape=jax.ShapeDtypeStruct((npair, n, PAIR), BF16),
        scratch_shapes=[pltpu.VMEM((2 * MLA_PAIRS_PER_STEP, ATT_T, PAIR), BF16),
                        pltpu.VMEM((2 * MLA_PAIRS_PER_STEP, ATT_T, LANES), F32)],
        compiler_params=_cparams(("parallel", "arbitrary")),
        name="mla_attn",
    )(proj, proj, kr)


AOUT_TM = 512
AOUT_ROW_BLOCKS = 2


def _attn_out_kernel(o_ref, w_ref, h_ref, gate_ref, g_ref, sc_ref, sh_ref, wr_ref, br_ref,
                     hn_ref, u_ref, idx_ref, gw_ref):
    ns = o_ref.shape[0]
    tm = h_ref.shape[0] // AOUT_ROW_BLOCKS
    for r in range(AOUT_ROW_BLOCKS):
        rs = slice(r * tm, (r + 1) * tm)
        y = jnp.dot(o_ref[0, rs, :], w_ref[0, 0:PAIR, :], preferred_element_type=F32)
        for s in range(1, ns):
            y = y + jnp.dot(o_ref[s, rs, :], w_ref[0, s * PAIR:(s + 1) * PAIR, :], preferred_element_type=F32)
        hn = h_ref[rs, :] + gate_ref[0] * y
        hn_ref[rs, :] = hn
        u = _norm_mod(hn, g_ref[...], sc_ref[0], sh_ref[0])
        u_ref[rs, :] = _pack_halves(u)
        u_hi = u.astype(BF16)
        u_lo = (u - u_hi.astype(F32)).astype(BF16)
        logits = jnp.dot(jnp.concatenate([u_hi, u_lo, u_hi], axis=-1), wr_ref[0],
                         preferred_element_type=F32) + br_ref[0]
        ne = logits.shape[1]
        eio = lax.broadcasted_iota(jnp.int32, (tm, ne), 1)
        lane = lax.broadcasted_iota(jnp.int32, (tm, LANES), 1)
        idx_out = jnp.zeros((tm, LANES), jnp.int32)
        val_out = jnp.zeros((tm, LANES), F32)
        l = logits
        v0 = None
        den = None
        for k in range(TOP_K):
            m = jnp.max(l, axis=-1, keepdims=True)
            i = jnp.min(jnp.where(l == m, eio, ne), axis=-1, keepdims=True)
            l = jnp.where(eio == i, -jnp.inf, l)
            if k == 0:
                v0 = m
            e = jnp.exp(m - v0)
            den = e if den is None else den + e
            idx_out = jnp.where(lane == k, i, idx_out)
            val_out = jnp.where(lane == k, e, val_out)
        idx_ref[rs, :] = idx_out
        gw_ref[rs, :] = val_out / den


def _attn_out_call(o, w_out, layer, h, gate, g, scale, shift, w_router, b_router, seq):
    ns, n, _ = o.shape
    d = h.shape[1]
    ne = w_router.shape[2]
    tiles_per_b = seq // AOUT_TM
    perb = lambda: pl.BlockSpec((1, 1, d), lambda i: (i // tiles_per_b, 0, 0))
    return pl.pallas_call(
        _attn_out_kernel,
        grid=(n // AOUT_TM,),
        in_specs=[
            pl.BlockSpec((ns, AOUT_TM, PAIR), lambda i: (0, i, 0)),
            pl.BlockSpec((1, d, d), lambda i: (layer, 0, 0), pipeline_mode=pl.Buffered(1)),
            pl.BlockSpec((AOUT_TM, d), lambda i: (i, 0)),
            perb(),
            pl.BlockSpec((1, d), lambda i: (0, 0)),
            perb(),
            perb(),
            pl.BlockSpec((1, 3 * d, ne), lambda i: (0, 0, 0)),
            pl.BlockSpec((1, 1, ne), lambda i: (0, 0, 0)),
        ],
        out_specs=[
            pl.BlockSpec((AOUT_TM, d), lambda i: (i, 0)),
            pl.BlockSpec((AOUT_TM, d // 2), lambda i: (i, 0)),
            pl.BlockSpec((AOUT_TM, LANES), lambda i: (i, 0)),
            pl.BlockSpec((AOUT_TM, LANES), lambda i: (i, 0)),
        ],
        out_shape=[
            jax.ShapeDtypeStruct((n, d), F32),
            jax.ShapeDtypeStruct((n, d // 2), jnp.uint32),
            jax.ShapeDtypeStruct((n, LANES), jnp.int32),
            jax.ShapeDtypeStruct((n, LANES), F32),
        ],
        compiler_params=_cparams(("parallel",)),
        name="attn_out",
    )(o, w_out, h, gate, g, scale, shift, w_router, b_router)


def _route_kernel(idx_ref, dest_ref, pt_ref, carry_ref, ps_ref):
    ph = pl.program_id(0)
    t = pl.program_id(1)
    nt = pl.num_programs(1)
    tb = idx_ref.shape[0]
    idx = idx_ref[...]
    eio = lax.broadcasted_iota(jnp.int32, (tb, LANES), 1)
    sel = [eio == idx[:, k:k + 1] for k in range(TOP_K)]
    oh = jnp.where(sel[0], 1.0, 0.0)
    for k in range(1, TOP_K):
        oh = oh + jnp.where(sel[k], 1.0, 0.0)

    @pl.when(t == 0)
    def _():
        carry_ref[...] = jnp.zeros_like(carry_ref)

    @pl.when((ph == 0) & (t == nt - 1))
    def _():
        cnt = carry_ref[...] + jnp.sum(oh, axis=0, keepdims=True)
        tiles = jnp.floor((cnt + (EXPERT_TM - 0.5)) * (1.0 / EXPERT_TM))
        r = lax.broadcasted_iota(jnp.int32, (LANES, LANES), 0)
        c = lax.broadcasted_iota(jnp.int32, (LANES, LANES), 1)
        upper = jnp.where(r < c, 1.0, 0.0).astype(BF16)
        t8 = jnp.broadcast_to(tiles, (SUBLANES, LANES))
        start = jnp.dot(t8.astype(BF16), upper, preferred_element_type=F32)
        ps_ref[...] = start[0:1] * EXPERT_TM
        row = lax.broadcasted_iota(jnp.int32, (SUBLANES, LANES), 0)
        pt_ref[...] = jnp.where(row == 0, t8, jnp.broadcast_to(cnt, (SUBLANES, LANES)))

    @pl.when(ph == 1)
    def _():
        r = lax.broadcasted_iota(jnp.int32, (tb, tb), 0)
        c = lax.broadcasted_iota(jnp.int32, (tb, tb), 1)
        lower = jnp.where(c < r, 1.0, 0.0).astype(BF16)
        before = jnp.dot(lower, oh.astype(BF16), preferred_element_type=F32)
        base = before + carry_ref[...] + ps_ref[...]
        out = jnp.zeros((tb, LANES), jnp.int32)
        for k in range(TOP_K):
            dk = jnp.sum(jnp.where(sel[k], base, 0.0), axis=-1, keepdims=True)
            out = jnp.where(eio == k, dk.astype(jnp.int32), out)
        dest_ref[...] = out

    carry_ref[...] += jnp.sum(oh, axis=0, keepdims=True)


def _route_call(idx):
    n = idx.shape[0]
    return pl.pallas_call(
        _route_kernel,
        grid=(2, n // ROUTE_TB),
        in_specs=[pl.BlockSpec((ROUTE_TB, LANES), lambda p, t: (t, 0))],
        out_specs=[
            pl.BlockSpec((ROUTE_TB, LANES), lambda p, t: (p * t, 0)),
            pl.BlockSpec((SUBLANES, LANES), lambda p, t: (0, 0)),
        ],
        out_shape=[
            jax.ShapeDtypeStruct((n, LANES), jnp.int32),
            jax.ShapeDtypeStruct((SUBLANES, LANES), F32),
        ],
        scratch_shapes=[pltpu.VMEM((1, LANES), F32), pltpu.VMEM((1, LANES), F32)],
        compiler_params=_cparams(("arbitrary", "arbitrary")),
        name="route",
    )(idx)


DISPATCH_ZROWS = 512
DISPATCH_TAIL = 64
ROW_DMA_UNROLL = 4
assert DISPATCH_ZROWS <= EXPERT_TM < 2 * DISPATCH_ZROWS and EXPERT_TM % DISPATCH_TAIL == 0


def _dispatch_kernel(dest_ref, pstart_ref, cnt_ref, pend_ref, u_ref, xb_ref, zbuf_ref, sem, zsem):
    i = pl.program_id(0)
    tt = u_ref.shape[0]
    rows = xb_ref.shape[0]
    ne = pstart_ref.shape[0]

    def zero_copy(pos, nrows):
        return pltpu.make_async_copy(zbuf_ref.at[pl.ds(0, nrows)], xb_ref.at[pl.ds(pos, nrows)], zsem)

    def pad_fill(act):
        def expert_body(e, carry):
            pos = pstart_ref[e] + cnt_ref[e]
            npad = pend_ref[e] - pos
            head = jnp.minimum(npad, (-pos) & (SUBLANES - 1))

            def head_body(r, c):
                act(zero_copy(pos + r, 1))
                return c

            lax.fori_loop(0, head, head_body, 0)
            base = pos + head
            rem = npad - head
            bit = DISPATCH_ZROWS
            while bit >= SUBLANES:
                off = rem - (rem & (2 * bit - 1))

                @pl.when((rem & bit) != 0)
                def _(off=off, bit=bit):
                    act(zero_copy(pl.multiple_of(base + off, SUBLANES), bit))

                bit //= 2
            return carry

        lax.fori_loop(0, ne, expert_body, 0)
        tail0 = pend_ref[ne - 1]

        def tail_body(b, c):
            act(zero_copy(pl.multiple_of(tail0 + b * DISPATCH_TAIL, DISPATCH_TAIL), DISPATCH_TAIL))
            return c

        lax.fori_loop(0, (rows - tail0) // DISPATCH_TAIL, tail_body, 0)

    @pl.when(i == 0)
    def _():
        zbuf_ref[...] = jnp.zeros_like(zbuf_ref)
        pad_fill(lambda c: c.start())

    def row_copy(t, d):
        return pltpu.make_async_copy(u_ref.at[pl.ds(t, 1)], xb_ref.at[pl.ds(d, 1)], sem)

    def issue(t, carry):
        for k in range(TOP_K):
            row_copy(t, dest_ref[(i * tt + t) * TOP_K + k]).start()
        return carry

    lax.fori_loop(0, tt, issue, 0, unroll=ROW_DMA_UNROLL)
    pltpu.make_async_copy(xb_ref.at[pl.ds(0, TOP_K * tt)], xb_ref.at[pl.ds(0, TOP_K * tt)], sem).wait()

    @pl.when(i == 0)
    def _():
        pad_fill(lambda c: c.wait())


def _dispatch_call(dest_flat, pstart, cnt, pend, u, rows):
    n, d = u.shape
    return pl.pallas_call(
        _dispatch_kernel,
        grid_spec=pltpu.PrefetchScalarGridSpec(
            num_scalar_prefetch=4,
            grid=(n // DISPATCH_TT,),
            in_specs=[pl.BlockSpec((DISPATCH_TT, d), lambda i, *_: (i, 0))],
            out_specs=pl.BlockSpec(memory_space=pl.ANY),
            scratch_shapes=[pltpu.VMEM((DISPATCH_ZROWS, d), u.dtype),
                            pltpu.SemaphoreType.DMA(()), pltpu.SemaphoreType.DMA(())],
        ),
        out_shape=jax.ShapeDtypeStruct((rows, d), u.dtype),
        compiler_params=_cparams(("arbitrary",), has_side_effects=True),
        name="dispatch",
    )(dest_flat, pstart, cnt, pend, u)


def _expert_kernel(te_ref, first_ref, nu_ref, x_ref, wg_ref, wu_ref, bg_ref, bu_ref, wd_ref, bd_ref, o_ref,
                   xbf_ref, wgb_ref, wub_ref, wdb_ref, act_ref):
    del te_ref
    i = pl.program_id(0)
    j = pl.program_id(1)
    nf = act_ref.shape[0]
    half = x_ref.shape[1]

    @pl.when(i < nu_ref[0])
    def _():
        @pl.when(j == 0)
        def _():
            lo, hi = _unpack_halves(x_ref[...])
            xbf_ref[:, 0:half] = lo.astype(BF16)
            xbf_ref[:, half:2 * half] = hi.astype(BF16)

        @pl.when(first_ref[i] == 1)
        def _():
            wgb_ref[j] = wg_ref[0, 0].astype(BF16)
            wub_ref[j] = wu_ref[0, 0].astype(BF16)
            wdb_ref[j] = wd_ref[0, 0].astype(BF16)

        x = xbf_ref[...]
        gate = jnp.dot(x, wgb_ref[j], preferred_element_type=F32) + bg_ref[0, 0]
        up = jnp.dot(x, wub_ref[j], preferred_element_type=F32) + bu_ref[0, 0]
        gate = jnp.minimum(gate, SWIGLU_LIMIT)
        up = jnp.clip(up, -SWIGLU_LIMIT, SWIGLU_LIMIT)
        glu = gate * _sigmoid(gate * SWIGLU_ALPHA)
        act_ref[j] = ((up + 1.0) * glu).astype(BF16)

        @pl.when(j == nf - 1)
        def _():
            y = jnp.dot(act_ref[0], wdb_ref[0], preferred_element_type=F32)
            for c in range(1, nf):
                y = y + jnp.dot(act_ref[c], wdb_ref[c], preferred_element_type=F32)
            o_ref[...] = _pack_halves(y + bd_ref[0, 0])

    @pl.when((i >= nu_ref[0]) & (j == 0))
    def _():
        o_ref[...] = jnp.zeros_like(o_ref)


def _expert_call(tile_e, first, n_used, xb, w_gu, b_gu, w_down, b_down, layer):
    rows, dw = xb.shape
    d = 2 * dw
    ne = w_gu.shape[1]
    f = w_down.shape[2]
    nf = f // EXPERT_TF
    n_tiles = rows // EXPERT_TM

    def tile(i, nu):
        return jnp.minimum(i, nu[0] - 1)

    def fcol(i, j, nu):
        return jnp.where(i < nu[0], j, nf - 1)

    def wcol(i, j, fi, nu):
        return jnp.where((i < nu[0]) & (fi[tile(i, nu)] == 1), j, nf - 1)

    return pl.pallas_call(
        _expert_kernel,
        grid_spec=pltpu.PrefetchScalarGridSpec(
            num_scalar_prefetch=3,
            grid=(n_tiles, nf),
            in_specs=[
                pl.BlockSpec((EXPERT_TM, dw), lambda i, j, te, fi, nu: (tile(i, nu), 0)),
                pl.BlockSpec((1, 1, d, EXPERT_TF),
                             lambda i, j, te, fi, nu: (layer, te[tile(i, nu)], 0, wcol(i, j, fi, nu))),
                pl.BlockSpec((1, 1, d, EXPERT_TF),
                             lambda i, j, te, fi, nu: (layer, te[tile(i, nu)], 0, nf + wcol(i, j, fi, nu))),
                pl.BlockSpec((1, 1, 1, EXPERT_TF),
                             lambda i, j, te, fi, nu: (layer, te[tile(i, nu)], 0, fcol(i, j, nu))),
                pl.BlockSpec((1, 1, 1, EXPERT_TF),
                             lambda i, j, te, fi, nu: (layer, te[tile(i, nu)], 0, nf + fcol(i, j, nu))),
                pl.BlockSpec((1, 1, EXPERT_TF, d),
                             lambda i, j, te, fi, nu: (layer, te[tile(i, nu)], wcol(i, j, fi, nu), 0)),
                pl.BlockSpec((1, 1, 1, d), lambda i, j, te, fi, nu: (layer, te[tile(i, nu)], 0, 0)),
            ],
            out_specs=pl.BlockSpec((EXPERT_TM, dw), lambda i, j, te, fi, nu: (i, 0)),
            scratch_shapes=[pltpu.VMEM((EXPERT_TM, d), BF16),
                            pltpu.VMEM((nf, d, EXPERT_TF), BF16),
                            pltpu.VMEM((nf, d, EXPERT_TF), BF16),
                            pltpu.VMEM((nf, EXPERT_TF, d), BF16),
                            pltpu.VMEM((nf, EXPERT_TM, EXPERT_TF), BF16)],
        ),
        out_shape=jax.ShapeDtypeStruct((rows, dw), jnp.uint32),
        compiler_params=_cparams(("arbitrary", "arbitrary")),
        name="experts",
    )(tile_e, first, n_used, xb, w_gu, w_gu, b_gu.reshape(b_gu.shape[0], ne, 1, 2 * f),
      b_gu.reshape(b_gu.shape[0], ne, 1, 2 * f), w_down, b_down.reshape(b_down.shape[0], ne, 1, d))


def _combine_kernel(dest_ref, yb_ref, gw_ref, h_ref, gate_ref, g_ref, o_ref, buf_ref, sem, *, final):
    i = pl.program_id(0)
    nsteps = pl.num_programs(0)
    tt = h_ref.shape[0]
    slot = i % 2

    def issue(tile, sl):
        def body(t, carry):
            for k in range(TOP_K):
                d = dest_ref[(tile * tt + t) * TOP_K + k]
                pltpu.make_async_copy(yb_ref.at[pl.ds(d, 1)], buf_ref.at[sl, pl.ds(k * tt + t, 1)],
                                      sem.at[sl]).start()
            return carry
        lax.fori_loop(0, tt, body, 0, unroll=ROW_DMA_UNROLL)

    @pl.when(i == 0)
    def _():
        issue(0, 0)

    @pl.when(i + 1 < nsteps)
    def _():
        issue(i + 1, 1 - slot)

    pltpu.make_async_copy(yb_ref.at[pl.ds(0, TOP_K * tt)], buf_ref.at[slot], sem.at[slot]).wait()
    gw = gw_ref[...]
    y_lo, y_hi = None, None
    for k in range(TOP_K):
        lo, hi = _unpack_halves(buf_ref[slot, k * tt:(k + 1) * tt])
        wk = gw[:, k:k + 1]
        y_lo = wk * lo if y_lo is None else y_lo + wk * lo
        y_hi = wk * hi if y_hi is None else y_hi + wk * hi
    hn = h_ref[...] + gate_ref[0] * jnp.concatenate([y_lo, y_hi], axis=-1)
    if final:
        hn = _rms(hn, g_ref[...])
    o_ref[...] = hn


def _combine_call(dest_flat, yb, gw, h, gate, g_final, seq, final):
    n, d = h.shape
    tiles_per_b = seq // COMBINE_TT
    return pl.pallas_call(
        functools.partial(_combine_kernel, final=final),
        grid_spec=pltpu.PrefetchScalarGridSpec(
            num_scalar_prefetch=1,
            grid=(n // COMBINE_TT,),
            in_specs=[
                pl.BlockSpec(memory_space=pl.ANY),
                pl.BlockSpec((COMBINE_TT, LANES), lambda i, dest: (i, 0)),
                pl.BlockSpec((COMBINE_TT, d), lambda i, dest: (i, 0)),
                pl.BlockSpec((1, 1, d), lambda i, dest: (i // tiles_per_b, 0, 0)),
                pl.BlockSpec((1, d), lambda i, dest: (0, 0)),
            ],
            out_specs=pl.BlockSpec((COMBINE_TT, d), lambda i, dest: (i, 0)),
            scratch_shapes=[pltpu.VMEM((2, TOP_K * COMBINE_TT, d // 2), jnp.uint32),
                            pltpu.SemaphoreType.DMA((2,))],
        ),
        out_shape=jax.ShapeDtypeStruct((n, d), F32),
        compiler_params=_cparams(("arbitrary",)),
        name="combine",
    )(dest_flat, yb, gw, h, gate, g_final)


def _rope_tables(positions, rot_dim, scale):
    half = rot_dim // 2
    inv_freq = ROPE_THETA ** (-jnp.arange(0, rot_dim, 2, dtype=F32) / rot_dim)
    ang = positions.reshape(-1).astype(F32)[:, None] * inv_freq
    cos, sin = jnp.cos(ang), jnp.sin(ang)
    n = ang.shape[0]
    c = jnp.concatenate([cos, cos, jnp.ones((n, LANES - rot_dim), F32)], axis=1)
    sa = jnp.concatenate([-sin, jnp.zeros((n, LANES - half), F32)], axis=1)
    sb = jnp.concatenate([jnp.zeros((n, half), F32), sin, jnp.zeros((n, LANES - rot_dim), F32)], axis=1)
    t = jnp.stack([c, sa, sb])
    return jnp.stack([t * scale, t])


def _moe(layer, h, u, idx, gw, gate_f, g_final, final, seq, w_gu, b_gu, w_down, b_down):
    n, d = h.shape
    rows = -(-(n * TOP_K + N_EXPERTS * (EXPERT_TM - 1)) // EXPERT_TM) * EXPERT_TM
    n_tiles = rows // EXPERT_TM
    dest, stats = _route_call(idx)
    dest_flat = dest[:, :TOP_K].reshape(-1)
    cnt = stats[1, :N_EXPERTS].astype(jnp.int32)
    tile_end = jnp.cumsum(stats[0, :N_EXPERTS].astype(jnp.int32))
    pend = tile_end * EXPERT_TM
    pstart = jnp.concatenate([jnp.zeros((1,), jnp.int32), pend[:-1]])
    tile_e = jnp.minimum(jnp.searchsorted(tile_end, jnp.arange(n_tiles, dtype=jnp.int32), side='right'),
                         N_EXPERTS - 1).astype(jnp.int32)
    first = jnp.concatenate([jnp.ones((1,), jnp.int32), (tile_e[1:] != tile_e[:-1]).astype(jnp.int32)])
    n_used = tile_end[-1:].astype(jnp.int32)
    xb = _dispatch_call(dest_flat, pstart, cnt, pend, u, rows)
    yb = _expert_call(tile_e, first, n_used, xb, w_gu, b_gu, w_down, b_down, layer)
    return _combine_call(dest_flat, yb, gw, h, gate_f, g_final, seq, final)


def kernel(x, c, positions, ada_w, ada_b, mix_norm_g, ffn_norm_g, final_norm_g, diff_w_in, diff_lambda, diff_subln_g, diff_w_out, mla_w_in, mla_q_norm_g, mla_kv_norm_g, mla_w_uq, mla_w_ukv, mla_w_out, moe_w_router, moe_b_router, moe_w_gate_up, moe_b_gate_up, moe_w_down, moe_b_down):
    batch, seq, d = x.shape
    depth = ada_w.shape[0]
    n = batch * seq
    assert seq % ATT_T == 0 and ATT_T % CHUNK == 0 and d % PAIR == 0
    assert n % DISPATCH_TT == 0 and n % ROUTE_TB == 0 and seq % DIN_TM == 0

    diff_scale = DIFF_HEAD_DIM ** -0.5
    mla_scale = (MLA_NOPE + MLA_ROPE) ** -0.5
    tab_d = _rope_tables(positions, DIFF_HEAD_DIM // 4, diff_scale)
    tab_m = _rope_tables(positions, MLA_ROPE, mla_scale)

    mod = _mod_call(c, ada_w, ada_b)
    mod = mod.reshape(depth, batch, N_MOD, 1, d)

    diff_w_in_bf = diff_w_in.astype(BF16)
    diff_w_out_bf = diff_w_out.astype(BF16)
    mla_w_out_bf = mla_w_out.astype(BF16)
    nm = mla_w_in.shape[0]
    heads_m = mla_w_out.shape[1] // MLA_V
    lat_pad = LANES - MLA_ROPE
    mla_w_in_bf = jnp.pad(mla_w_in, ((0, 0), (0, 0), (0, lat_pad))).astype(BF16)
    wq = mla_w_uq.reshape(nm, MLA_Q_RANK, heads_m, MLA_NOPE + MLA_ROPE)
    wq_rope = jnp.pad(wq[..., MLA_NOPE:], ((0, 0), (0, 0), (0, 0), (0, LANES - MLA_ROPE)))
    mla_w_uq_bf = jnp.concatenate([wq[..., :MLA_NOPE].reshape(nm, MLA_Q_RANK, -1),
                                   wq_rope.reshape(nm, MLA_Q_RANK, -1)], axis=-1).astype(BF16)
    wkv = mla_w_ukv.reshape(nm, MLA_KV_RANK, heads_m, MLA_NOPE + MLA_V)
    mla_w_ukv_bf = jnp.concatenate([wkv[..., :MLA_NOPE].reshape(nm, MLA_KV_RANK, -1),
                                    wkv[..., MLA_NOPE:].reshape(nm, MLA_KV_RANK, -1)], axis=-1).astype(BF16)

    h = x.reshape(n, d)
    for i in range(depth):
        shift_a, scale_a, gate_a, shift_f, scale_f, gate_f = [mod[i, :, m] for m in range(N_MOD)]
        g_mix = mix_norm_g[i].reshape(1, d)
        g_ffn = ffn_norm_g[i].reshape(1, d)
        j = i // N_MIXERS
        if i % N_MIXERS == 0:
            lambda_init = 0.8 - 0.6 * math.exp(-0.3 * i)
            qkv = _diff_in_call(h, g_mix, scale_a, shift_a, diff_w_in_bf, j, tab_d, seq)
            o = _diff_attn_call(qkv, diff_lambda[j], diff_subln_g[j].reshape(1, -1), lambda_init, batch, seq)
            w_out = diff_w_out_bf
        else:
            proj, kr = _mla_in_call(h, g_mix, scale_a, shift_a, mla_w_in_bf, mla_q_norm_g[j].reshape(1, -1),
                                    mla_kv_norm_g[j].reshape(1, -1), mla_w_uq_bf, mla_w_ukv_bf, j, tab_m, seq,
                                    mla_scale)
            o = _mla_attn_call(proj, kr, batch, seq)
            w_out = mla_w_out_bf
        wr_hi = moe_w_router[i:i + 1].astype(BF16)
        wr_lo = (moe_w_router[i:i + 1] - wr_hi.astype(F32)).astype(BF16)
        wr3 = jnp.concatenate([wr_hi, wr_hi, wr_lo], axis=1)
        h, u, idx, gw = _attn_out_call(o, w_out, j, h, gate_a, g_ffn, scale_f, shift_f,
                                       wr3, moe_b_router[i:i + 1].reshape(1, 1, -1), seq)
        h = _moe(i, h, u, idx, gw, gate_f, final_norm_g.reshape(1, d), i == depth - 1, seq,
                 moe_w_gate_up, moe_b_gate_up, moe_w_down, moe_b_down)
    return h.reshape(batch, seq, d)
```

```python
import functools
import math

import jax
import jax.numpy as jnp
from jax import lax
from jax.experimental import pallas as pl
from jax.experimental.pallas import tpu as pltpu

F32 = jnp.float32
BF16 = jnp.bfloat16

CHUNK = 64
N_MIXERS = 2
ROPE_THETA = 500000.0
RMS_EPS = 1e-6
N_MOD = 6
DIFF_HEAD_DIM = 128
MLA_V = 128
MLA_Q_RANK = 512
MLA_KV_RANK = 512
MLA_NOPE = 128
MLA_ROPE = 64
N_EXPERTS = 32
TOP_K = 4
SWIGLU_LIMIT = 7.0
SWIGLU_ALPHA = 1.702

LANES = 128
SUBLANES = 8
V7X_VMEM_BYTES = 64 * 1024 * 1024
VMEM_LIMIT = 56 * 1024 * 1024

NEG_BIG = -1e30

ATT_T = 256
PAIR = 256
EXPERT_TM = 576
EXPERT_TF = 256
ROUTE_TB = 512
DISPATCH_TT = 512
COMBINE_TT = 256


def _cparams(sem, **kw):
    return pltpu.CompilerParams(dimension_semantics=sem, vmem_limit_bytes=VMEM_LIMIT, **kw)


def _sigmoid(x):
    return 1.0 / (1.0 + jnp.exp(-x))


def _rms(x, g):
    ms = jnp.mean(x * x, axis=-1, keepdims=True)
    return x * lax.rsqrt(ms + RMS_EPS) * g


def _norm_mod(x, g, scale, shift):
    return _rms(x, g) * (1.0 + scale) + shift


def _pack_halves(x):
    w = x.shape[-1] // 2
    lo = lax.bitcast_convert_type(x[:, :w].astype(BF16).astype(F32), jnp.uint32)
    hi = lax.bitcast_convert_type(x[:, w:].astype(BF16).astype(F32), jnp.uint32)
    return (lo >> 16) | (hi & jnp.uint32(0xFFFF0000))


def _unpack_halves(words):
    lo = lax.bitcast_convert_type(words << 16, F32)
    hi = lax.bitcast_convert_type(words & jnp.uint32(0xFFFF0000), F32)
    return lo, hi


def _rope128(x, c, sa, sb, half):
    return x * c + pltpu.roll(x, LANES - half, 1) * sa + pltpu.roll(x, half, 1) * sb


MOD_TN = 512
MOD_KC = 64


def _mod_kernel(cb_ref, w_ref, b_ref, o_ref, cs_ref):
    nb = cb_ref.shape[0]
    d = cb_ref.shape[1]
    nj = MOD_TN // LANES

    @pl.when((pl.program_id(0) == 0) & (pl.program_id(1) == 0))
    def _():
        c = cb_ref[...]
        cs_ref[...] = c * _sigmoid(c)

    def body(kc, accs):
        k0 = pl.multiple_of(kc * MOD_KC, MOD_KC)
        w = w_ref[0, pl.ds(k0, MOD_KC), :]
        out = []
        for b in range(nb):
            cb = cs_ref[b, pl.ds(k0, MOD_KC), :].reshape(MOD_KC // SUBLANES, SUBLANES, LANES)
            row = []
            for jj in range(nj):
                w3 = w[:, jj * LANES:(jj + 1) * LANES].reshape(MOD_KC // SUBLANES, SUBLANES, LANES)
                row.append(accs[b][jj] + jnp.sum(w3 * cb, axis=0))
            out.append(tuple(row))
        return tuple(out)

    zero = jnp.zeros((SUBLANES, LANES), F32)
    init = tuple(tuple(zero for _ in range(nj)) for _ in range(nb))
    accs = lax.fori_loop(0, d // MOD_KC, body, init)
    for b in range(nb):
        for jj in range(nj):
            r = jnp.sum(accs[b][jj], axis=0, keepdims=True)
            o_ref[0, b:b + 1, jj * LANES:(jj + 1) * LANES] = r + b_ref[0, :, jj * LANES:(jj + 1) * LANES]


def _mod_call(c, ada_w, ada_b):
    depth, d, n6 = ada_w.shape
    nb = c.shape[0]
    cb = jnp.broadcast_to(c[:, :, None], (nb, d, LANES))
    return pl.pallas_call(
        _mod_kernel,
        grid=(depth, n6 // MOD_TN),
        in_specs=[
            pl.BlockSpec((nb, d, LANES), lambda l, j: (0, 0, 0)),
            pl.BlockSpec((1, d, MOD_TN), lambda l, j: (l, 0, j)),
            pl.BlockSpec((1, 1, MOD_TN), lambda l, j: (l, 0, j)),
        ],
        out_specs=pl.BlockSpec((1, nb, MOD_TN), lambda l, j: (l, 0, j)),
        out_shape=jax.ShapeDtypeStruct((depth, nb, n6), F32),
        scratch_shapes=[pltpu.VMEM((nb, d, LANES), F32)],
        compiler_params=_cparams(("arbitrary", "arbitrary")),
        name="adaln_mod",
    )(cb, ada_w, ada_b.reshape(depth, 1, n6))


DIN_TM = 512
DIN_TN = 1024


def _diff_in_kernel(h_ref, g_ref, sc_ref, sh_ref, w_ref, tab_ref, o_ref, u_ref, *, nqb):
    j = pl.program_id(1)
    nh = DIN_TN // PAIR

    @pl.when(j == 0)
    def _():
        u_ref[...] = _norm_mod(h_ref[...], g_ref[...], sc_ref[0], sh_ref[0]).astype(BF16)

    def slab(hh):
        return jnp.dot(u_ref[...], w_ref[0, :, hh * PAIR:(hh + 1) * PAIR], preferred_element_type=F32)

    @pl.when(j < 2 * nqb)
    def _():
        c = tab_ref[0, 0]
        sa = tab_ref[0, 1]
        sb = tab_ref[0, 2]
        for hh in range(nh):
            acc = slab(hh)
            for half in range(2):
                y = _rope128(acc[:, half * LANES:(half + 1) * LANES], c, sa, sb, DIFF_HEAD_DIM // 8)
                o_ref[0, hh, :, half * LANES:(half + 1) * LANES] = y.astype(BF16)

    @pl.when(j >= 2 * nqb)
    def _():
        for hh in range(nh):
            o_ref[0, hh] = slab(hh).astype(BF16)


def _diff_in_call(h, g, scale, shift, w_bf, layer, tab, seq):
    n, d = h.shape
    nout = w_bf.shape[2]
    third = nout // 3
    nqb = third // DIN_TN
    heads = third // PAIR
    hpb = DIN_TN // PAIR
    tiles_per_b = seq // DIN_TM
    return pl.pallas_call(
        functools.partial(_diff_in_kernel, nqb=nqb),
        grid=(n // DIN_TM, nout // DIN_TN),
        in_specs=[
            pl.BlockSpec((DIN_TM, d), lambda i, j: (i, 0)),
            pl.BlockSpec((1, d), lambda i, j: (0, 0)),
            pl.BlockSpec((1, 1, d), lambda i, j: (i // tiles_per_b, 0, 0)),
            pl.BlockSpec((1, 1, d), lambda i, j: (i // tiles_per_b, 0, 0)),
            pl.BlockSpec((1, d, DIN_TN), lambda i, j: (layer, 0, j)),
            pl.BlockSpec((1, 3, DIN_TM, LANES), lambda i, j: (jnp.minimum(j // nqb, 1), 0, i, 0)),
        ],
        out_specs=pl.BlockSpec((1, hpb, DIN_TM, PAIR), lambda i, j: (j // nqb, j % nqb, i, 0)),
        out_shape=jax.ShapeDtypeStruct((3, heads, n, PAIR), BF16),
        scratch_shapes=[pltpu.VMEM((DIN_TM, d), BF16)],
        compiler_params=_cparams(("parallel", "arbitrary")),
        name="diff_in",
    )(h, g, scale, shift, w_bf, tab)


def _chunk_mask():
    r = lax.broadcasted_iota(jnp.int32, (ATT_T, ATT_T), 0) // CHUNK
    c = lax.broadcasted_iota(jnp.int32, (ATT_T, ATT_T), 1) // CHUNK
    return c <= r


def _qk(q, k):
    return lax.dot_general(q, k, (((1,), (1,)), ((), ())), preferred_element_type=F32)


def _fold_lanes(x):
    return [x[:, c * LANES:(c + 1) * LANES] for c in range(ATT_T // LANES)]


def _two_pass_softmax(qi, n_chain, score_fn, value_fn, s_ref, acc_ref):
    mask = _chunk_mask()

    def pass1(kt, mxs, masked):
        new = []
        for e in range(n_chain):
            s = score_fn(e, kt)
            if masked:
                s = jnp.where(mask, s, NEG_BIG)
            s_ref[e, kt] = s
            mx = mxs[e]
            for part in _fold_lanes(s):
                mx = jnp.maximum(mx, part)
            new.append(mx)
        return tuple(new)

    mx0 = jnp.full((ATT_T, LANES), NEG_BIG, F32)
    mxs = lax.fori_loop(0, qi, lambda kt, c: pass1(kt, c, False), (mx0,) * n_chain)
    mxs = pass1(qi, mxs, True)
    ms = [jnp.max(mx, axis=-1, keepdims=True) for mx in mxs]
    acc_ref[...] = jnp.zeros_like(acc_ref)

    def pass2(kt, sums):
        new = []
        for e in range(n_chain):
            p = jnp.exp(s_ref[e, kt] - ms[e])
            sm = sums[e]
            for part in _fold_lanes(p):
                sm = sm + part
            new.append(sm)
            acc_ref[e] += jnp.dot(p.astype(BF16), value_fn(e, kt), preferred_element_type=F32)
        return tuple(new)

    sums = lax.fori_loop(0, qi + 1, pass2, (jnp.zeros((ATT_T, LANES), F32),) * n_chain)
    return [jnp.sum(sm, axis=-1, keepdims=True) for sm in sums]


DIFF_HEADS_PER_STEP = 4


def _diff_attn_kernel(q_ref, k_ref, v_ref, lam_ref, g_ref, o_ref, s_ref, acc_ref, *, lambda_init):
    qi = pl.program_id(1)
    heads = q_ref.shape[1]
    hps = DIFF_HEADS_PER_STEP
    lam = lam_ref[...]
    lam_full = (jnp.exp(jnp.sum(lam[0:1] * lam[1:2], axis=-1, keepdims=True))
                - jnp.exp(jnp.sum(lam[2:3] * lam[3:4], axis=-1, keepdims=True)) + lambda_init)

    def group_body(gi, carry):
        def score(e, kt):
            h = gi * hps + e // 2
            sl = slice((e % 2) * LANES, (e % 2 + 1) * LANES)
            k0 = pl.multiple_of(kt * ATT_T, ATT_T)
            return _qk(q_ref[0, h, :, sl], k_ref[0, h, pl.ds(k0, ATT_T), sl])

        def value(e, kt):
            k0 = pl.multiple_of(kt * ATT_T, ATT_T)
            return v_ref[0, gi * hps + e // 2, pl.ds(k0, ATT_T), :]

        ls = _two_pass_softmax(qi, 2 * hps, score, value, s_ref, acc_ref)
        for n in range(hps):
            o = acc_ref[2 * n] / ls[2 * n] - lam_full * (acc_ref[2 * n + 1] / ls[2 * n + 1])
            o = _rms(o, g_ref[...]) * (1.0 - lambda_init)
            o_ref[gi * hps + n] = o.astype(BF16)
        return carry

    lax.fori_loop(0, heads // hps, group_body, 0)


def _diff_attn_call(qkv, lam, g, lambda_init, batch, seq):
    _, heads, n, _ = qkv.shape
    nq = seq // ATT_T
    kv_spec = lambda which: pl.BlockSpec((1, heads, seq, PAIR), lambda b, qi: (which, 0, b, 0),
                                         pipeline_mode=pl.Buffered(1))
    return pl.pallas_call(
        functools.partial(_diff_attn_kernel, lambda_init=lambda_init),
        grid=(batch, nq),
        in_specs=[
            pl.BlockSpec((1, heads, ATT_T, PAIR), lambda b, qi: (0, 0, b * nq + qi, 0)),
            kv_spec(1),
            kv_spec(2),
            pl.BlockSpec((4, DIFF_HEAD_DIM), lambda b, qi: (0, 0)),
            pl.BlockSpec((1, PAIR), lambda b, qi: (0, 0)),
        ],
        out_specs=pl.BlockSpec((heads, ATT_T, PAIR), lambda b, qi: (0, b * nq + qi, 0)),
        out_shape=jax.ShapeDtypeStruct((heads, n, PAIR), BF16),
        scratch_shapes=[pltpu.VMEM((2 * DIFF_HEADS_PER_STEP, nq, ATT_T, ATT_T), F32),
                        pltpu.VMEM((2 * DIFF_HEADS_PER_STEP, ATT_T, PAIR), F32)],
        compiler_params=_cparams(("parallel", "arbitrary")),
        name="diff_attn",
    )(qkv, qkv, qkv, lam, g)


MLA_TM = 256


def _mla_in_kernel(h_ref, g_ref, sc_ref, sh_ref, win_ref, gq_ref, gkv_ref, wuq_ref, wukv_ref, tab_ref,
                   proj_ref, kr_ref, *, scale):
    u = _norm_mod(h_ref[...], g_ref[...], sc_ref[0], sh_ref[0]).astype(BF16)
    lat = jnp.dot(u, win_ref[0], preferred_element_type=F32)
    qn = _rms(lat[:, 0:MLA_Q_RANK], gq_ref[...]).astype(BF16)
    kvn = _rms(lat[:, MLA_Q_RANK:MLA_Q_RANK + MLA_KV_RANK], gkv_ref[...]).astype(BF16)
    half = MLA_ROPE // 2
    kr = lat[:, MLA_Q_RANK + MLA_KV_RANK:MLA_Q_RANK + MLA_KV_RANK + LANES]
    kr_ref[...] = _rope128(kr, tab_ref[1, 0], tab_ref[1, 1], tab_ref[1, 2], half).astype(BF16)

    npair = proj_ref.shape[1]
    width = npair * PAIR
    q = jnp.dot(qn, wuq_ref[0], preferred_element_type=F32)
    c = tab_ref[0, 0]
    sa = tab_ref[0, 1]
    sb = tab_ref[0, 2]
    for gp in range(npair):
        proj_ref[0, gp] = (q[:, gp * PAIR:(gp + 1) * PAIR] * scale).astype(BF16)
        for a in range(2):
            lo = width + gp * PAIR + a * LANES
            y = _rope128(q[:, lo:lo + LANES], c, sa, sb, half)
            proj_ref[1, gp, :, a * LANES:(a + 1) * LANES] = y.astype(BF16)
    kv = jnp.dot(kvn, wukv_ref[0], preferred_element_type=F32)
    for gp in range(npair):
        proj_ref[2, gp] = kv[:, gp * PAIR:(gp + 1) * PAIR].astype(BF16)
        proj_ref[3, gp] = kv[:, width + gp * PAIR:width + (gp + 1) * PAIR].astype(BF16)


def _mla_in_call(h, g, scale_a, shift_a, w_in, gq, gkv, w_uq, w_ukv, layer, tab, seq, scale):
    n, d = h.shape
    lat_w = w_in.shape[2]
    up_w = w_uq.shape[2]
    npair = up_w // (2 * PAIR)
    tiles_per_b = seq // MLA_TM
    const = lambda shape: pl.BlockSpec(shape, lambda i: (layer,) + (0,) * (len(shape) - 1),
                                       pipeline_mode=pl.Buffered(1))
    return pl.pallas_call(
        functools.partial(_mla_in_kernel, scale=scale),
        grid=(n // MLA_TM,),
        in_specs=[
            pl.BlockSpec((MLA_TM, d), lambda i: (i, 0)),
            pl.BlockSpec((1, d), lambda i: (0, 0)),
            pl.BlockSpec((1, 1, d), lambda i: (i // tiles_per_b, 0, 0)),
            pl.BlockSpec((1, 1, d), lambda i: (i // tiles_per_b, 0, 0)),
            const((1, d, lat_w)),
            pl.BlockSpec((1, MLA_Q_RANK), lambda i: (0, 0)),
            pl.BlockSpec((1, MLA_KV_RANK), lambda i: (0, 0)),
            const((1, MLA_Q_RANK, up_w)),
            const((1, MLA_KV_RANK, up_w)),
            pl.BlockSpec((2, 3, MLA_TM, LANES), lambda i: (0, 0, i, 0)),
        ],
        out_specs=[
            pl.BlockSpec((4, npair, MLA_TM, PAIR), lambda i: (0, 0, i, 0)),
            pl.BlockSpec((MLA_TM, LANES), lambda i: (i, 0)),
        ],
        out_shape=[
            jax.ShapeDtypeStruct((4, npair, n, PAIR), BF16),
            jax.ShapeDtypeStruct((n, LANES), BF16),
        ],
        compiler_params=_cparams(("parallel",)),
        name="mla_in",
    )(h, g, scale_a, shift_a, w_in, gq, gkv, w_uq, w_ukv, tab)


MLA_PAIRS_PER_STEP = 4


def _mla_attn_kernel(q_ref, kv_ref, kr_ref, o_ref, qc_ref, s_ref, acc_ref):
    qi = pl.program_id(1)
    npair = q_ref.shape[1]
    pps = MLA_PAIRS_PER_STEP
    lane_sl = [slice(a * LANES, (a + 1) * LANES) for a in range(2)]

    def group_body(gi, carry):
        for e in range(2 * pps):
            gp = gi * pps + e // 2
            sl = lane_sl[e % 2]
            qc_ref[e] = jnp.concatenate([q_ref[0, gp, :, sl], q_ref[1, gp, :, sl]], axis=-1)

        def score(e, kt):
            k0 = pl.multiple_of(kt * ATT_T, ATT_T)
            kc = jnp.concatenate([kv_ref[0, gi * pps + e // 2, pl.ds(k0, ATT_T), lane_sl[e % 2]],
                                  kr_ref[pl.ds(k0, ATT_T), :]], axis=-1)
            return _qk(qc_ref[e], kc)

        def value(e, kt):
            k0 = pl.multiple_of(kt * ATT_T, ATT_T)
            return kv_ref[1, gi * pps + e // 2, pl.ds(k0, ATT_T), lane_sl[e % 2]]

        ls = _two_pass_softmax(qi, 2 * pps, score, value, s_ref, acc_ref)
        for e in range(2 * pps):
            o_ref[gi * pps + e // 2, :, lane_sl[e % 2]] = (acc_ref[e] / ls[e]).astype(BF16)
        return carry

    lax.fori_loop(0, npair // pps, group_body, 0)


def _mla_attn_call(proj, kr, batch, seq):
    _, npair, n, _ = proj.shape
    nq = seq // ATT_T
    return pl.pallas_call(
        _mla_attn_kernel,
        grid=(batch, nq),
        in_specs=[
            pl.BlockSpec((2, npair, ATT_T, PAIR), lambda b, qi: (0, 0, b * nq + qi, 0)),
            pl.BlockSpec((2, npair, seq, PAIR), lambda b, qi: (1, 0, b, 0), pipeline_mode=pl.Buffered(1)),
            pl.BlockSpec((seq, LANES), lambda b, qi: (b, 0), pipeline_mode=pl.Buffered(1)),
        ],
        out_specs=pl.BlockSpec((npair, ATT_T, PAIR), lambda b, qi: (0, b * nq + qi, 0)),
        out_shape=jax.ShapeDtypeStruct((npair, n, PAIR), BF16),
        scratch_shapes=[pltpu.VMEM((2 * MLA_PAIRS_PER_STEP, ATT_T, PAIR), BF16),
                        pltpu.VMEM((2 * MLA_PAIRS_PER_STEP, nq, ATT_T, ATT_T), F32),
                        pltpu.VMEM((2 * MLA_PAIRS_PER_STEP, ATT_T, LANES), F32)],
        compiler_params=_cparams(("parallel", "arbitrary")),
        name="mla_attn",
    )(proj, proj, kr)


AOUT_TM = 512
AOUT_ROW_BLOCKS = 2


def _attn_out_kernel(o_ref, w_ref, h_ref, gate_ref, g_ref, sc_ref, sh_ref, wr_ref, br_ref,
                     hn_ref, u_ref, idx_ref, gw_ref):
    ns = o_ref.shape[0]
    tm = h_ref.shape[0] // AOUT_ROW_BLOCKS
    for r in range(AOUT_ROW_BLOCKS):
        rs = slice(r * tm, (r + 1) * tm)
        y = jnp.dot(o_ref[0, rs, :], w_ref[0, 0:PAIR, :], preferred_element_type=F32)
        for s in range(1, ns):
            y = y + jnp.dot(o_ref[s, rs, :], w_ref[0, s * PAIR:(s + 1) * PAIR, :], preferred_element_type=F32)
        hn = h_ref[rs, :] + gate_ref[0] * y
        hn_ref[rs, :] = hn
        u = _norm_mod(hn, g_ref[...], sc_ref[0], sh_ref[0])
        u_ref[rs, :] = _pack_halves(u)
        u_hi = u.astype(BF16)
        u_lo = (u - u_hi.astype(F32)).astype(BF16)
        logits = jnp.dot(jnp.concatenate([u_hi, u_lo, u_hi], axis=-1), wr_ref[0],
                         preferred_element_type=F32) + br_ref[0]
        ne = logits.shape[1]
        eio = lax.broadcasted_iota(jnp.int32, (tm, ne), 1)
        lane = lax.broadcasted_iota(jnp.int32, (tm, LANES), 1)
        idx_out = jnp.zeros((tm, LANES), jnp.int32)
        val_out = jnp.zeros((tm, LANES), F32)
        l = logits
        v0 = None
        den = None
        for k in range(TOP_K):
            m = jnp.max(l, axis=-1, keepdims=True)
            i = jnp.min(jnp.where(l == m, eio, ne), axis=-1, keepdims=True)
            l = jnp.where(eio == i, -jnp.inf, l)
            if k == 0:
                v0 = m
            e = jnp.exp(m - v0)
            den = e if den is None else den + e
            idx_out = jnp.where(lane == k, i, idx_out)
            val_out = jnp.where(lane == k, e, val_out)
        idx_ref[rs, :] = idx_out
        gw_ref[rs, :] = val_out / den


def _attn_out_call(o, w_out, layer, h, gate, g, scale, shift, w_router, b_router, seq):
    ns, n, _ = o.shape
    d = h.shape[1]
    ne = w_router.shape[2]
    tiles_per_b = seq // AOUT_TM
    perb = lambda: pl.BlockSpec((1, 1, d), lambda i: (i // tiles_per_b, 0, 0))
    return pl.pallas_call(
        _attn_out_kernel,
        grid=(n // AOUT_TM,),
        in_specs=[
            pl.BlockSpec((ns, AOUT_TM, PAIR), lambda i: (0, i, 0)),
            pl.BlockSpec((1, d, d), lambda i: (layer, 0, 0), pipeline_mode=pl.Buffered(1)),
            pl.BlockSpec((AOUT_TM, d), lambda i: (i, 0)),
            perb(),
            pl.BlockSpec((1, d), lambda i: (0, 0)),
            perb(),
            perb(),
            pl.BlockSpec((1, 3 * d, ne), lambda i: (0, 0, 0)),
            pl.BlockSpec((1, 1, ne), lambda i: (0, 0, 0)),
        ],
        out_specs=[
            pl.BlockSpec((AOUT_TM, d), lambda i: (i, 0)),
            pl.BlockSpec((AOUT_TM, d // 2), lambda i: (i, 0)),
            pl.BlockSpec((AOUT_TM, LANES), lambda i: (i, 0)),
            pl.BlockSpec((AOUT_TM, LANES), lambda i: (i, 0)),
        ],
        out_shape=[
            jax.ShapeDtypeStruct((n, d), F32),
            jax.ShapeDtypeStruct((n, d // 2), jnp.uint32),
            jax.ShapeDtypeStruct((n, LANES), jnp.int32),
            jax.ShapeDtypeStruct((n, LANES), F32),
        ],
        compiler_params=_cparams(("parallel",)),
        name="attn_out",
    )(o, w_out, h, gate, g, scale, shift, w_router, b_router)


def _route_kernel(idx_ref, dest_ref, pt_ref, carry_ref, ps_ref):
    ph = pl.program_id(0)
    t = pl.program_id(1)
    nt = pl.num_programs(1)
    tb = idx_ref.shape[0]
    idx = idx_ref[...]
    eio = lax.broadcasted_iota(jnp.int32, (tb, LANES), 1)
    sel = [eio == idx[:, k:k + 1] for k in range(TOP_K)]
    oh = jnp.where(sel[0], 1.0, 0.0)
    for k in range(1, TOP_K):
        oh = oh + jnp.where(sel[k], 1.0, 0.0)

    @pl.when(t == 0)
    def _():
        carry_ref[...] = jnp.zeros_like(carry_ref)

    @pl.when((ph == 0) & (t == nt - 1))
    def _():
        cnt = carry_ref[...] + jnp.sum(oh, axis=0, keepdims=True)
        tiles = jnp.floor((cnt + (EXPERT_TM - 0.5)) * (1.0 / EXPERT_TM))
        r = lax.broadcasted_iota(jnp.int32, (LANES, LANES), 0)
        c = lax.broadcasted_iota(jnp.int32, (LANES, LANES), 1)
        upper = jnp.where(r < c, 1.0, 0.0).astype(BF16)
        t8 = jnp.broadcast_to(tiles, (SUBLANES, LANES))
        start = jnp.dot(t8.astype(BF16), upper, preferred_element_type=F32)
        ps_ref[...] = start[0:1] * EXPERT_TM
        row = lax.broadcasted_iota(jnp.int32, (SUBLANES, LANES), 0)
        pt_ref[...] = jnp.where(row == 0, t8, jnp.broadcast_to(cnt, (SUBLANES, LANES)))

    @pl.when(ph == 1)
    def _():
        r = lax.broadcasted_iota(jnp.int32, (tb, tb), 0)
        c = lax.broadcasted_iota(jnp.int32, (tb, tb), 1)
        lower = jnp.where(c < r, 1.0, 0.0).astype(BF16)
        before = jnp.dot(lower, oh.astype(BF16), preferred_element_type=F32)
        base = before + carry_ref[...] + ps_ref[...]
        out = jnp.zeros((tb, LANES), jnp.int32)
        for k in range(TOP_K):
            dk = jnp.sum(jnp.where(sel[k], base, 0.0), axis=-1, keepdims=True)
            out = jnp.where(eio == k, dk.astype(jnp.int32), out)
        dest_ref[...] = out

    carry_ref[...] += jnp.sum(oh, axis=0, keepdims=True)


def _route_call(idx):
    n = idx.shape[0]
    return pl.pallas_call(
        _route_kernel,
        grid=(2, n // ROUTE_TB),
        in_specs=[pl.BlockSpec((ROUTE_TB, LANES), lambda p, t: (t, 0))],
        out_specs=[
            pl.BlockSpec((ROUTE_TB, LANES), lambda p, t: (p * t, 0)),
            pl.BlockSpec((SUBLANES, LANES), lambda p, t: (0, 0)),
        ],
        out_shape=[
            jax.ShapeDtypeStruct((n, LANES), jnp.int32),
            jax.ShapeDtypeStruct((SUBLANES, LANES), F32),
        ],
        scratch_shapes=[pltpu.VMEM((1, LANES), F32), pltpu.VMEM((1, LANES), F32)],
        compiler_params=_cparams(("arbitrary", "arbitrary")),
        name="route",
    )(idx)


DISPATCH_ZROWS = 512
DISPATCH_TAIL = 64
ROW_DMA_UNROLL = 4
assert DISPATCH_ZROWS <= EXPERT_TM < 2 * DISPATCH_ZROWS and EXPERT_TM % DISPATCH_TAIL == 0


def _dispatch_kernel(dest_ref, pstart_ref, cnt_ref, pend_ref, u_ref, xb_ref, zbuf_ref, sem, zsem):
    i = pl.program_id(0)
    tt = u_ref.shape[0]
    rows = xb_ref.shape[0]
    ne = pstart_ref.shape[0]

    def zero_copy(pos, nrows):
        return pltpu.make_async_copy(zbuf_ref.at[pl.ds(0, nrows)], xb_ref.at[pl.ds(pos, nrows)], zsem)

    def pad_fill(act):
        def expert_body(e, carry):
            pos = pstart_ref[e] + cnt_ref[e]
            npad = pend_ref[e] - pos
            head = jnp.minimum(npad, (-pos) & (SUBLANES - 1))

            def head_body(r, c):
                act(zero_copy(pos + r, 1))
                return c

            lax.fori_loop(0, head, head_body, 0)
            base = pos + head
            rem = npad - head
            bit = DISPATCH_ZROWS
            while bit >= SUBLANES:
                off = rem - (rem & (2 * bit - 1))

                @pl.when((rem & bit) != 0)
                def _(off=off, bit=bit):
                    act(zero_copy(pl.multiple_of(base + off, SUBLANES), bit))

                bit //= 2
            return carry

        lax.fori_loop(0, ne, expert_body, 0)
        tail0 = pend_ref[ne - 1]

        def tail_body(b, c):
            act(zero_copy(pl.multiple_of(tail0 + b * DISPATCH_TAIL, DISPATCH_TAIL), DISPATCH_TAIL))
            return c

        lax.fori_loop(0, (rows - tail0) // DISPATCH_TAIL, tail_body, 0)

    @pl.when(i == 0)
    def _():
        zbuf_ref[...] = jnp.zeros_like(zbuf_ref)
        pad_fill(lambda c: c.start())

    def row_copy(t, d):
        return pltpu.make_async_copy(u_ref.at[pl.ds(t, 1)], xb_ref.at[pl.ds(d, 1)], sem)

    def issue(t, carry):
        for k in range(TOP_K):
            row_copy(t, dest_ref[(i * tt + t) * TOP_K + k]).start()
        return carry

    lax.fori_loop(0, tt, issue, 0, unroll=ROW_DMA_UNROLL)
    pltpu.make_async_copy(xb_ref.at[pl.ds(0, TOP_K * tt)], xb_ref.at[pl.ds(0, TOP_K * tt)], sem).wait()

    @pl.when(i == 0)
    def _():
        pad_fill(lambda c: c.wait())


def _dispatch_call(dest_flat, pstart, cnt, pend, u, rows):
    n, d = u.shape
    return pl.pallas_call(
        _dispatch_kernel,
        grid_spec=pltpu.PrefetchScalarGridSpec(
            num_scalar_prefetch=4,
            grid=(n // DISPATCH_TT,),
            in_specs=[pl.BlockSpec((DISPATCH_TT, d), lambda i, *_: (i, 0))],
            out_specs=pl.BlockSpec(memory_space=pl.ANY),
            scratch_shapes=[pltpu.VMEM((DISPATCH_ZROWS, d), u.dtype),
                            pltpu.SemaphoreType.DMA(()), pltpu.SemaphoreType.DMA(())],
        ),
        out_shape=jax.ShapeDtypeStruct((rows, d), u.dtype),
        compiler_params=_cparams(("arbitrary",), has_side_effects=True),
        name="dispatch",
    )(dest_flat, pstart, cnt, pend, u)


def _expert_kernel(te_ref, first_ref, nu_ref, x_ref, wg_ref, wu_ref, bg_ref, bu_ref, wd_ref, bd_ref, o_ref,
                   xbf_ref, wgb_ref, wub_ref, wdb_ref, act_ref):
    del te_ref
    i = pl.program_id(0)
    j = pl.program_id(1)
    nf = act_ref.shape[0]
    half = x_ref.shape[1]

    @pl.when(i < nu_ref[0])
    def _():
        @pl.when(j == 0)
        def _():
            lo, hi = _unpack_halves(x_ref[...])
            xbf_ref[:, 0:half] = lo.astype(BF16)
            xbf_ref[:, half:2 * half] = hi.astype(BF16)

        @pl.when(first_ref[i] == 1)
        def _():
            wgb_ref[j] = wg_ref[0, 0].astype(BF16)
            wub_ref[j] = wu_ref[0, 0].astype(BF16)
            wdb_ref[j] = wd_ref[0, 0].astype(BF16)

        x = xbf_ref[...]
        gate = jnp.dot(x, wgb_ref[j], preferred_element_type=F32) + bg_ref[0, 0]
        up = jnp.dot(x, wub_ref[j], preferred_element_type=F32) + bu_ref[0, 0]
        gate = jnp.minimum(gate, SWIGLU_LIMIT)
        up = jnp.clip(up, -SWIGLU_LIMIT, SWIGLU_LIMIT)
        glu = gate * _sigmoid(gate * SWIGLU_ALPHA)
        act_ref[j] = ((up + 1.0) * glu).astype(BF16)

        @pl.when(j == nf - 1)
        def _():
            y = jnp.dot(act_ref[0], wdb_ref[0], preferred_element_type=F32)
            for c in range(1, nf):
                y = y + jnp.dot(act_ref[c], wdb_ref[c], preferred_element_type=F32)
            o_ref[...] = _pack_halves(y + bd_ref[0, 0])

    @pl.when((i >= nu_ref[0]) & (j == 0))
    def _():
        o_ref[...] = jnp.zeros_like(o_ref)


def _expert_call(tile_e, first, n_used, xb, w_gu, b_gu, w_down, b_down, layer):
    rows, dw = xb.shape
    d = 2 * dw
    ne = w_gu.shape[1]
    f = w_down.shape[2]
    nf = f // EXPERT_TF
    n_tiles = rows // EXPERT_TM

    def tile(i, nu):
        return jnp.minimum(i, nu[0] - 1)

    def fcol(i, j, nu):
        return jnp.where(i < nu[0], j, nf - 1)

    def wcol(i, j, fi, nu):
        return jnp.where((i < nu[0]) & (fi[tile(i, nu)] == 1), j, nf - 1)

    return pl.pallas_call(
        _expert_kernel,
        grid_spec=pltpu.PrefetchScalarGridSpec(
            num_scalar_prefetch=3,
            grid=(n_tiles, nf),
            in_specs=[
                pl.BlockSpec((EXPERT_TM, dw), lambda i, j, te, fi, nu: (tile(i, nu), 0)),
                pl.BlockSpec((1, 1, d, EXPERT_TF),
                             lambda i, j, te, fi, nu: (layer, te[tile(i, nu)], 0, wcol(i, j, fi, nu))),
                pl.BlockSpec((1, 1, d, EXPERT_TF),
                             lambda i, j, te, fi, nu: (layer, te[tile(i, nu)], 0, nf + wcol(i, j, fi, nu))),
                pl.BlockSpec((1, 1, 1, EXPERT_TF),
                             lambda i, j, te, fi, nu: (layer, te[tile(i, nu)], 0, fcol(i, j, nu))),
                pl.BlockSpec((1, 1, 1, EXPERT_TF),
                             lambda i, j, te, fi, nu: (layer, te[tile(i, nu)], 0, nf + fcol(i, j, nu))),
                pl.BlockSpec((1, 1, EXPERT_TF, d),
                             lambda i, j, te, fi, nu: (layer, te[tile(i, nu)], wcol(i, j, fi, nu), 0)),
                pl.BlockSpec((1, 1, 1, d), lambda i, j, te, fi, nu: (layer, te[tile(i, nu)], 0, 0)),
            ],
            out_specs=pl.BlockSpec((EXPERT_TM, dw), lambda i, j, te, fi, nu: (i, 0)),
            scratch_shapes=[pltpu.VMEM((EXPERT_TM, d), BF16),
                            pltpu.VMEM((nf, d, EXPERT_TF), BF16),
                            pltpu.VMEM((nf, d, EXPERT_TF), BF16),
                            pltpu.VMEM((nf, EXPERT_TF, d), BF16),
                            pltpu.VMEM((nf, EXPERT_TM, EXPERT_TF), BF16)],
        ),
        out_shape=jax.ShapeDtypeStruct((rows, dw), jnp.uint32),
        compiler_params=_cparams(("arbitrary", "arbitrary")),
        name="experts",
    )(tile_e, first, n_used, xb, w_gu, w_gu, b_gu.reshape(b_gu.shape[0], ne, 1, 2 * f),
      b_gu.reshape(b_gu.shape[0], ne, 1, 2 * f), w_down, b_down.reshape(b_down.shape[0], ne, 1, d))


def _combine_kernel(dest_ref, yb_ref, gw_ref, h_ref, gate_ref, g_ref, o_ref, buf_ref, sem, *, final):
    i = pl.program_id(0)
    nsteps = pl.num_programs(0)
    tt = h_ref.shape[0]
    slot = i % 2

    def issue(tile, sl):
        def body(t, carry):
            for k in range(TOP_K):
                d = dest_ref[(tile * tt + t) * TOP_K + k]
                pltpu.make_async_copy(yb_ref.at[pl.ds(d, 1)], buf_ref.at[sl, pl.ds(k * tt + t, 1)],
                                      sem.at[sl]).start()
            return carry
        lax.fori_loop(0, tt, body, 0, unroll=ROW_DMA_UNROLL)

    @pl.when(i == 0)
    def _():
        issue(0, 0)

    @pl.when(i + 1 < nsteps)
    def _():
        issue(i + 1, 1 - slot)

    pltpu.make_async_copy(yb_ref.at[pl.ds(0, TOP_K * tt)], buf_ref.at[slot], sem.at[slot]).wait()
    gw = gw_ref[...]
    y_lo, y_hi = None, None
    for k in range(TOP_K):
        lo, hi = _unpack_halves(buf_ref[slot, k * tt:(k + 1) * tt])
        wk = gw[:, k:k + 1]
        y_lo = wk * lo if y_lo is None else y_lo + wk * lo
        y_hi = wk * hi if y_hi is None else y_hi + wk * hi
    hn = h_ref[...] + gate_ref[0] * jnp.concatenate([y_lo, y_hi], axis=-1)
    if final:
        hn = _rms(hn, g_ref[...])
    o_ref[...] = hn


def _combine_call(dest_flat, yb, gw, h, gate, g_final, seq, final):
    n, d = h.shape
    tiles_per_b = seq // COMBINE_TT
    return pl.pallas_call(
        functools.partial(_combine_kernel, final=final),
        grid_spec=pltpu.PrefetchScalarGridSpec(
            num_scalar_prefetch=1,
            grid=(n // COMBINE_TT,),
            in_specs=[
                pl.BlockSpec(memory_space=pl.ANY),
                pl.BlockSpec((COMBINE_TT, LANES), lambda i, dest: (i, 0)),
                pl.BlockSpec((COMBINE_TT, d), lambda i, dest: (i, 0)),
                pl.BlockSpec((1, 1, d), lambda i, dest: (i // tiles_per_b, 0, 0)),
                pl.BlockSpec((1, d), lambda i, dest: (0, 0)),
            ],
            out_specs=pl.BlockSpec((COMBINE_TT, d), lambda i, dest: (i, 0)),
            scratch_shapes=[pltpu.VMEM((2, TOP_K * COMBINE_TT, d // 2), jnp.uint32),
                            pltpu.SemaphoreType.DMA((2,))],
        ),
        out_shape=jax.ShapeDtypeStruct((n, d), F32),
        compiler_params=_cparams(("arbitrary",)),
        name="combine",
    )(dest_flat, yb, gw, h, gate, g_final)


def _rope_tables(positions, rot_dim, scale):
    half = rot_dim // 2
    inv_freq = ROPE_THETA ** (-jnp.arange(0, rot_dim, 2, dtype=F32) / rot_dim)
    ang = positions.reshape(-1).astype(F32)[:, None] * inv_freq
    cos, sin = jnp.cos(ang), jnp.sin(ang)
    n = ang.shape[0]
    c = jnp.concatenate([cos, cos, jnp.ones((n, LANES - rot_dim), F32)], axis=1)
    sa = jnp.concatenate([-sin, jnp.zeros((n, LANES - half), F32)], axis=1)
    sb = jnp.concatenate([jnp.zeros((n, half), F32), sin, jnp.zeros((n, LANES - rot_dim), F32)], axis=1)
    t = jnp.stack([c, sa, sb])
    return jnp.stack([t * scale, t])


def _moe(layer, h, u, idx, gw, gate_f, g_final, final, seq, w_gu, b_gu, w_down, b_down):
    n, d = h.shape
    rows = -(-(n * TOP_K + N_EXPERTS * (EXPERT_TM - 1)) // EXPERT_TM) * EXPERT_TM
    n_tiles = rows // EXPERT_TM
    dest, stats = _route_call(idx)
    dest_flat = dest[:, :TOP_K].reshape(-1)
    cnt = stats[1, :N_EXPERTS].astype(jnp.int32)
    tile_end = jnp.cumsum(stats[0, :N_EXPERTS].astype(jnp.int32))
    pend = tile_end * EXPERT_TM
    pstart = jnp.concatenate([jnp.zeros((1,), jnp.int32), pend[:-1]])
    tile_e = jnp.minimum(jnp.searchsorted(tile_end, jnp.arange(n_tiles, dtype=jnp.int32), side='right'),
                         N_EXPERTS - 1).astype(jnp.int32)
    first = jnp.concatenate([jnp.ones((1,), jnp.int32), (tile_e[1:] != tile_e[:-1]).astype(jnp.int32)])
    n_used = tile_end[-1:].astype(jnp.int32)
    xb = _dispatch_call(dest_flat, pstart, cnt, pend, u, rows)
    yb = _expert_call(tile_e, first, n_used, xb, w_gu, b_gu, w_down, b_down, layer)
    return _combine_call(dest_flat, yb, gw, h, gate_f, g_final, seq, final)


def kernel(x, c, positions, ada_w, ada_b, mix_norm_g, ffn_norm_g, final_norm_g, diff_w_in, diff_lambda, diff_subln_g, diff_w_out, mla_w_in, mla_q_norm_g, mla_kv_norm_g, mla_w_uq, mla_w_ukv, mla_w_out, moe_w_router, moe_b_router, moe_w_gate_up, moe_b_gate_up, moe_w_down, moe_b_down):
    batch, seq, d = x.shape
    depth = ada_w.shape[0]
    n = batch * seq
    assert seq % ATT_T == 0 and ATT_T % CHUNK == 0 and d % PAIR == 0
    assert n % DISPATCH_TT == 0 and n % ROUTE_TB == 0 and seq % DIN_TM == 0

    diff_scale = DIFF_HEAD_DIM ** -0.5
    mla_scale = (MLA_NOPE + MLA_ROPE) ** -0.5
    tab_d = _rope_tables(positions, DIFF_HEAD_DIM // 4, diff_scale)
    tab_m = _rope_tables(positions, MLA_ROPE, mla_scale)

    mod = _mod_call(c, ada_w, ada_b)
    mod = mod.reshape(depth, batch, N_MOD, 1, d)

    diff_w_in_bf = diff_w_in.astype(BF16)
    diff_w_out_bf = diff_w_out.astype(BF16)
    mla_w_out_bf = mla_w_out.astype(BF16)
    nm = mla_w_in.shape[0]
    heads_m = mla_w_out.shape[1] // MLA_V
    lat_pad = LANES - MLA_ROPE
    mla_w_in_bf = jnp.pad(mla_w_in, ((0, 0), (0, 0), (0, lat_pad))).astype(BF16)
    wq = mla_w_uq.reshape(nm, MLA_Q_RANK, heads_m, MLA_NOPE + MLA_ROPE)
    wq_rope = jnp.pad(wq[..., MLA_NOPE:], ((0, 0), (0, 0), (0, 0), (0, LANES - MLA_ROPE)))
    mla_w_uq_bf = jnp.concatenate([wq[..., :MLA_NOPE].reshape(nm, MLA_Q_RANK, -1),
                                   wq_rope.reshape(nm, MLA_Q_RANK, -1)], axis=-1).astype(BF16)
    wkv = mla_w_ukv.reshape(nm, MLA_KV_RANK, heads_m, MLA_NOPE + MLA_V)
    mla_w_ukv_bf = jnp.concatenate([wkv[..., :MLA_NOPE].reshape(nm, MLA_KV_RANK, -1),
                                    wkv[..., MLA_NOPE:].reshape(nm, MLA_KV_RANK, -1)], axis=-1).astype(BF16)

    h = x.reshape(n, d)
    for i in range(depth):
        shift_a, scale_a, gate_a, shift_f, scale_f, gate_f = [mod[i, :, m] for m in range(N_MOD)]
        g_mix = mix_norm_g[i].reshape(1, d)
        g_ffn = ffn_norm_g[i].reshape(1, d)
        j = i // N_MIXERS
        if i % N_MIXERS == 0:
            lambda_init = 0.8 - 0.6 * math.exp(-0.3 * i)
            qkv = _diff_in_call(h, g_mix, scale_a, shift_a, diff_w_in_bf, j, tab_d, seq)
            o = _diff_attn_call(qkv, diff_lambda[j], diff_subln_g[j].reshape(1, -1), lambda_init, batch, seq)
            w_out = diff_w_out_bf
        else:
            proj, kr = _mla_in_call(h, g_mix, scale_a, shift_a, mla_w_in_bf, mla_q_norm_g[j].reshape(1, -1),
                                    mla_kv_norm_g[j].reshape(1, -1), mla_w_uq_bf, mla_w_ukv_bf, j, tab_m, seq,
                                    mla_scale)
            o = _mla_attn_call(proj, kr, batch, seq)
            w_out = mla_w_out_bf
        wr_hi = moe_w_router[i:i + 1].astype(BF16)
        wr_lo = (moe_w_router[i:i + 1] - wr_hi.astype(F32)).astype(BF16)
        wr3 = jnp.concatenate([wr_hi, wr_hi, wr_lo], axis=1)
        h, u, idx, gw = _attn_out_call(o, w_out, j, h, gate_a, g_ffn, scale_f, shift_f,
                                       wr3, moe_b_router[i:i + 1].reshape(1, 1, -1), seq)
        h = _moe(i, h, u, idx, gw, gate_f, final_norm_g.reshape(1, d), i == depth - 1, seq,
                 moe_w_gate_up, moe_b_gate_up, moe_w_down, moe_b_down)
    return h.reshape(batch, seq, d)
```

```python
import functools
import math

import jax
import jax.numpy as jnp
from jax import lax
from jax.experimental import pallas as pl
from jax.experimental.pallas import tpu as pltpu

F32 = jnp.float32
BF16 = jnp.bfloat16

CHUNK = 64
N_MIXERS = 2
ROPE_THETA = 500000.0
RMS_EPS = 1e-6
N_MOD = 6
DIFF_HEAD_DIM = 128
MLA_V = 128
MLA_Q_RANK = 512
MLA_KV_RANK = 512
MLA_NOPE = 128
MLA_ROPE = 64
N_EXPERTS = 32
TOP_K = 4
SWIGLU_LIMIT = 7.0
SWIGLU_ALPHA = 1.702

LANES = 128
SUBLANES = 8
V7X_VMEM_BYTES = 64 * 1024 * 1024
VMEM_LIMIT = 56 * 1024 * 1024

NEG_BIG = -1e30
LOG2E = math.log2(math.e)

ATT_T = 256
PAIR = 256
EXPERT_TM = 576
EXPERT_TF = 256
EXPERT_ROW_BLOCKS = 2
ROUTE_TB = 512
DISPATCH_TT = 512
COMBINE_TT = 256


def _cparams(sem, **kw):
    return pltpu.CompilerParams(dimension_semantics=sem, vmem_limit_bytes=VMEM_LIMIT, **kw)


def _sigmoid(x):
    return 1.0 / (1.0 + jnp.exp(-x))


def _rms(x, g):
    ms = jnp.mean(x * x, axis=-1, keepdims=True)
    return x * lax.rsqrt(ms + RMS_EPS) * g


def _norm_mod(x, g, scale, shift):
    return _rms(x, g) * (1.0 + scale) + shift


def _pack_halves(x):
    w = x.shape[-1] // 2
    lo = lax.bitcast_convert_type(x[:, :w].astype(BF16).astype(F32), jnp.uint32)
    hi = lax.bitcast_convert_type(x[:, w:].astype(BF16).astype(F32), jnp.uint32)
    return (lo >> 16) | (hi & jnp.uint32(0xFFFF0000))


def _unpack_halves(words):
    lo = lax.bitcast_convert_type(words << 16, F32)
    hi = lax.bitcast_convert_type(words & jnp.uint32(0xFFFF0000), F32)
    return lo, hi


def _rope128(x, c, sa, sb, half):
    return x * c + pltpu.roll(x, LANES - half, 1) * sa + pltpu.roll(x, half, 1) * sb


MOD_TN = 512
MOD_KC = 64


def _mod_kernel(cb_ref, w_ref, b_ref, o_ref, cs_ref):
    nb = cb_ref.shape[0]
    d = cb_ref.shape[1]
    nj = MOD_TN // LANES

    @pl.when((pl.program_id(0) == 0) & (pl.program_id(1) == 0))
    def _():
        c = cb_ref[...]
        cs_ref[...] = c * _sigmoid(c)

    def body(kc, accs):
        k0 = pl.multiple_of(kc * MOD_KC, MOD_KC)
        w = w_ref[0, pl.ds(k0, MOD_KC), :]
        out = []
        for b in range(nb):
            cb = cs_ref[b, pl.ds(k0, MOD_KC), :].reshape(MOD_KC // SUBLANES, SUBLANES, LANES)
            row = []
            for jj in range(nj):
                w3 = w[:, jj * LANES:(jj + 1) * LANES].reshape(MOD_KC // SUBLANES, SUBLANES, LANES)
                row.append(accs[b][jj] + jnp.sum(w3 * cb, axis=0))
            out.append(tuple(row))
        return tuple(out)

    zero = jnp.zeros((SUBLANES, LANES), F32)
    init = tuple(tuple(zero for _ in range(nj)) for _ in range(nb))
    accs = lax.fori_loop(0, d // MOD_KC, body, init)
    for b in range(nb):
        for jj in range(nj):
            r = jnp.sum(accs[b][jj], axis=0, keepdims=True)
            o_ref[0, b:b + 1, jj * LANES:(jj + 1) * LANES] = r + b_ref[0, :, jj * LANES:(jj + 1) * LANES]


def _mod_call(c, ada_w, ada_b):
    depth, d, n6 = ada_w.shape
    nb = c.shape[0]
    cb = jnp.broadcast_to(c[:, :, None], (nb, d, LANES))
    return pl.pallas_call(
        _mod_kernel,
        grid=(depth, n6 // MOD_TN),
        in_specs=[
            pl.BlockSpec((nb, d, LANES), lambda l, j: (0, 0, 0)),
            pl.BlockSpec((1, d, MOD_TN), lambda l, j: (l, 0, j)),
            pl.BlockSpec((1, 1, MOD_TN), lambda l, j: (l, 0, j)),
        ],
        out_specs=pl.BlockSpec((1, nb, MOD_TN), lambda l, j: (l, 0, j)),
        out_shape=jax.ShapeDtypeStruct((depth, nb, n6), F32),
        scratch_shapes=[pltpu.VMEM((nb, d, LANES), F32)],
        compiler_params=_cparams(("arbitrary", "arbitrary")),
        name="adaln_mod",
    )(cb, ada_w, ada_b.reshape(depth, 1, n6))


DIN_TM = 512
DIN_TN = 1024


def _diff_in_kernel(h_ref, g_ref, sc_ref, sh_ref, w_ref, tab_ref, o_ref, u_ref, *, nqb):
    j = pl.program_id(1)
    nh = DIN_TN // PAIR

    @pl.when(j == 0)
    def _():
        u_ref[...] = _norm_mod(h_ref[...], g_ref[...], sc_ref[0], sh_ref[0]).astype(BF16)

    def slab(hh):
        return jnp.dot(u_ref[...], w_ref[0, :, hh * PAIR:(hh + 1) * PAIR], preferred_element_type=F32)

    @pl.when(j < 2 * nqb)
    def _():
        c = tab_ref[0, 0]
        sa = tab_ref[0, 1]
        sb = tab_ref[0, 2]
        for hh in range(nh):
            acc = slab(hh)
            for half in range(2):
                y = _rope128(acc[:, half * LANES:(half + 1) * LANES], c, sa, sb, DIFF_HEAD_DIM // 8)
                o_ref[0, hh, :, half * LANES:(half + 1) * LANES] = y.astype(BF16)

    @pl.when(j >= 2 * nqb)
    def _():
        for hh in range(nh):
            o_ref[0, hh] = slab(hh).astype(BF16)


def _diff_in_call(h, g, scale, shift, w_bf, layer, tab, seq):
    n, d = h.shape
    nout = w_bf.shape[2]
    third = nout // 3
    nqb = third // DIN_TN
    heads = third // PAIR
    hpb = DIN_TN // PAIR
    tiles_per_b = seq // DIN_TM
    return pl.pallas_call(
        functools.partial(_diff_in_kernel, nqb=nqb),
        grid=(n // DIN_TM, nout // DIN_TN),
        in_specs=[
            pl.BlockSpec((DIN_TM, d), lambda i, j: (i, 0)),
            pl.BlockSpec((1, d), lambda i, j: (0, 0)),
            pl.BlockSpec((1, 1, d), lambda i, j: (i // tiles_per_b, 0, 0)),
            pl.BlockSpec((1, 1, d), lambda i, j: (i // tiles_per_b, 0, 0)),
            pl.BlockSpec((1, d, DIN_TN), lambda i, j: (layer, 0, j)),
            pl.BlockSpec((1, 3, DIN_TM, LANES), lambda i, j: (jnp.minimum(j // nqb, 1), 0, i, 0)),
        ],
        out_specs=pl.BlockSpec((1, hpb, DIN_TM, PAIR), lambda i, j: (j // nqb, j % nqb, i, 0)),
        out_shape=jax.ShapeDtypeStruct((3, heads, n, PAIR), BF16),
        scratch_shapes=[pltpu.VMEM((DIN_TM, d), BF16)],
        compiler_params=_cparams(("parallel", "arbitrary")),
        name="diff_in",
    )(h, g, scale, shift, w_bf, tab)


def _chunk_mask():
    r = lax.broadcasted_iota(jnp.int32, (ATT_T, ATT_T), 0) // CHUNK
    c = lax.broadcasted_iota(jnp.int32, (ATT_T, ATT_T), 1) // CHUNK
    return c <= r


def _qk(q, k):
    return lax.dot_general(q, k, (((1,), (1,)), ((), ())), preferred_element_type=F32)


def _fold_lanes(x):
    return [x[:, c * LANES:(c + 1) * LANES] for c in range(ATT_T // LANES)]


def _two_pass_softmax(qi, n_chain, score_fn, value_fn, s_ref, acc_ref):
    mask = _chunk_mask()

    def pass1(kt, mxs, masked):
        new = []
        for e in range(n_chain):
            s = score_fn(e, kt)
            if masked:
                s = jnp.where(mask, s, NEG_BIG)
            s_ref[e, kt] = s
            mx = mxs[e]
            for part in _fold_lanes(s):
                mx = jnp.maximum(mx, part)
            new.append(mx)
        return tuple(new)

    mx0 = jnp.full((ATT_T, LANES), NEG_BIG, F32)
    mxs = lax.fori_loop(0, qi, lambda kt, c: pass1(kt, c, False), (mx0,) * n_chain)
    mxs = pass1(qi, mxs, True)
    ms = [jnp.max(mx, axis=-1, keepdims=True) for mx in mxs]
    acc_ref[...] = jnp.zeros_like(acc_ref)

    def pass2(kt, sums):
        new = []
        for e in range(n_chain):
            p = jnp.exp2(s_ref[e, kt] - ms[e])
            sm = sums[e]
            for part in _fold_lanes(p):
                sm = sm + part
            new.append(sm)
            acc_ref[e] += jnp.dot(p.astype(BF16), value_fn(e, kt), preferred_element_type=F32)
        return tuple(new)

    sums = lax.fori_loop(0, qi + 1, pass2, (jnp.zeros((ATT_T, LANES), F32),) * n_chain)
    return [jnp.sum(sm, axis=-1, keepdims=True) for sm in sums]


DIFF_HEADS_PER_STEP = 4


def _diff_attn_kernel(q_ref, k_ref, v_ref, lam_ref, g_ref, o_ref, s_ref, acc_ref, *, lambda_init):
    qi = pl.program_id(1)
    heads = q_ref.shape[1]
    hps = DIFF_HEADS_PER_STEP
    lam = lam_ref[...]
    lam_full = (jnp.exp(jnp.sum(lam[0:1] * lam[1:2], axis=-1, keepdims=True))
                - jnp.exp(jnp.sum(lam[2:3] * lam[3:4], axis=-1, keepdims=True)) + lambda_init)

    def group_body(gi, carry):
        def score(e, kt):
            h = gi * hps + e // 2
            sl = slice((e % 2) * LANES, (e % 2 + 1) * LANES)
            k0 = pl.multiple_of(kt * ATT_T, ATT_T)
            return _qk(q_ref[0, h, :, sl], k_ref[0, h, pl.ds(k0, ATT_T), sl])

        def value(e, kt):
            k0 = pl.multiple_of(kt * ATT_T, ATT_T)
            return v_ref[0, gi * hps + e // 2, pl.ds(k0, ATT_T), :]

        ls = _two_pass_softmax(qi, 2 * hps, score, value, s_ref, acc_ref)
        for n in range(hps):
            o = acc_ref[2 * n] / ls[2 * n] - lam_full * (acc_ref[2 * n + 1] / ls[2 * n + 1])
            o = _rms(o, g_ref[...]) * (1.0 - lambda_init)
            o_ref[gi * hps + n] = o.astype(BF16)
        return carry

    lax.fori_loop(0, heads // hps, group_body, 0)


def _diff_attn_call(qkv, lam, g, lambda_init, batch, seq):
    _, heads, n, _ = qkv.shape
    nq = seq // ATT_T
    kv_spec = lambda which: pl.BlockSpec((1, heads, seq, PAIR), lambda b, qi: (which, 0, b, 0),
                                         pipeline_mode=pl.Buffered(1))
    return pl.pallas_call(
        functools.partial(_diff_attn_kernel, lambda_init=lambda_init),
        grid=(batch, nq),
        in_specs=[
            pl.BlockSpec((1, heads, ATT_T, PAIR), lambda b, qi: (0, 0, b * nq + qi, 0)),
            kv_spec(1),
            kv_spec(2),
            pl.BlockSpec((4, DIFF_HEAD_DIM), lambda b, qi: (0, 0)),
            pl.BlockSpec((1, PAIR), lambda b, qi: (0, 0)),
        ],
        out_specs=pl.BlockSpec((heads, ATT_T, PAIR), lambda b, qi: (0, b * nq + qi, 0)),
        out_shape=jax.ShapeDtypeStruct((heads, n, PAIR), BF16),
        scratch_shapes=[pltpu.VMEM((2 * DIFF_HEADS_PER_STEP, nq, ATT_T, ATT_T), F32),
                        pltpu.VMEM((2 * DIFF_HEADS_PER_STEP, ATT_T, PAIR), F32)],
        compiler_params=_cparams(("parallel", "arbitrary")),
        name="diff_attn",
    )(qkv, qkv, qkv, lam, g)


MLA_TM = 256


def _mla_in_kernel(h_ref, g_ref, sc_ref, sh_ref, win_ref, gq_ref, gkv_ref, wuq_ref, wukv_ref, tab_ref,
                   proj_ref, kr_ref, *, scale):
    u = _norm_mod(h_ref[...], g_ref[...], sc_ref[0], sh_ref[0]).astype(BF16)
    lat = jnp.dot(u, win_ref[0], preferred_element_type=F32)
    qn = _rms(lat[:, 0:MLA_Q_RANK], gq_ref[...]).astype(BF16)
    kvn = _rms(lat[:, MLA_Q_RANK:MLA_Q_RANK + MLA_KV_RANK], gkv_ref[...]).astype(BF16)
    half = MLA_ROPE // 2
    kr = lat[:, MLA_Q_RANK + MLA_KV_RANK:MLA_Q_RANK + MLA_KV_RANK + LANES]
    kr_ref[...] = _rope128(kr, tab_ref[1, 0], tab_ref[1, 1], tab_ref[1, 2], half).astype(BF16)

    npair = proj_ref.shape[1]
    width = npair * PAIR
    q = jnp.dot(qn, wuq_ref[0], preferred_element_type=F32)
    c = tab_ref[0, 0]
    sa = tab_ref[0, 1]
    sb = tab_ref[0, 2]
    for gp in range(npair):
        proj_ref[0, gp] = (q[:, gp * PAIR:(gp + 1) * PAIR] * scale).astype(BF16)
        for a in range(2):
            lo = width + gp * PAIR + a * LANES
            y = _rope128(q[:, lo:lo + LANES], c, sa, sb, half)
            proj_ref[1, gp, :, a * LANES:(a + 1) * LANES] = y.astype(BF16)
    kv = jnp.dot(kvn, wukv_ref[0], preferred_element_type=F32)
    for gp in range(npair):
        proj_ref[2, gp] = kv[:, gp * PAIR:(gp + 1) * PAIR].astype(BF16)
        proj_ref[3, gp] = kv[:, width + gp * PAIR:width + (gp + 1) * PAIR].astype(BF16)


def _mla_in_call(h, g, scale_a, shift_a, w_in, gq, gkv, w_uq, w_ukv, layer, tab, seq, scale):
    n, d = h.shape
    lat_w = w_in.shape[2]
    up_w = w_uq.shape[2]
    npair = up_w // (2 * PAIR)
    tiles_per_b = seq // MLA_TM
    const = lambda shape: pl.BlockSpec(shape, lambda i: (layer,) + (0,) * (len(shape) - 1),
                                       pipeline_mode=pl.Buffered(1))
    return pl.pallas_call(
        functools.partial(_mla_in_kernel, scale=scale),
        grid=(n // MLA_TM,),
        in_specs=[
            pl.BlockSpec((MLA_TM, d), lambda i: (i, 0)),
            pl.BlockSpec((1, d), lambda i: (0, 0)),
            pl.BlockSpec((1, 1, d), lambda i: (i // tiles_per_b, 0, 0)),
            pl.BlockSpec((1, 1, d), lambda i: (i // tiles_per_b, 0, 0)),
            const((1, d, lat_w)),
            pl.BlockSpec((1, MLA_Q_RANK), lambda i: (0, 0)),
            pl.BlockSpec((1, MLA_KV_RANK), lambda i: (0, 0)),
            const((1, MLA_Q_RANK, up_w)),
            const((1, MLA_KV_RANK, up_w)),
            pl.BlockSpec((2, 3, MLA_TM, LANES), lambda i: (0, 0, i, 0)),
        ],
        out_specs=[
            pl.BlockSpec((4, npair, MLA_TM, PAIR), lambda i: (0, 0, i, 0)),
            pl.BlockSpec((MLA_TM, LANES), lambda i: (i, 0)),
        ],
        out_shape=[
            jax.ShapeDtypeStruct((4, npair, n, PAIR), BF16),
            jax.ShapeDtypeStruct((n, LANES), BF16),
        ],
        compiler_params=_cparams(("parallel",)),
        name="mla_in",
    )(h, g, scale_a, shift_a, w_in, gq, gkv, w_uq, w_ukv, tab)


MLA_PAIRS_PER_STEP = 4


def _mla_attn_kernel(q_ref, kv_ref, kr_ref, o_ref, qc_ref, s_ref, acc_ref):
    qi = pl.program_id(1)
    npair = q_ref.shape[1]
    pps = MLA_PAIRS_PER_STEP
    lane_sl = [slice(a * LANES, (a + 1) * LANES) for a in range(2)]

    def group_body(gi, carry):
        for e in range(2 * pps):
            gp = gi * pps + e // 2
            sl = lane_sl[e % 2]
            qc_ref[e] = jnp.concatenate([q_ref[0, gp, :, sl], q_ref[1, gp, :, sl]], axis=-1)

        def score(e, kt):
            k0 = pl.multiple_of(kt * ATT_T, ATT_T)
            kc = jnp.concatenate([kv_ref[0, gi * pps + e // 2, pl.ds(k0, ATT_T), lane_sl[e % 2]],
                                  kr_ref[pl.ds(k0, ATT_T), :]], axis=-1)
            return _qk(qc_ref[e], kc)

        def value(e, kt):
            k0 = pl.multiple_of(kt * ATT_T, ATT_T)
            return kv_ref[1, gi * pps + e // 2, pl.ds(k0, ATT_T), lane_sl[e % 2]]

        ls = _two_pass_softmax(qi, 2 * pps, score, value, s_ref, acc_ref)
        for e in range(2 * pps):
            o_ref[gi * pps + e // 2, :, lane_sl[e % 2]] = (acc_ref[e] / ls[e]).astype(BF16)
        return carry

    lax.fori_loop(0, npair // pps, group_body, 0)


def _mla_attn_call(proj, kr, batch, seq):
    _, npair, n, _ = proj.shape
    nq = seq // ATT_T
    return pl.pallas_call(
        _mla_attn_kernel,
        grid=(batch, nq),
        in_specs=[
            pl.BlockSpec((2, npair, ATT_T, PAIR), lambda b, qi: (0, 0, b * nq + qi, 0)),
            pl.BlockSpec((2, npair, seq, PAIR), lambda b, qi: (1, 0, b, 0), pipeline_mode=pl.Buffered(1)),
            pl.BlockSpec((seq, LANES), lambda b, qi: (b, 0), pipeline_mode=pl.Buffered(1)),
        ],
        out_specs=pl.BlockSpec((npair, ATT_T, PAIR), lambda b, qi: (0, b * nq + qi, 0)),
        out_shape=jax.ShapeDtypeStruct((npair, n, PAIR), BF16),
        scratch_shapes=[pltpu.VMEM((2 * MLA_PAIRS_PER_STEP, ATT_T, PAIR), BF16),
                        pltpu.VMEM((2 * MLA_PAIRS_PER_STEP, nq, ATT_T, ATT_T), F32),
                        pltpu.VMEM((2 * MLA_PAIRS_PER_STEP, ATT_T, LANES), F32)],
        compiler_params=_cparams(("parallel", "arbitrary")),
        name="mla_attn",
    )(proj, proj, kr)


AOUT_TM = 512
AOUT_ROW_BLOCKS = 2


def _attn_out_kernel(o_ref, w_ref, h_ref, gate_ref, g_ref, sc_ref, sh_ref, wr_ref, br_ref,
                     hn_ref, u_ref, idx_ref, gw_ref):
    ns = o_ref.shape[0]
    tm = h_ref.shape[0] // AOUT_ROW_BLOCKS
    for r in range(AOUT_ROW_BLOCKS):
        rs = slice(r * tm, (r + 1) * tm)
        y = jnp.dot(o_ref[0, rs, :], w_ref[0, 0:PAIR, :], preferred_element_type=F32)
        for s in range(1, ns):
            y = y + jnp.dot(o_ref[s, rs, :], w_ref[0, s * PAIR:(s + 1) * PAIR, :], preferred_element_type=F32)
        hn = h_ref[rs, :] + gate_ref[0] * y
        hn_ref[rs, :] = hn
        u = _norm_mod(hn, g_ref[...], sc_ref[0], sh_ref[0])
        u_ref[rs, :] = _pack_halves(u)
        u_hi = u.astype(BF16)
        u_lo = (u - u_hi.astype(F32)).astype(BF16)
        logits = jnp.dot(jnp.concatenate([u_hi, u_lo, u_hi], axis=-1), wr_ref[0],
                         preferred_element_type=F32) + br_ref[0]
        ne = logits.shape[1]
        eio = lax.broadcasted_iota(jnp.int32, (tm, ne), 1)
        lane = lax.broadcasted_iota(jnp.int32, (tm, LANES), 1)
        idx_out = jnp.zeros((tm, LANES), jnp.int32)
        val_out = jnp.zeros((tm, LANES), F32)
        l = logits
        v0 = None
        den = None
        for k in range(TOP_K):
            m = jnp.max(l, axis=-1, keepdims=True)
            i = jnp.min(jnp.where(l == m, eio, ne), axis=-1, keepdims=True)
            l = jnp.where(eio == i, -jnp.inf, l)
            if k == 0:
                v0 = m
            e = jnp.exp(m - v0)
            den = e if den is None else den + e
            idx_out = jnp.where(lane == k, i, idx_out)
            val_out = jnp.where(lane == k, e, val_out)
        idx_ref[rs, :] = idx_out
        gw_ref[rs, :] = val_out / den


def _attn_out_call(o, w_out, layer, h, gate, g, scale, shift, w_router, b_router, seq):
    ns, n, _ = o.shape
    d = h.shape[1]
    ne = w_router.shape[2]
    tiles_per_b = seq // AOUT_TM
    perb = lambda: pl.BlockSpec((1, 1, d), lambda i: (i // tiles_per_b, 0, 0))
    return pl.pallas_call(
        _attn_out_kernel,
        grid=(n // AOUT_TM,),
        in_specs=[
            pl.BlockSpec((ns, AOUT_TM, PAIR), lambda i: (0, i, 0)),
            pl.BlockSpec((1, d, d), lambda i: (layer, 0, 0), pipeline_mode=pl.Buffered(1)),
            pl.BlockSpec((AOUT_TM, d), lambda i: (i, 0)),
            perb(),
            pl.BlockSpec((1, d), lambda i: (0, 0)),
            perb(),
            perb(),
            pl.BlockSpec((1, 3 * d, ne), lambda i: (0, 0, 0)),
            pl.BlockSpec((1, 1, ne), lambda i: (0, 0, 0)),
        ],
        out_specs=[
            pl.BlockSpec((AOUT_TM, d), lambda i: (i, 0)),
            pl.BlockSpec((AOUT_TM, d // 2), lambda i: (i, 0)),
            pl.BlockSpec((AOUT_TM, LANES), lambda i: (i, 0)),
            pl.BlockSpec((AOUT_TM, LANES), lambda i: (i, 0)),
        ],
        out_shape=[
            jax.ShapeDtypeStruct((n, d), F32),
            jax.ShapeDtypeStruct((n, d // 2), jnp.uint32),
            jax.ShapeDtypeStruct((n, LANES), jnp.int32),
            jax.ShapeDtypeStruct((n, LANES), F32),
        ],
        compiler_params=_cparams(("parallel",)),
        name="attn_out",
    )(o, w_out, h, gate, g, scale, shift, w_router, b_router)


def _route_kernel(idx_ref, dest_ref, pt_ref, carry_ref, ps_ref):
    ph = pl.program_id(0)
    t = pl.program_id(1)
    nt = pl.num_programs(1)
    tb = idx_ref.shape[0]
    idx = idx_ref[...]
    eio = lax.broadcasted_iota(jnp.int32, (tb, LANES), 1)
    sel = [eio == idx[:, k:k + 1] for k in range(TOP_K)]
    oh = jnp.where(sel[0], 1.0, 0.0)
    for k in range(1, TOP_K):
        oh = oh + jnp.where(sel[k], 1.0, 0.0)

    @pl.when(t == 0)
    def _():
        carry_ref[...] = jnp.zeros_like(carry_ref)

    @pl.when((ph == 0) & (t == nt - 1))
    def _():
        cnt = carry_ref[...] + jnp.sum(oh, axis=0, keepdims=True)
        tiles = jnp.floor((cnt + (EXPERT_TM - 0.5)) * (1.0 / EXPERT_TM))
        r = lax.broadcasted_iota(jnp.int32, (LANES, LANES), 0)
        c = lax.broadcasted_iota(jnp.int32, (LANES, LANES), 1)
        upper = jnp.where(r < c, 1.0, 0.0).astype(BF16)
        t8 = jnp.broadcast_to(tiles, (SUBLANES, LANES))
        start = jnp.dot(t8.astype(BF16), upper, preferred_element_type=F32)
        ps_ref[...] = start[0:1] * EXPERT_TM
        row = lax.broadcasted_iota(jnp.int32, (SUBLANES, LANES), 0)
        pt_ref[...] = jnp.where(row == 0, t8, jnp.broadcast_to(cnt, (SUBLANES, LANES)))

    @pl.when(ph == 1)
    def _():
        r = lax.broadcasted_iota(jnp.int32, (tb, tb), 0)
        c = lax.broadcasted_iota(jnp.int32, (tb, tb), 1)
        lower = jnp.where(c < r, 1.0, 0.0).astype(BF16)
        before = jnp.dot(lower, oh.astype(BF16), preferred_element_type=F32)
        base = before + carry_ref[...] + ps_ref[...]
        out = jnp.zeros((tb, LANES), jnp.int32)
        for k in range(TOP_K):
            dk = jnp.sum(jnp.where(sel[k], base, 0.0), axis=-1, keepdims=True)
            out = jnp.where(eio == k, dk.astype(jnp.int32), out)
        dest_ref[...] = out

    carry_ref[...] += jnp.sum(oh, axis=0, keepdims=True)


def _route_call(idx):
    n = idx.shape[0]
    return pl.pallas_call(
        _route_kernel,
        grid=(2, n // ROUTE_TB),
        in_specs=[pl.BlockSpec((ROUTE_TB, LANES), lambda p, t: (t, 0))],
        out_specs=[
            pl.BlockSpec((ROUTE_TB, LANES), lambda p, t: (p * t, 0)),
            pl.BlockSpec((SUBLANES, LANES), lambda p, t: (0, 0)),
        ],
        out_shape=[
            jax.ShapeDtypeStruct((n, LANES), jnp.int32),
            jax.ShapeDtypeStruct((SUBLANES, LANES), F32),
        ],
        scratch_shapes=[pltpu.VMEM((1, LANES), F32), pltpu.VMEM((1, LANES), F32)],
        compiler_params=_cparams(("arbitrary", "arbitrary")),
        name="route",
    )(idx)


DISPATCH_ZROWS = 512
DISPATCH_TAIL = 64
ROW_DMA_UNROLL = 4
assert DISPATCH_ZROWS <= EXPERT_TM < 2 * DISPATCH_ZROWS and EXPERT_TM % DISPATCH_TAIL == 0


def _dispatch_kernel(dest_ref, pstart_ref, cnt_ref, pend_ref, u_ref, xb_ref, zbuf_ref, sem, zsem):
    i = pl.program_id(0)
    tt = u_ref.shape[0]
    rows = xb_ref.shape[0]
    ne = pstart_ref.shape[0]

    def zero_copy(pos, nrows):
        return pltpu.make_async_copy(zbuf_ref.at[pl.ds(0, nrows)], xb_ref.at[pl.ds(pos, nrows)], zsem)

    def pad_fill(act):
        def expert_body(e, carry):
            pos = pstart_ref[e] + cnt_ref[e]
            npad = pend_ref[e] - pos
            head = jnp.minimum(npad, (-pos) & (SUBLANES - 1))

            def head_body(r, c):
                act(zero_copy(pos + r, 1))
                return c

            lax.fori_loop(0, head, head_body, 0)
            base = pos + head
            rem = npad - head
            bit = DISPATCH_ZROWS
            while bit >= SUBLANES:
                off = rem - (rem & (2 * bit - 1))

                @pl.when((rem & bit) != 0)
                def _(off=off, bit=bit):
                    act(zero_copy(pl.multiple_of(base + off, SUBLANES), bit))

                bit //= 2
            return carry

        lax.fori_loop(0, ne, expert_body, 0)
        tail0 = pend_ref[ne - 1]

        def tail_body(b, c):
            act(zero_copy(pl.multiple_of(tail0 + b * DISPATCH_TAIL, DISPATCH_TAIL), DISPATCH_TAIL))
            return c

        lax.fori_loop(0, (rows - tail0) // DISPATCH_TAIL, tail_body, 0)

    @pl.when(i == 0)
    def _():
        zbuf_ref[...] = jnp.zeros_like(zbuf_ref)
        pad_fill(lambda c: c.start())

    def row_copy(t, d):
        return pltpu.make_async_copy(u_ref.at[pl.ds(t, 1)], xb_ref.at[pl.ds(d, 1)], sem)

    def issue(t, carry):
        for k in range(TOP_K):
            row_copy(t, dest_ref[(i * tt + t) * TOP_K + k]).start()
        return carry

    lax.fori_loop(0, tt, issue, 0, unroll=ROW_DMA_UNROLL)
    pltpu.make_async_copy(xb_ref.at[pl.ds(0, TOP_K * tt)], xb_ref.at[pl.ds(0, TOP_K * tt)], sem).wait()

    @pl.when(i == 0)
    def _():
        pad_fill(lambda c: c.wait())


def _dispatch_call(dest_flat, pstart, cnt, pend, u, rows):
    n, d = u.shape
    return pl.pallas_call(
        _dispatch_kernel,
        grid_spec=pltpu.PrefetchScalarGridSpec(
            num_scalar_prefetch=4,
            grid=(n // DISPATCH_TT,),
            in_specs=[pl.BlockSpec((DISPATCH_TT, d), lambda i, *_: (i, 0))],
            out_specs=pl.BlockSpec(memory_space=pl.ANY),
            scratch_shapes=[pltpu.VMEM((DISPATCH_ZROWS, d), u.dtype),
                            pltpu.SemaphoreType.DMA(()), pltpu.SemaphoreType.DMA(())],
        ),
        out_shape=jax.ShapeDtypeStruct((rows, d), u.dtype),
        compiler_params=_cparams(("arbitrary",), has_side_effects=True),
        name="dispatch",
    )(dest_flat, pstart, cnt, pend, u)


def _expert_kernel(te_ref, first_ref, nu_ref, x_ref, wg_ref, wu_ref, bg_ref, bu_ref, wd_ref, bd_ref, o_ref,
                   xbf_ref, wgb_ref, wub_ref, wdb_ref, act_ref):
    del te_ref
    i = pl.program_id(0)
    j = pl.program_id(1)
    nf = act_ref.shape[0]
    half = x_ref.shape[1]

    @pl.when(i < nu_ref[0])
    def _():
        @pl.when(j == 0)
        def _():
            lo, hi = _unpack_halves(x_ref[...])
            xbf_ref[:, 0:half] = lo.astype(BF16)
            xbf_ref[:, half:2 * half] = hi.astype(BF16)

        @pl.when(first_ref[i] == 1)
        def _():
            wgb_ref[j] = wg_ref[0, 0].astype(BF16)
            wub_ref[j] = wu_ref[0, 0].astype(BF16)
            wdb_ref[j] = wd_ref[0, 0].astype(BF16)

        rb = x_ref.shape[0] // EXPERT_ROW_BLOCKS
        for r in range(EXPERT_ROW_BLOCKS):
            rs = slice(r * rb, (r + 1) * rb)
            x = xbf_ref[rs, :]
            gate = jnp.dot(x, wgb_ref[j], preferred_element_type=F32) + bg_ref[0, 0]
            up = jnp.dot(x, wub_ref[j], preferred_element_type=F32) + bu_ref[0, 0]
            gate = jnp.minimum(gate, SWIGLU_LIMIT)
            up = jnp.clip(up, -SWIGLU_LIMIT, SWIGLU_LIMIT)
            glu = gate * _sigmoid(gate * SWIGLU_ALPHA)
            act_ref[j, rs, :] = ((up + 1.0) * glu).astype(BF16)

        @pl.when(j == nf - 1)
        def _():
            for r in range(EXPERT_ROW_BLOCKS):
                rs = slice(r * rb, (r + 1) * rb)
                y = jnp.dot(act_ref[0, rs, :], wdb_ref[0], preferred_element_type=F32)
                for c in range(1, nf):
                    y = y + jnp.dot(act_ref[c, rs, :], wdb_ref[c], preferred_element_type=F32)
                o_ref[rs, :] = _pack_halves(y + bd_ref[0, 0])

    @pl.when((i >= nu_ref[0]) & (j == 0))
    def _():
        o_ref[...] = jnp.zeros_like(o_ref)


def _expert_call(tile_e, first, n_used, xb, w_gu, b_gu, w_down, b_down, layer):
    rows, dw = xb.shape
    d = 2 * dw
    ne = w_gu.shape[1]
    f = w_down.shape[2]
    nf = f // EXPERT_TF
    n_tiles = rows // EXPERT_TM

    def tile(i, nu):
        return jnp.minimum(i, nu[0] - 1)

    def fcol(i, j, nu):
        return jnp.where(i < nu[0], j, nf - 1)

    def wcol(i, j, fi, nu):
        return jnp.where((i < nu[0]) & (fi[tile(i, nu)] == 1), j, nf - 1)

    return pl.pallas_call(
        _expert_kernel,
        grid_spec=pltpu.PrefetchScalarGridSpec(
            num_scalar_prefetch=3,
            grid=(n_tiles, nf),
            in_specs=[
                pl.BlockSpec((EXPERT_TM, dw), lambda i, j, te, fi, nu: (tile(i, nu), 0)),
                pl.BlockSpec((1, 1, d, EXPERT_TF),
                             lambda i, j, te, fi, nu: (layer, te[tile(i, nu)], 0, wcol(i, j, fi, nu))),
                pl.BlockSpec((1, 1, d, EXPERT_TF),
                             lambda i, j, te, fi, nu: (layer, te[tile(i, nu)], 0, nf + wcol(i, j, fi, nu))),
                pl.BlockSpec((1, 1, 1, EXPERT_TF),
                             lambda i, j, te, fi, nu: (layer, te[tile(i, nu)], 0, fcol(i, j, nu))),
                pl.BlockSpec((1, 1, 1, EXPERT_TF),
                             lambda i, j, te, fi, nu: (layer, te[tile(i, nu)], 0, nf + fcol(i, j, nu))),
                pl.BlockSpec((1, 1, EXPERT_TF, d),
                             lambda i, j, te, fi, nu: (layer, te[tile(i, nu)], wcol(i, j, fi, nu), 0)),
                pl.BlockSpec((1, 1, 1, d), lambda i, j, te, fi, nu: (layer, te[tile(i, nu)], 0, 0)),
            ],
            out_specs=pl.BlockSpec((EXPERT_TM, dw), lambda i, j, te, fi, nu: (i, 0)),
            scratch_shapes=[pltpu.VMEM((EXPERT_TM, d), BF16),
                            pltpu.VMEM((nf, d, EXPERT_TF), BF16),
                            pltpu.VMEM((nf, d, EXPERT_TF), BF16),
                            pltpu.VMEM((nf, EXPERT_TF, d), BF16),
                            pltpu.VMEM((nf, EXPERT_TM, EXPERT_TF), BF16)],
        ),
        out_shape=jax.ShapeDtypeStruct((rows, dw), jnp.uint32),
        compiler_params=_cparams(("arbitrary", "arbitrary")),
        name="experts",
    )(tile_e, first, n_used, xb, w_gu, w_gu, b_gu.reshape(b_gu.shape[0], ne, 1, 2 * f),
      b_gu.reshape(b_gu.shape[0], ne, 1, 2 * f), w_down, b_down.reshape(b_down.shape[0], ne, 1, d))


def _combine_kernel(dest_ref, yb_ref, gw_ref, h_ref, gate_ref, g_ref, o_ref, buf_ref, sem, *, final):
    i = pl.program_id(0)
    nsteps = pl.num_programs(0)
    tt = h_ref.shape[0]
    slot = i % 2

    def issue(tile, sl):
        def body(t, carry):
            for k in range(TOP_K):
                d = dest_ref[(tile * tt + t) * TOP_K + k]
                pltpu.make_async_copy(yb_ref.at[pl.ds(d, 1)], buf_ref.at[sl, pl.ds(k * tt + t, 1)],
                                      sem.at[sl]).start()
            return carry
        lax.fori_loop(0, tt, body, 0, unroll=ROW_DMA_UNROLL)

    @pl.when(i == 0)
    def _():
        issue(0, 0)

    @pl.when(i + 1 < nsteps)
    def _():
        issue(i + 1, 1 - slot)

    pltpu.make_async_copy(yb_ref.at[pl.ds(0, TOP_K * tt)], buf_ref.at[slot], sem.at[slot]).wait()
    gw = gw_ref[...]
    y_lo, y_hi = None, None
    for k in range(TOP_K):
        lo, hi = _unpack_halves(buf_ref[slot, k * tt:(k + 1) * tt])
        wk = gw[:, k:k + 1]
        y_lo = wk * lo if y_lo is None else y_lo + wk * lo
        y_hi = wk * hi if y_hi is None else y_hi + wk * hi
    hn = h_ref[...] + gate_ref[0] * jnp.concatenate([y_lo, y_hi], axis=-1)
    if final:
        hn = _rms(hn, g_ref[...])
    o_ref[...] = hn


def _combine_call(dest_flat, yb, gw, h, gate, g_final, seq, final):
    n, d = h.shape
    tiles_per_b = seq // COMBINE_TT
    return pl.pallas_call(
        functools.partial(_combine_kernel, final=final),
        grid_spec=pltpu.PrefetchScalarGridSpec(
            num_scalar_prefetch=1,
            grid=(n // COMBINE_TT,),
            in_specs=[
                pl.BlockSpec(memory_space=pl.ANY),
                pl.BlockSpec((COMBINE_TT, LANES), lambda i, dest: (i, 0)),
                pl.BlockSpec((COMBINE_TT, d), lambda i, dest: (i, 0)),
                pl.BlockSpec((1, 1, d), lambda i, dest: (i // tiles_per_b, 0, 0)),
                pl.BlockSpec((1, d), lambda i, dest: (0, 0)),
            ],
            out_specs=pl.BlockSpec((COMBINE_TT, d), lambda i, dest: (i, 0)),
            scratch_shapes=[pltpu.VMEM((2, TOP_K * COMBINE_TT, d // 2), jnp.uint32),
                            pltpu.SemaphoreType.DMA((2,))],
        ),
        out_shape=jax.ShapeDtypeStruct((n, d), F32),
        compiler_params=_cparams(("arbitrary",)),
        name="combine",
    )(dest_flat, yb, gw, h, gate, g_final)


def _rope_tables(positions, rot_dim, scale):
    half = rot_dim // 2
    inv_freq = ROPE_THETA ** (-jnp.arange(0, rot_dim, 2, dtype=F32) / rot_dim)
    ang = positions.reshape(-1).astype(F32)[:, None] * inv_freq
    cos, sin = jnp.cos(ang), jnp.sin(ang)
    n = ang.shape[0]
    c = jnp.concatenate([cos, cos, jnp.ones((n, LANES - rot_dim), F32)], axis=1)
    sa = jnp.concatenate([-sin, jnp.zeros((n, LANES - half), F32)], axis=1)
    sb = jnp.concatenate([jnp.zeros((n, half), F32), sin, jnp.zeros((n, LANES - rot_dim), F32)], axis=1)
    t = jnp.stack([c, sa, sb])
    return jnp.stack([t * scale, t])


def _moe(layer, h, u, idx, gw, gate_f, g_final, final, seq, w_gu, b_gu, w_down, b_down):
    n, d = h.shape
    rows = -(-(n * TOP_K + N_EXPERTS * (EXPERT_TM - 1)) // EXPERT_TM) * EXPERT_TM
    n_tiles = rows // EXPERT_TM
    dest, stats = _route_call(idx)
    dest_flat = dest[:, :TOP_K].reshape(-1)
    cnt = stats[1, :N_EXPERTS].astype(jnp.int32)
    tile_end = jnp.cumsum(stats[0, :N_EXPERTS].astype(jnp.int32))
    pend = tile_end * EXPERT_TM
    pstart = jnp.concatenate([jnp.zeros((1,), jnp.int32), pend[:-1]])
    tile_e = jnp.minimum(jnp.searchsorted(tile_end, jnp.arange(n_tiles, dtype=jnp.int32), side='right'),
                         N_EXPERTS - 1).astype(jnp.int32)
    first = jnp.concatenate([jnp.ones((1,), jnp.int32), (tile_e[1:] != tile_e[:-1]).astype(jnp.int32)])
    n_used = tile_end[-1:].astype(jnp.int32)
    xb = _dispatch_call(dest_flat, pstart, cnt, pend, u, rows)
    yb = _expert_call(tile_e, first, n_used, xb, w_gu, b_gu, w_down, b_down, layer)
    return _combine_call(dest_flat, yb, gw, h, gate_f, g_final, seq, final)


def kernel(x, c, positions, ada_w, ada_b, mix_norm_g, ffn_norm_g, final_norm_g, diff_w_in, diff_lambda, diff_subln_g, diff_w_out, mla_w_in, mla_q_norm_g, mla_kv_norm_g, mla_w_uq, mla_w_ukv, mla_w_out, moe_w_router, moe_b_router, moe_w_gate_up, moe_b_gate_up, moe_w_down, moe_b_down):
    batch, seq, d = x.shape
    depth = ada_w.shape[0]
    n = batch * seq
    assert seq % ATT_T == 0 and ATT_T % CHUNK == 0 and d % PAIR == 0
    assert n % DISPATCH_TT == 0 and n % ROUTE_TB == 0 and seq % DIN_TM == 0

    diff_scale = DIFF_HEAD_DIM ** -0.5 * LOG2E
    mla_scale = (MLA_NOPE + MLA_ROPE) ** -0.5 * LOG2E
    tab_d = _rope_tables(positions, DIFF_HEAD_DIM // 4, diff_scale)
    tab_m = _rope_tables(positions, MLA_ROPE, mla_scale)

    mod = _mod_call(c, ada_w, ada_b)
    mod = mod.reshape(depth, batch, N_MOD, 1, d)

    diff_w_in_bf = diff_w_in.astype(BF16)
    diff_w_out_bf = diff_w_out.astype(BF16)
    mla_w_out_bf = mla_w_out.astype(BF16)
    nm = mla_w_in.shape[0]
    heads_m = mla_w_out.shape[1] // MLA_V
    lat_pad = LANES - MLA_ROPE
    mla_w_in_bf = jnp.pad(mla_w_in, ((0, 0), (0, 0), (0, lat_pad))).astype(BF16)
    wq = mla_w_uq.reshape(nm, MLA_Q_RANK, heads_m, MLA_NOPE + MLA_ROPE)
    wq_rope = jnp.pad(wq[..., MLA_NOPE:], ((0, 0), (0, 0), (0, 0), (0, LANES - MLA_ROPE)))
    mla_w_uq_bf = jnp.concatenate([wq[..., :MLA_NOPE].reshape(nm, MLA_Q_RANK, -1),
                                   wq_rope.reshape(nm, MLA_Q_RANK, -1)], axis=-1).astype(BF16)
    wkv = mla_w_ukv.reshape(nm, MLA_KV_RANK, heads_m, MLA_NOPE + MLA_V)
    mla_w_ukv_bf = jnp.concatenate([wkv[..., :MLA_NOPE].reshape(nm, MLA_KV_RANK, -1),
                                    wkv[..., MLA_NOPE:].reshape(nm, MLA_KV_RANK, -1)], axis=-1).astype(BF16)

    h = x.reshape(n, d)
    for i in range(depth):
        shift_a, scale_a, gate_a, shift_f, scale_f, gate_f = [mod[i, :, m] for m in range(N_MOD)]
        g_mix = mix_norm_g[i].reshape(1, d)
        g_ffn = ffn_norm_g[i].reshape(1, d)
        j = i // N_MIXERS
        if i % N_MIXERS == 0:
            lambda_init = 0.8 - 0.6 * math.exp(-0.3 * i)
            qkv = _diff_in_call(h, g_mix, scale_a, shift_a, diff_w_in_bf, j, tab_d, seq)
            o = _diff_attn_call(qkv, diff_lambda[j], diff_subln_g[j].reshape(1, -1), lambda_init, batch, seq)
            w_out = diff_w_out_bf
        else:
            proj, kr = _mla_in_call(h, g_mix, scale_a, shift_a, mla_w_in_bf, mla_q_norm_g[j].reshape(1, -1),
                                    mla_kv_norm_g[j].reshape(1, -1), mla_w_uq_bf, mla_w_ukv_bf, j, tab_m, seq,
                                    mla_scale)
            o = _mla_attn_call(proj, kr, batch, seq)
            w_out = mla_w_out_bf
        wr_hi = moe_w_router[i:i + 1].astype(BF16)
        wr_lo = (moe_w_router[i:i + 1] - wr_hi.astype(F32)).astype(BF16)
        wr3 = jnp.concatenate([wr_hi, wr_hi, wr_lo], axis=1)
        h, u, idx, gw = _attn_out_call(o, w_out, j, h, gate_a, g_ffn, scale_f, shift_f,
                                       wr3, moe_b_router[i:i + 1].reshape(1, 1, -1), seq)
        h = _moe(i, h, u, idx, gw, gate_f, final_norm_g.reshape(1, d), i == depth - 1, seq,
                 moe_w_gate_up, moe_b_gate_up, moe_w_down, moe_b_down)
    return h.reshape(batch, seq, d)
```

```python
import functools
import math

import jax
import jax.numpy as jnp
from jax import lax
from jax.experimental import pallas as pl
from jax.experimental.pallas import tpu as pltpu

F32 = jnp.float32
BF16 = jnp.bfloat16

CHUNK = 64
N_MIXERS = 2
ROPE_THETA = 500000.0
RMS_EPS = 1e-6
N_MOD = 6
DIFF_HEAD_DIM = 128
MLA_V = 128
MLA_Q_RANK = 512
MLA_KV_RANK = 512
MLA_NOPE = 128
MLA_ROPE = 64
N_EXPERTS = 32
TOP_K = 4
SWIGLU_LIMIT = 7.0
SWIGLU_ALPHA = 1.702

LANES = 128
SUBLANES = 8
V7X_VMEM_BYTES = 64 * 1024 * 1024
VMEM_LIMIT = 56 * 1024 * 1024

NEG_BIG = -1e30
LOG2E = math.log2(math.e)

ATT_T = 256
PAIR = 256
EXPERT_TM = 576
EXPERT_TF = 512
EXPERT_ROW_BLOCKS = 2
ROUTE_TB = 512
DISPATCH_TT = 512
COMBINE_TT = 256


def _cparams(sem, **kw):
    return pltpu.CompilerParams(dimension_semantics=sem, vmem_limit_bytes=VMEM_LIMIT, **kw)


def _sigmoid(x):
    return 1.0 / (1.0 + jnp.exp(-x))


def _rms(x, g):
    ms = jnp.mean(x * x, axis=-1, keepdims=True)
    return x * lax.rsqrt(ms + RMS_EPS) * g


def _norm_mod(x, g, scale, shift):
    return _rms(x, g) * (1.0 + scale) + shift


def _pack_halves(x):
    w = x.shape[-1] // 2
    lo = lax.bitcast_convert_type(x[:, :w].astype(BF16).astype(F32), jnp.uint32)
    hi = lax.bitcast_convert_type(x[:, w:].astype(BF16).astype(F32), jnp.uint32)
    return (lo >> 16) | (hi & jnp.uint32(0xFFFF0000))


def _unpack_halves(words):
    lo = lax.bitcast_convert_type(words << 16, F32)
    hi = lax.bitcast_convert_type(words & jnp.uint32(0xFFFF0000), F32)
    return lo, hi


def _rope128(x, c, sa, sb, half):
    return x * c + pltpu.roll(x, LANES - half, 1) * sa + pltpu.roll(x, half, 1) * sb


MOD_TN = 512
MOD_KC = 64


def _mod_kernel(cb_ref, w_ref, b_ref, o_ref, cs_ref):
    nb = cb_ref.shape[0]
    d = cb_ref.shape[1]
    nj = MOD_TN // LANES

    @pl.when((pl.program_id(0) == 0) & (pl.program_id(1) == 0))
    def _():
        c = cb_ref[...]
        cs_ref[...] = c * _sigmoid(c)

    def body(kc, accs):
        k0 = pl.multiple_of(kc * MOD_KC, MOD_KC)
        w = w_ref[0, pl.ds(k0, MOD_KC), :]
        out = []
        for b in range(nb):
            cb = cs_ref[b, pl.ds(k0, MOD_KC), :].reshape(MOD_KC // SUBLANES, SUBLANES, LANES)
            row = []
            for jj in range(nj):
                w3 = w[:, jj * LANES:(jj + 1) * LANES].reshape(MOD_KC // SUBLANES, SUBLANES, LANES)
                row.append(accs[b][jj] + jnp.sum(w3 * cb, axis=0))
            out.append(tuple(row))
        return tuple(out)

    zero = jnp.zeros((SUBLANES, LANES), F32)
    init = tuple(tuple(zero for _ in range(nj)) for _ in range(nb))
    accs = lax.fori_loop(0, d // MOD_KC, body, init)
    for b in range(nb):
        for jj in range(nj):
            r = jnp.sum(accs[b][jj], axis=0, keepdims=True)
            o_ref[0, b:b + 1, jj * LANES:(jj + 1) * LANES] = r + b_ref[0, :, jj * LANES:(jj + 1) * LANES]


def _mod_call(c, ada_w, ada_b):
    depth, d, n6 = ada_w.shape
    nb = c.shape[0]
    cb = jnp.broadcast_to(c[:, :, None], (nb, d, LANES))
    return pl.pallas_call(
        _mod_kernel,
        grid=(depth, n6 // MOD_TN),
        in_specs=[
            pl.BlockSpec((nb, d, LANES), lambda l, j: (0, 0, 0)),
            pl.BlockSpec((1, d, MOD_TN), lambda l, j: (l, 0, j)),
            pl.BlockSpec((1, 1, MOD_TN), lambda l, j: (l, 0, j)),
        ],
        out_specs=pl.BlockSpec((1, nb, MOD_TN), lambda l, j: (l, 0, j)),
        out_shape=jax.ShapeDtypeStruct((depth, nb, n6), F32),
        scratch_shapes=[pltpu.VMEM((nb, d, LANES), F32)],
        compiler_params=_cparams(("arbitrary", "arbitrary")),
        name="adaln_mod",
    )(cb, ada_w, ada_b.reshape(depth, 1, n6))


DIN_TM = 512
DIN_TN = 1024


def _diff_in_kernel(h_ref, g_ref, sc_ref, sh_ref, w_ref, tab_ref, o_ref, u_ref, *, nqb):
    j = pl.program_id(1)
    nh = DIN_TN // PAIR

    @pl.when(j == 0)
    def _():
        u_ref[...] = _norm_mod(h_ref[...], g_ref[...], sc_ref[0], sh_ref[0]).astype(BF16)

    def slab(hh):
        return jnp.dot(u_ref[...], w_ref[0, :, hh * PAIR:(hh + 1) * PAIR], preferred_element_type=F32)

    @pl.when(j < 2 * nqb)
    def _():
        c = tab_ref[0, 0]
        sa = tab_ref[0, 1]
        sb = tab_ref[0, 2]
        for hh in range(nh):
            acc = slab(hh)
            for half in range(2):
                y = _rope128(acc[:, half * LANES:(half + 1) * LANES], c, sa, sb, DIFF_HEAD_DIM // 8)
                o_ref[0, hh, :, half * LANES:(half + 1) * LANES] = y.astype(BF16)

    @pl.when(j >= 2 * nqb)
    def _():
        for hh in range(nh):
            o_ref[0, hh] = slab(hh).astype(BF16)


def _diff_in_call(h, g, scale, shift, w_bf, layer, tab, seq):
    n, d = h.shape
    nout = w_bf.shape[2]
    third = nout // 3
    nqb = third // DIN_TN
    heads = third // PAIR
    hpb = DIN_TN // PAIR
    tiles_per_b = seq // DIN_TM
    return pl.pallas_call(
        functools.partial(_diff_in_kernel, nqb=nqb),
        grid=(n // DIN_TM, nout // DIN_TN),
        in_specs=[
            pl.BlockSpec((DIN_TM, d), lambda i, j: (i, 0)),
            pl.BlockSpec((1, d), lambda i, j: (0, 0)),
            pl.BlockSpec((1, 1, d), lambda i, j: (i // tiles_per_b, 0, 0)),
            pl.BlockSpec((1, 1, d), lambda i, j: (i // tiles_per_b, 0, 0)),
            pl.BlockSpec((1, d, DIN_TN), lambda i, j: (layer, 0, j)),
            pl.BlockSpec((1, 3, DIN_TM, LANES), lambda i, j: (jnp.minimum(j // nqb, 1), 0, i, 0)),
        ],
        out_specs=pl.BlockSpec((1, hpb, DIN_TM, PAIR), lambda i, j: (j // nqb, j % nqb, i, 0)),
        out_shape=jax.ShapeDtypeStruct((3, heads, n, PAIR), BF16),
        scratch_shapes=[pltpu.VMEM((DIN_TM, d), BF16)],
        compiler_params=_cparams(("parallel", "arbitrary")),
        name="diff_in",
    )(h, g, scale, shift, w_bf, tab)


def _chunk_mask():
    r = lax.broadcasted_iota(jnp.int32, (ATT_T, ATT_T), 0) // CHUNK
    c = lax.broadcasted_iota(jnp.int32, (ATT_T, ATT_T), 1) // CHUNK
    return c <= r


def _qk(q, k):
    return lax.dot_general(q, k, (((1,), (1,)), ((), ())), preferred_element_type=F32)


def _fold_lanes(x):
    return [x[:, c * LANES:(c + 1) * LANES] for c in range(ATT_T // LANES)]


def _two_pass_softmax(qi, n_chain, score_fn, value_fn, s_ref, acc_ref):
    mask = _chunk_mask()

    def pass1(kt, mxs, masked):
        new = []
        for e in range(n_chain):
            s = score_fn(e, kt)
            if masked:
                s = jnp.where(mask, s, NEG_BIG)
            s_ref[e, kt] = s
            mx = mxs[e]
            for part in _fold_lanes(s):
                mx = jnp.maximum(mx, part)
            new.append(mx)
        return tuple(new)

    mx0 = jnp.full((ATT_T, LANES), NEG_BIG, F32)
    mxs = lax.fori_loop(0, qi, lambda kt, c: pass1(kt, c, False), (mx0,) * n_chain)
    mxs = pass1(qi, mxs, True)
    ms = [jnp.max(mx, axis=-1, keepdims=True) for mx in mxs]
    acc_ref[...] = jnp.zeros_like(acc_ref)

    def pass2(kt, sums):
        new = []
        for e in range(n_chain):
            p = jnp.exp2(s_ref[e, kt] - ms[e])
            sm = sums[e]
            for part in _fold_lanes(p):
                sm = sm + part
            new.append(sm)
            acc_ref[e] += jnp.dot(p.astype(BF16), value_fn(e, kt), preferred_element_type=F32)
        return tuple(new)

    sums = lax.fori_loop(0, qi + 1, pass2, (jnp.zeros((ATT_T, LANES), F32),) * n_chain)
    return [jnp.sum(sm, axis=-1, keepdims=True) for sm in sums]


DIFF_HEADS_PER_STEP = 4


def _diff_attn_kernel(q_ref, k_ref, v_ref, lam_ref, g_ref, o_ref, s_ref, acc_ref, *, lambda_init):
    qi = pl.program_id(1)
    heads = q_ref.shape[1]
    hps = DIFF_HEADS_PER_STEP
    lam = lam_ref[...]
    lam_full = (jnp.exp(jnp.sum(lam[0:1] * lam[1:2], axis=-1, keepdims=True))
                - jnp.exp(jnp.sum(lam[2:3] * lam[3:4], axis=-1, keepdims=True)) + lambda_init)

    def group_body(gi, carry):
        def score(e, kt):
            h = gi * hps + e // 2
            sl = slice((e % 2) * LANES, (e % 2 + 1) * LANES)
            k0 = pl.multiple_of(kt * ATT_T, ATT_T)
            return _qk(q_ref[0, h, :, sl], k_ref[0, h, pl.ds(k0, ATT_T), sl])

        def value(e, kt):
            k0 = pl.multiple_of(kt * ATT_T, ATT_T)
            return v_ref[0, gi * hps + e // 2, pl.ds(k0, ATT_T), :]

        ls = _two_pass_softmax(qi, 2 * hps, score, value, s_ref, acc_ref)
        for n in range(hps):
            o = acc_ref[2 * n] / ls[2 * n] - lam_full * (acc_ref[2 * n + 1] / ls[2 * n + 1])
            o = _rms(o, g_ref[...]) * (1.0 - lambda_init)
            o_ref[gi * hps + n] = o.astype(BF16)
        return carry

    lax.fori_loop(0, heads // hps, group_body, 0)


def _diff_attn_call(qkv, lam, g, lambda_init, batch, seq):
    _, heads, n, _ = qkv.shape
    nq = seq // ATT_T
    kv_spec = lambda which: pl.BlockSpec((1, heads, seq, PAIR), lambda b, qi: (which, 0, b, 0),
                                         pipeline_mode=pl.Buffered(1))
    return pl.pallas_call(
        functools.partial(_diff_attn_kernel, lambda_init=lambda_init),
        grid=(batch, nq),
        in_specs=[
            pl.BlockSpec((1, heads, ATT_T, PAIR), lambda b, qi: (0, 0, b * nq + qi, 0)),
            kv_spec(1),
            kv_spec(2),
            pl.BlockSpec((4, DIFF_HEAD_DIM), lambda b, qi: (0, 0)),
            pl.BlockSpec((1, PAIR), lambda b, qi: (0, 0)),
        ],
        out_specs=pl.BlockSpec((heads, ATT_T, PAIR), lambda b, qi: (0, b * nq + qi, 0)),
        out_shape=jax.ShapeDtypeStruct((heads, n, PAIR), BF16),
        scratch_shapes=[pltpu.VMEM((2 * DIFF_HEADS_PER_STEP, nq, ATT_T, ATT_T), F32),
                        pltpu.VMEM((2 * DIFF_HEADS_PER_STEP, ATT_T, PAIR), F32)],
        compiler_params=_cparams(("parallel", "arbitrary")),
        name="diff_attn",
    )(qkv, qkv, qkv, lam, g)


MLA_TM = 256


def _mla_in_kernel(h_ref, g_ref, sc_ref, sh_ref, win_ref, gq_ref, gkv_ref, wuq_ref, wukv_ref, tab_ref,
                   proj_ref, kr_ref, *, scale):
    u = _norm_mod(h_ref[...], g_ref[...], sc_ref[0], sh_ref[0]).astype(BF16)
    lat = jnp.dot(u, win_ref[0], preferred_element_type=F32)
    qn = _rms(lat[:, 0:MLA_Q_RANK], gq_ref[...]).astype(BF16)
    kvn = _rms(lat[:, MLA_Q_RANK:MLA_Q_RANK + MLA_KV_RANK], gkv_ref[...]).astype(BF16)
    half = MLA_ROPE // 2
    kr = lat[:, MLA_Q_RANK + MLA_KV_RANK:MLA_Q_RANK + MLA_KV_RANK + LANES]
    kr_ref[...] = _rope128(kr, tab_ref[1, 0], tab_ref[1, 1], tab_ref[1, 2], half).astype(BF16)

    npair = proj_ref.shape[1]
    width = npair * PAIR
    q = jnp.dot(qn, wuq_ref[0], preferred_element_type=F32)
    c = tab_ref[0, 0]
    sa = tab_ref[0, 1]
    sb = tab_ref[0, 2]
    for gp in range(npair):
        proj_ref[0, gp] = (q[:, gp * PAIR:(gp + 1) * PAIR] * scale).astype(BF16)
        for a in range(2):
            lo = width + gp * PAIR + a * LANES
            y = _rope128(q[:, lo:lo + LANES], c, sa, sb, half)
            proj_ref[1, gp, :, a * LANES:(a + 1) * LANES] = y.astype(BF16)
    kv = jnp.dot(kvn, wukv_ref[0], preferred_element_type=F32)
    for gp in range(npair):
        proj_ref[2, gp] = kv[:, gp * PAIR:(gp + 1) * PAIR].astype(BF16)
        proj_ref[3, gp] = kv[:, width + gp * PAIR:width + (gp + 1) * PAIR].astype(BF16)


def _mla_in_call(h, g, scale_a, shift_a, w_in, gq, gkv, w_uq, w_ukv, layer, tab, seq, scale):
    n, d = h.shape
    lat_w = w_in.shape[2]
    up_w = w_uq.shape[2]
    npair = up_w // (2 * PAIR)
    tiles_per_b = seq // MLA_TM
    const = lambda shape: pl.BlockSpec(shape, lambda i: (layer,) + (0,) * (len(shape) - 1),
                                       pipeline_mode=pl.Buffered(1))
    return pl.pallas_call(
        functools.partial(_mla_in_kernel, scale=scale),
        grid=(n // MLA_TM,),
        in_specs=[
            pl.BlockSpec((MLA_TM, d), lambda i: (i, 0)),
            pl.BlockSpec((1, d), lambda i: (0, 0)),
            pl.BlockSpec((1, 1, d), lambda i: (i // tiles_per_b, 0, 0)),
            pl.BlockSpec((1, 1, d), lambda i: (i // tiles_per_b, 0, 0)),
            const((1, d, lat_w)),
            pl.BlockSpec((1, MLA_Q_RANK), lambda i: (0, 0)),
            pl.BlockSpec((1, MLA_KV_RANK), lambda i: (0, 0)),
            const((1, MLA_Q_RANK, up_w)),
            const((1, MLA_KV_RANK, up_w)),
            pl.BlockSpec((2, 3, MLA_TM, LANES), lambda i: (0, 0, i, 0)),
        ],
        out_specs=[
            pl.BlockSpec((4, npair, MLA_TM, PAIR), lambda i: (0, 0, i, 0)),
            pl.BlockSpec((MLA_TM, LANES), lambda i: (i, 0)),
        ],
        out_shape=[
            jax.ShapeDtypeStruct((4, npair, n, PAIR), BF16),
            jax.ShapeDtypeStruct((n, LANES), BF16),
        ],
        compiler_params=_cparams(("parallel",)),
        name="mla_in",
    )(h, g, scale_a, shift_a, w_in, gq, gkv, w_uq, w_ukv, tab)


MLA_PAIRS_PER_STEP = 4


def _mla_attn_kernel(q_ref, kv_ref, kr_ref, o_ref, qc_ref, s_ref, acc_ref):
    qi = pl.program_id(1)
    npair = q_ref.shape[1]
    pps = MLA_PAIRS_PER_STEP
    lane_sl = [slice(a * LANES, (a + 1) * LANES) for a in range(2)]

    def group_body(gi, carry):
        for e in range(2 * pps):
            gp = gi * pps + e // 2
            sl = lane_sl[e % 2]
            qc_ref[e] = jnp.concatenate([q_ref[0, gp, :, sl], q_ref[1, gp, :, sl]], axis=-1)

        def score(e, kt):
            k0 = pl.multiple_of(kt * ATT_T, ATT_T)
            kc = jnp.concatenate([kv_ref[0, gi * pps + e // 2, pl.ds(k0, ATT_T), lane_sl[e % 2]],
                                  kr_ref[pl.ds(k0, ATT_T), :]], axis=-1)
            return _qk(qc_ref[e], kc)

        def value(e, kt):
            k0 = pl.multiple_of(kt * ATT_T, ATT_T)
            return kv_ref[1, gi * pps + e // 2, pl.ds(k0, ATT_T), lane_sl[e % 2]]

        ls = _two_pass_softmax(qi, 2 * pps, score, value, s_ref, acc_ref)
        for e in range(2 * pps):
            o_ref[gi * pps + e // 2, :, lane_sl[e % 2]] = (acc_ref[e] / ls[e]).astype(BF16)
        return carry

    lax.fori_loop(0, npair // pps, group_body, 0)


def _mla_attn_call(proj, kr, batch, seq):
    _, npair, n, _ = proj.shape
    nq = seq // ATT_T
    return pl.pallas_call(
        _mla_attn_kernel,
        grid=(batch, nq),
        in_specs=[
            pl.BlockSpec((2, npair, ATT_T, PAIR), lambda b, qi: (0, 0, b * nq + qi, 0)),
            pl.BlockSpec((2, npair, seq, PAIR), lambda b, qi: (1, 0, b, 0), pipeline_mode=pl.Buffered(1)),
            pl.BlockSpec((seq, LANES), lambda b, qi: (b, 0), pipeline_mode=pl.Buffered(1)),
        ],
        out_specs=pl.BlockSpec((npair, ATT_T, PAIR), lambda b, qi: (0, b * nq + qi, 0)),
        out_shape=jax.ShapeDtypeStruct((npair, n, PAIR), BF16),
        scratch_shapes=[pltpu.VMEM((2 * MLA_PAIRS_PER_STEP, ATT_T, PAIR), BF16),
                        pltpu.VMEM((2 * MLA_PAIRS_PER_STEP, nq, ATT_T, ATT_T), F32),
                        pltpu.VMEM((2 * MLA_PAIRS_PER_STEP, ATT_T, LANES), F32)],
        compiler_params=_cparams(("parallel", "arbitrary")),
        name="mla_attn",
    )(proj, proj, kr)


AOUT_TM = 512
AOUT_ROW_BLOCKS = 2


def _attn_out_kernel(o_ref, w_ref, h_ref, gate_ref, g_ref, sc_ref, sh_ref, wr_ref, br_ref,
                     hn_ref, u_ref, idx_ref, gw_ref):
    ns = o_ref.shape[0]
    tm = h_ref.shape[0] // AOUT_ROW_BLOCKS
    for r in range(AOUT_ROW_BLOCKS):
        rs = slice(r * tm, (r + 1) * tm)
        y = jnp.dot(o_ref[0, rs, :], w_ref[0, 0:PAIR, :], preferred_element_type=F32)
        for s in range(1, ns):
            y = y + jnp.dot(o_ref[s, rs, :], w_ref[0, s * PAIR:(s + 1) * PAIR, :], preferred_element_type=F32)
        hn = h_ref[rs, :] + gate_ref[0] * y
        hn_ref[rs, :] = hn
        u = _norm_mod(hn, g_ref[...], sc_ref[0], sh_ref[0])
        u_ref[rs, :] = _pack_halves(u)
        u_hi = u.astype(BF16)
        u_lo = (u - u_hi.astype(F32)).astype(BF16)
        logits = jnp.dot(jnp.concatenate([u_hi, u_lo, u_hi], axis=-1), wr_ref[0],
                         preferred_element_type=F32) + br_ref[0]
        ne = logits.shape[1]
        eio = lax.broadcasted_iota(jnp.int32, (tm, ne), 1)
        lane = lax.broadcasted_iota(jnp.int32, (tm, LANES), 1)
        idx_out = jnp.zeros((tm, LANES), jnp.int32)
        val_out = jnp.zeros((tm, LANES), F32)
        l = logits
        v0 = None
        den = None
        for k in range(TOP_K):
            m = jnp.max(l, axis=-1, keepdims=True)
            i = jnp.min(jnp.where(l == m, eio, ne), axis=-1, keepdims=True)
            l = jnp.where(eio == i, -jnp.inf, l)
            if k == 0:
                v0 = m
            e = jnp.exp(m - v0)
            den = e if den is None else den + e
            idx_out = jnp.where(lane == k, i, idx_out)
            val_out = jnp.where(lane == k, e, val_out)
        idx_ref[rs, :] = idx_out
        gw_ref[rs, :] = val_out / den


def _attn_out_call(o, w_out, layer, h, gate, g, scale, shift, w_router, b_router, seq):
    ns, n, _ = o.shape
    d = h.shape[1]
    ne = w_router.shape[2]
    tiles_per_b = seq // AOUT_TM
    perb = lambda: pl.BlockSpec((1, 1, d), lambda i: (i // tiles_per_b, 0, 0))
    return pl.pallas_call(
        _attn_out_kernel,
        grid=(n // AOUT_TM,),
        in_specs=[
            pl.BlockSpec((ns, AOUT_TM, PAIR), lambda i: (0, i, 0)),
            pl.BlockSpec((1, d, d), lambda i: (layer, 0, 0), pipeline_mode=pl.Buffered(1)),
            pl.BlockSpec((AOUT_TM, d), lambda i: (i, 0)),
            perb(),
            pl.BlockSpec((1, d), lambda i: (0, 0)),
            perb(),
            perb(),
            pl.BlockSpec((1, 3 * d, ne), lambda i: (0, 0, 0)),
            pl.BlockSpec((1, 1, ne), lambda i: (0, 0, 0)),
        ],
        out_specs=[
            pl.BlockSpec((AOUT_TM, d), lambda i: (i, 0)),
            pl.BlockSpec((AOUT_TM, d // 2), lambda i: (i, 0)),
            pl.BlockSpec((AOUT_TM, LANES), lambda i: (i, 0)),
            pl.BlockSpec((AOUT_TM, LANES), lambda i: (i, 0)),
        ],
        out_shape=[
            jax.ShapeDtypeStruct((n, d), F32),
            jax.ShapeDtypeStruct((n, d // 2), jnp.uint32),
            jax.ShapeDtypeStruct((n, LANES), jnp.int32),
            jax.ShapeDtypeStruct((n, LANES), F32),
        ],
        compiler_params=_cparams(("parallel",)),
        name="attn_out",
    )(o, w_out, h, gate, g, scale, shift, w_router, b_router)


def _route_kernel(idx_ref, dest_ref, pt_ref, carry_ref, ps_ref):
    ph = pl.program_id(0)
    t = pl.program_id(1)
    nt = pl.num_programs(1)
    tb = idx_ref.shape[0]
    idx = idx_ref[...]
    eio = lax.broadcasted_iota(jnp.int32, (tb, LANES), 1)
    sel = [eio == idx[:, k:k + 1] for k in range(TOP_K)]
    oh = jnp.where(sel[0], 1.0, 0.0)
    for k in range(1, TOP_K):
        oh = oh + jnp.where(sel[k], 1.0, 0.0)

    @pl.when(t == 0)
    def _():
        carry_ref[...] = jnp.zeros_like(carry_ref)

    @pl.when((ph == 0) & (t == nt - 1))
    def _():
        cnt = carry_ref[...] + jnp.sum(oh, axis=0, keepdims=True)
        tiles = jnp.floor((cnt + (EXPERT_TM - 0.5)) * (1.0 / EXPERT_TM))
        r = lax.broadcasted_iota(jnp.int32, (LANES, LANES), 0)
        c = lax.broadcasted_iota(jnp.int32, (LANES, LANES), 1)
        upper = jnp.where(r < c, 1.0, 0.0).astype(BF16)
        t8 = jnp.broadcast_to(tiles, (SUBLANES, LANES))
        start = jnp.dot(t8.astype(BF16), upper, preferred_element_type=F32)
        ps_ref[...] = start[0:1] * EXPERT_TM
        row = lax.broadcasted_iota(jnp.int32, (SUBLANES, LANES), 0)
        pt_ref[...] = jnp.where(row == 0, t8, jnp.broadcast_to(cnt, (SUBLANES, LANES)))

    @pl.when(ph == 1)
    def _():
        r = lax.broadcasted_iota(jnp.int32, (tb, tb), 0)
        c = lax.broadcasted_iota(jnp.int32, (tb, tb), 1)
        lower = jnp.where(c < r, 1.0, 0.0).astype(BF16)
        before = jnp.dot(lower, oh.astype(BF16), preferred_element_type=F32)
        base = before + carry_ref[...] + ps_ref[...]
        out = jnp.zeros((tb, LANES), jnp.int32)
        for k in range(TOP_K):
            dk = jnp.sum(jnp.where(sel[k], base, 0.0), axis=-1, keepdims=True)
            out = jnp.where(eio == k, dk.astype(jnp.int32), out)
        dest_ref[...] = out

    carry_ref[...] += jnp.sum(oh, axis=0, keepdims=True)


def _route_call(idx):
    n = idx.shape[0]
    return pl.pallas_call(
        _route_kernel,
        grid=(2, n // ROUTE_TB),
        in_specs=[pl.BlockSpec((ROUTE_TB, LANES), lambda p, t: (t, 0))],
        out_specs=[
            pl.BlockSpec((ROUTE_TB, LANES), lambda p, t: (p * t, 0)),
            pl.BlockSpec((SUBLANES, LANES), lambda p, t: (0, 0)),
        ],
        out_shape=[
            jax.ShapeDtypeStruct((n, LANES), jnp.int32),
            jax.ShapeDtypeStruct((SUBLANES, LANES), F32),
        ],
        scratch_shapes=[pltpu.VMEM((1, LANES), F32), pltpu.VMEM((1, LANES), F32)],
        compiler_params=_cparams(("arbitrary", "arbitrary")),
        name="route",
    )(idx)


DISPATCH_ZROWS = 512
DISPATCH_TAIL = 64
ROW_DMA_UNROLL = 4
assert DISPATCH_ZROWS <= EXPERT_TM < 2 * DISPATCH_ZROWS and EXPERT_TM % DISPATCH_TAIL == 0


def _dispatch_kernel(dest_ref, pstart_ref, cnt_ref, pend_ref, u_ref, xb_ref, zbuf_ref, sem, zsem):
    i = pl.program_id(0)
    tt = u_ref.shape[0]
    rows = xb_ref.shape[0]
    ne = pstart_ref.shape[0]

    def zero_copy(pos, nrows):
        return pltpu.make_async_copy(zbuf_ref.at[pl.ds(0, nrows)], xb_ref.at[pl.ds(pos, nrows)], zsem)

    def pad_fill(act):
        def expert_body(e, carry):
            pos = pstart_ref[e] + cnt_ref[e]
            npad = pend_ref[e] - pos
            head = jnp.minimum(npad, (-pos) & (SUBLANES - 1))

            def head_body(r, c):
                act(zero_copy(pos + r, 1))
                return c

            lax.fori_loop(0, head, head_body, 0)
            base = pos + head
            rem = npad - head
            bit = DISPATCH_ZROWS
            while bit >= SUBLANES:
                off = rem - (rem & (2 * bit - 1))

                @pl.when((rem & bit) != 0)
                def _(off=off, bit=bit):
                    act(zero_copy(pl.multiple_of(base + off, SUBLANES), bit))

                bit //= 2
            return carry

        lax.fori_loop(0, ne, expert_body, 0)
        tail0 = pend_ref[ne - 1]

        def tail_body(b, c):
            act(zero_copy(pl.multiple_of(tail0 + b * DISPATCH_TAIL, DISPATCH_TAIL), DISPATCH_TAIL))
            return c

        lax.fori_loop(0, (rows - tail0) // DISPATCH_TAIL, tail_body, 0)

    @pl.when(i == 0)
    def _():
        zbuf_ref[...] = jnp.zeros_like(zbuf_ref)
        pad_fill(lambda c: c.start())

    def row_copy(t, d):
        return pltpu.make_async_copy(u_ref.at[pl.ds(t, 1)], xb_ref.at[pl.ds(d, 1)], sem)

    def issue(t, carry):
        for k in range(TOP_K):
            row_copy(t, dest_ref[(i * tt + t) * TOP_K + k]).start()
        return carry

    lax.fori_loop(0, tt, issue, 0, unroll=ROW_DMA_UNROLL)
    pltpu.make_async_copy(xb_ref.at[pl.ds(0, TOP_K * tt)], xb_ref.at[pl.ds(0, TOP_K * tt)], sem).wait()

    @pl.when(i == 0)
    def _():
        pad_fill(lambda c: c.wait())


def _dispatch_call(dest_flat, pstart, cnt, pend, u, rows):
    n, d = u.shape
    return pl.pallas_call(
        _dispatch_kernel,
        grid_spec=pltpu.PrefetchScalarGridSpec(
            num_scalar_prefetch=4,
            grid=(n // DISPATCH_TT,),
            in_specs=[pl.BlockSpec((DISPATCH_TT, d), lambda i, *_: (i, 0))],
            out_specs=pl.BlockSpec(memory_space=pl.ANY),
            scratch_shapes=[pltpu.VMEM((DISPATCH_ZROWS, d), u.dtype),
                            pltpu.SemaphoreType.DMA(()), pltpu.SemaphoreType.DMA(())],
        ),
        out_shape=jax.ShapeDtypeStruct((rows, d), u.dtype),
        compiler_params=_cparams(("arbitrary",), has_side_effects=True),
        name="dispatch",
    )(dest_flat, pstart, cnt, pend, u)


def _expert_kernel(te_ref, first_ref, nu_ref, x_ref, wg_ref, wu_ref, bg_ref, bu_ref, wd_ref, bd_ref, o_ref,
                   xbf_ref, wgb_ref, wub_ref, wdb_ref, act_ref):
    del te_ref
    i = pl.program_id(0)
    j = pl.program_id(1)
    nf = act_ref.shape[0]
    half = x_ref.shape[1]

    @pl.when(i < nu_ref[0])
    def _():
        @pl.when(j == 0)
        def _():
            lo, hi = _unpack_halves(x_ref[...])
            xbf_ref[:, 0:half] = lo.astype(BF16)
            xbf_ref[:, half:2 * half] = hi.astype(BF16)

        @pl.when(first_ref[i] == 1)
        def _():
            wgb_ref[j] = wg_ref[0, 0].astype(BF16)
            wub_ref[j] = wu_ref[0, 0].astype(BF16)
            wdb_ref[j] = wd_ref[0, 0].astype(BF16)

        rb = x_ref.shape[0] // EXPERT_ROW_BLOCKS
        for r in range(EXPERT_ROW_BLOCKS):
            rs = slice(r * rb, (r + 1) * rb)
            x = xbf_ref[rs, :]
            gate = jnp.dot(x, wgb_ref[j], preferred_element_type=F32) + bg_ref[0, 0]
            up = jnp.dot(x, wub_ref[j], preferred_element_type=F32) + bu_ref[0, 0]
            gate = jnp.minimum(gate, SWIGLU_LIMIT)
            up = jnp.clip(up, -SWIGLU_LIMIT, SWIGLU_LIMIT)
            glu = gate * _sigmoid(gate * SWIGLU_ALPHA)
            act_ref[j, rs, :] = ((up + 1.0) * glu).astype(BF16)

        @pl.when(j == nf - 1)
        def _():
            for r in range(EXPERT_ROW_BLOCKS):
                rs = slice(r * rb, (r + 1) * rb)
                y = jnp.dot(act_ref[0, rs, :], wdb_ref[0], preferred_element_type=F32)
                for c in range(1, nf):
                    y = y + jnp.dot(act_ref[c, rs, :], wdb_ref[c], preferred_element_type=F32)
                o_ref[rs, :] = _pack_halves(y + bd_ref[0, 0])

    @pl.when((i >= nu_ref[0]) & (j == 0))
    def _():
        o_ref[...] = jnp.zeros_like(o_ref)


def _expert_call(tile_e, first, n_used, xb, w_gu, b_gu, w_down, b_down, layer):
    rows, dw = xb.shape
    d = 2 * dw
    ne = w_gu.shape[1]
    f = w_down.shape[2]
    nf = f // EXPERT_TF
    n_tiles = rows // EXPERT_TM

    def tile(i, nu):
        return jnp.minimum(i, nu[0] - 1)

    def fcol(i, j, nu):
        return jnp.where(i < nu[0], j, nf - 1)

    def wcol(i, j, fi, nu):
        return jnp.where((i < nu[0]) & (fi[tile(i, nu)] == 1), j, nf - 1)

    return pl.pallas_call(
        _expert_kernel,
        grid_spec=pltpu.PrefetchScalarGridSpec(
            num_scalar_prefetch=3,
            grid=(n_tiles, nf),
            in_specs=[
                pl.BlockSpec((EXPERT_TM, dw), lambda i, j, te, fi, nu: (tile(i, nu), 0)),
                pl.BlockSpec((1, 1, d, EXPERT_TF),
                             lambda i, j, te, fi, nu: (layer, te[tile(i, nu)], 0, wcol(i, j, fi, nu))),
                pl.BlockSpec((1, 1, d, EXPERT_TF),
                             lambda i, j, te, fi, nu: (layer, te[tile(i, nu)], 0, nf + wcol(i, j, fi, nu))),
                pl.BlockSpec((1, 1, 1, EXPERT_TF),
                             lambda i, j, te, fi, nu: (layer, te[tile(i, nu)], 0, fcol(i, j, nu))),
                pl.BlockSpec((1, 1, 1, EXPERT_TF),
                             lambda i, j, te, fi, nu: (layer, te[tile(i, nu)], 0, nf + fcol(i, j, nu))),
                pl.BlockSpec((1, 1, EXPERT_TF, d),
                             lambda i, j, te, fi, nu: (layer, te[tile(i, nu)], wcol(i, j, fi, nu), 0)),
                pl.BlockSpec((1, 1, 1, d), lambda i, j, te, fi, nu: (layer, te[tile(i, nu)], 0, 0)),
            ],
            out_specs=pl.BlockSpec((EXPERT_TM, dw), lambda i, j, te, fi, nu: (i, 0)),
            scratch_shapes=[pltpu.VMEM((EXPERT_TM, d), BF16),
                            pltpu.VMEM((nf, d, EXPERT_TF), BF16),
                            pltpu.VMEM((nf, d, EXPERT_TF), BF16),
                            pltpu.VMEM((nf, EXPERT_TF, d), BF16),
                            pltpu.VMEM((nf, EXPERT_TM, EXPERT_TF), BF16)],
        ),
        out_shape=jax.ShapeDtypeStruct((rows, dw), jnp.uint32),
        compiler_params=_cparams(("arbitrary", "arbitrary")),
        name="experts",
    )(tile_e, first, n_used, xb, w_gu, w_gu, b_gu.reshape(b_gu.shape[0], ne, 1, 2 * f),
      b_gu.reshape(b_gu.shape[0], ne, 1, 2 * f), w_down, b_down.reshape(b_down.shape[0], ne, 1, d))


def _combine_kernel(dest_ref, yb_ref, gw_ref, h_ref, gate_ref, g_ref, o_ref, buf_ref, sem, *, final):
    i = pl.program_id(0)
    nsteps = pl.num_programs(0)
    tt = h_ref.shape[0]
    slot = i % 2

    def issue(tile, sl):
        def body(t, carry):
            for k in range(TOP_K):
                d = dest_ref[(tile * tt + t) * TOP_K + k]
                pltpu.make_async_copy(yb_ref.at[pl.ds(d, 1)], buf_ref.at[sl, pl.ds(k * tt + t, 1)],
                                      sem.at[sl]).start()
            return carry
        lax.fori_loop(0, tt, body, 0, unroll=ROW_DMA_UNROLL)

    @pl.when(i == 0)
    def _():
        issue(0, 0)

    @pl.when(i + 1 < nsteps)
    def _():
        issue(i + 1, 1 - slot)

    pltpu.make_async_copy(yb_ref.at[pl.ds(0, TOP_K * tt)], buf_ref.at[slot], sem.at[slot]).wait()
    gw = gw_ref[...]
    y_lo, y_hi = None, None
    for k in range(TOP_K):
        lo, hi = _unpack_halves(buf_ref[slot, k * tt:(k + 1) * tt])
        wk = gw[:, k:k + 1]
        y_lo = wk * lo if y_lo is None else y_lo + wk * lo
        y_hi = wk * hi if y_hi is None else y_hi + wk * hi
    hn = h_ref[...] + gate_ref[0] * jnp.concatenate([y_lo, y_hi], axis=-1)
    if final:
        hn = _rms(hn, g_ref[...])
    o_ref[...] = hn


def _combine_call(dest_flat, yb, gw, h, gate, g_final, seq, final):
    n, d = h.shape
    tiles_per_b = seq // COMBINE_TT
    return pl.pallas_call(
        functools.partial(_combine_kernel, final=final),
        grid_spec=pltpu.PrefetchScalarGridSpec(
            num_scalar_prefetch=1,
            grid=(n // COMBINE_TT,),
            in_specs=[
                pl.BlockSpec(memory_space=pl.ANY),
                pl.BlockSpec((COMBINE_TT, LANES), lambda i, dest: (i, 0)),
                pl.BlockSpec((COMBINE_TT, d), lambda i, dest: (i, 0)),
                pl.BlockSpec((1, 1, d), lambda i, dest: (i // tiles_per_b, 0, 0)),
                pl.BlockSpec((1, d), lambda i, dest: (0, 0)),
            ],
            out_specs=pl.BlockSpec((COMBINE_TT, d), lambda i, dest: (i, 0)),
            scratch_shapes=[pltpu.VMEM((2, TOP_K * COMBINE_TT, d // 2), jnp.uint32),
                            pltpu.SemaphoreType.DMA((2,))],
        ),
        out_shape=jax.ShapeDtypeStruct((n, d), F32),
        compiler_params=_cparams(("arbitrary",)),
        name="combine",
    )(dest_flat, yb, gw, h, gate, g_final)


def _rope_tables(positions, rot_dim, scale):
    half = rot_dim // 2
    inv_freq = ROPE_THETA ** (-jnp.arange(0, rot_dim, 2, dtype=F32) / rot_dim)
    ang = positions.reshape(-1).astype(F32)[:, None] * inv_freq
    cos, sin = jnp.cos(ang), jnp.sin(ang)
    n = ang.shape[0]
    c = jnp.concatenate([cos, cos, jnp.ones((n, LANES - rot_dim), F32)], axis=1)
    sa = jnp.concatenate([-sin, jnp.zeros((n, LANES - half), F32)], axis=1)
    sb = jnp.concatenate([jnp.zeros((n, half), F32), sin, jnp.zeros((n, LANES - rot_dim), F32)], axis=1)
    t = jnp.stack([c, sa, sb])
    return jnp.stack([t * scale, t])


def _moe(layer, h, u, idx, gw, gate_f, g_final, final, seq, w_gu, b_gu, w_down, b_down):
    n, d = h.shape
    rows = -(-(n * TOP_K + N_EXPERTS * (EXPERT_TM - 1)) // EXPERT_TM) * EXPERT_TM
    n_tiles = rows // EXPERT_TM
    dest, stats = _route_call(idx)
    dest_flat = dest[:, :TOP_K].reshape(-1)
    cnt = stats[1, :N_EXPERTS].astype(jnp.int32)
    tile_end = jnp.cumsum(stats[0, :N_EXPERTS].astype(jnp.int32))
    pend = tile_end * EXPERT_TM
    pstart = jnp.concatenate([jnp.zeros((1,), jnp.int32), pend[:-1]])
    tile_e = jnp.minimum(jnp.searchsorted(tile_end, jnp.arange(n_tiles, dtype=jnp.int32), side='right'),
                         N_EXPERTS - 1).astype(jnp.int32)
    first = jnp.concatenate([jnp.ones((1,), jnp.int32), (tile_e[1:] != tile_e[:-1]).astype(jnp.int32)])
    n_used = tile_end[-1:].astype(jnp.int32)
    xb = _dispatch_call(dest_flat, pstart, cnt, pend, u, rows)
    yb = _expert_call(tile_e, first, n_used, xb, w_gu, b_gu, w_down, b_down, layer)
    return _combine_call(dest_flat, yb, gw, h, gate_f, g_final, seq, final)


def kernel(x, c, positions, ada_w, ada_b, mix_norm_g, ffn_norm_g, final_norm_g, diff_w_in, diff_lambda, diff_subln_g, diff_w_out, mla_w_in, mla_q_norm_g, mla_kv_norm_g, mla_w_uq, mla_w_ukv, mla_w_out, moe_w_router, moe_b_router, moe_w_gate_up, moe_b_gate_up, moe_w_down, moe_b_down):
    batch, seq, d = x.shape
    depth = ada_w.shape[0]
    n = batch * seq
    assert seq % ATT_T == 0 and ATT_T % CHUNK == 0 and d % PAIR == 0
    assert n % DISPATCH_TT == 0 and n % ROUTE_TB == 0 and seq % DIN_TM == 0

    diff_scale = DIFF_HEAD_DIM ** -0.5 * LOG2E
    mla_scale = (MLA_NOPE + MLA_ROPE) ** -0.5 * LOG2E
    tab_d = _rope_tables(positions, DIFF_HEAD_DIM // 4, diff_scale)
    tab_m = _rope_tables(positions, MLA_ROPE, mla_scale)

    mod = _mod_call(c, ada_w, ada_b)
    mod = mod.reshape(depth, batch, N_MOD, 1, d)

    diff_w_in_bf = diff_w_in.astype(BF16)
    diff_w_out_bf = diff_w_out.astype(BF16)
    mla_w_out_bf = mla_w_out.astype(BF16)
    nm = mla_w_in.shape[0]
    heads_m = mla_w_out.shape[1] // MLA_V
    lat_pad = LANES - MLA_ROPE
    mla_w_in_bf = jnp.pad(mla_w_in, ((0, 0), (0, 0), (0, lat_pad))).astype(BF16)
    wq = mla_w_uq.reshape(nm, MLA_Q_RANK, heads_m, MLA_NOPE + MLA_ROPE)
    wq_rope = jnp.pad(wq[..., MLA_NOPE:], ((0, 0), (0, 0), (0, 0), (0, LANES - MLA_ROPE)))
    mla_w_uq_bf = jnp.concatenate([wq[..., :MLA_NOPE].reshape(nm, MLA_Q_RANK, -1),
                                   wq_rope.reshape(nm, MLA_Q_RANK, -1)], axis=-1).astype(BF16)
    wkv = mla_w_ukv.reshape(nm, MLA_KV_RANK, heads_m, MLA_NOPE + MLA_V)
    mla_w_ukv_bf = jnp.concatenate([wkv[..., :MLA_NOPE].reshape(nm, MLA_KV_RANK, -1),
                                    wkv[..., MLA_NOPE:].reshape(nm, MLA_KV_RANK, -1)], axis=-1).astype(BF16)

    h = x.reshape(n, d)
    for i in range(depth):
        shift_a, scale_a, gate_a, shift_f, scale_f, gate_f = [mod[i, :, m] for m in range(N_MOD)]
        g_mix = mix_norm_g[i].reshape(1, d)
        g_ffn = ffn_norm_g[i].reshape(1, d)
        j = i // N_MIXERS
        if i % N_MIXERS == 0:
            lambda_init = 0.8 - 0.6 * math.exp(-0.3 * i)
            qkv = _diff_in_call(h, g_mix, scale_a, shift_a, diff_w_in_bf, j, tab_d, seq)
            o = _diff_attn_call(qkv, diff_lambda[j], diff_subln_g[j].reshape(1, -1), lambda_init, batch, seq)
            w_out = diff_w_out_bf
        else:
            proj, kr = _mla_in_call(h, g_mix, scale_a, shift_a, mla_w_in_bf, mla_q_norm_g[j].reshape(1, -1),
                                    mla_kv_norm_g[j].reshape(1, -1), mla_w_uq_bf, mla_w_ukv_bf, j, tab_m, seq,
                                    mla_scale)
            o = _mla_attn_call(proj, kr, batch, seq)
            w_out = mla_w_out_bf
        wr_hi = moe_w_router[i:i + 1].astype(BF16)
        wr_lo = (moe_w_router[i:i + 1] - wr_hi.astype(F32)).astype(BF16)
        wr3 = jnp.concatenate([wr_hi, wr_hi, wr_lo], axis=1)
        h, u, idx, gw = _attn_out_call(o, w_out, j, h, gate_a, g_ffn, scale_f, shift_f,
                                       wr3, moe_b_router[i:i + 1].reshape(1, 1, -1), seq)
        h = _moe(i, h, u, idx, gw, gate_f, final_norm_g.reshape(1, d), i == depth - 1, seq,
                 moe_w_gate_up, moe_b_gate_up, moe_w_down, moe_b_down)
    return h.reshape(batch, seq, d)
```

```python
import functools
import math

import jax
import jax.numpy as jnp
from jax import lax
from jax.experimental import pallas as pl
from jax.experimental.pallas import tpu as pltpu

F32 = jnp.float32
BF16 = jnp.bfloat16

CHUNK = 64
N_MIXERS = 2
ROPE_THETA = 500000.0
RMS_EPS = 1e-6
N_MOD = 6
DIFF_HEAD_DIM = 128
MLA_V = 128
MLA_Q_RANK = 512
MLA_KV_RANK = 512
MLA_NOPE = 128
MLA_ROPE = 64
N_EXPERTS = 32
TOP_K = 4
SWIGLU_LIMIT = 7.0
SWIGLU_ALPHA = 1.702

LANES = 128
SUBLANES = 8
V7X_VMEM_BYTES = 64 * 1024 * 1024
VMEM_LIMIT = V7X_VMEM_BYTES - 8 * 1024 * 1024

NEG_BIG = -1e30
LOG2E = math.log2(math.e)

ATT_T = 256
PAIR = 256
EXPERT_TM = 576
EXPERT_TF = 512
EXPERT_ROW_BLOCKS = 2
ROUTE_TB = 512
DISPATCH_TT = 512
COMBINE_TT = 256


def _cparams(sem, **kw):
    return pltpu.CompilerParams(dimension_semantics=sem, vmem_limit_bytes=VMEM_LIMIT, **kw)


def _sigmoid(x):
    return 1.0 / (1.0 + jnp.exp(-x))


def _rms(x, g):
    ms = jnp.mean(x * x, axis=-1, keepdims=True)
    return x * lax.rsqrt(ms + RMS_EPS) * g


def _norm_mod(x, g, scale, shift):
    return _rms(x, g) * (1.0 + scale) + shift


def _pack_halves(x):
    w = x.shape[-1] // 2
    lo = lax.bitcast_convert_type(x[:, :w].astype(BF16).astype(F32), jnp.uint32)
    hi = lax.bitcast_convert_type(x[:, w:].astype(BF16).astype(F32), jnp.uint32)
    return (lo >> 16) | (hi & jnp.uint32(0xFFFF0000))


def _unpack_halves(words):
    lo = lax.bitcast_convert_type(words << 16, F32)
    hi = lax.bitcast_convert_type(words & jnp.uint32(0xFFFF0000), F32)
    return lo, hi


def _rope128(x, c, sa, sb, half):
    return x * c + pltpu.roll(x, LANES - half, 1) * sa + pltpu.roll(x, half, 1) * sb


MOD_TN = 512
MOD_KC = 64


def _mod_kernel(cb_ref, w_ref, b_ref, o_ref, cs_ref):
    nb = cb_ref.shape[0]
    d = cb_ref.shape[1]
    nj = MOD_TN // LANES

    @pl.when((pl.program_id(0) == 0) & (pl.program_id(1) == 0))
    def _():
        c = cb_ref[...]
        cs_ref[...] = c * _sigmoid(c)

    def body(kc, accs):
        k0 = pl.multiple_of(kc * MOD_KC, MOD_KC)
        w = w_ref[0, pl.ds(k0, MOD_KC), :]
        out = []
        for b in range(nb):
            cb = cs_ref[b, pl.ds(k0, MOD_KC), :].reshape(MOD_KC // SUBLANES, SUBLANES, LANES)
            row = []
            for jj in range(nj):
                w3 = w[:, jj * LANES:(jj + 1) * LANES].reshape(MOD_KC // SUBLANES, SUBLANES, LANES)
                row.append(accs[b][jj] + jnp.sum(w3 * cb, axis=0))
            out.append(tuple(row))
        return tuple(out)

    zero = jnp.zeros((SUBLANES, LANES), F32)
    init = tuple(tuple(zero for _ in range(nj)) for _ in range(nb))
    accs = lax.fori_loop(0, d // MOD_KC, body, init)
    for b in range(nb):
        for jj in range(nj):
            r = jnp.sum(accs[b][jj], axis=0, keepdims=True)
            o_ref[0, b:b + 1, jj * LANES:(jj + 1) * LANES] = r + b_ref[0, :, jj * LANES:(jj + 1) * LANES]


def _mod_call(c, ada_w, ada_b):
    depth, d, n6 = ada_w.shape
    nb = c.shape[0]
    cb = jnp.broadcast_to(c[:, :, None], (nb, d, LANES))
    return pl.pallas_call(
        _mod_kernel,
        grid=(depth, n6 // MOD_TN),
        in_specs=[
            pl.BlockSpec((nb, d, LANES), lambda l, j: (0, 0, 0)),
            pl.BlockSpec((1, d, MOD_TN), lambda l, j: (l, 0, j)),
            pl.BlockSpec((1, 1, MOD_TN), lambda l, j: (l, 0, j)),
        ],
        out_specs=pl.BlockSpec((1, nb, MOD_TN), lambda l, j: (l, 0, j)),
        out_shape=jax.ShapeDtypeStruct((depth, nb, n6), F32),
        scratch_shapes=[pltpu.VMEM((nb, d, LANES), F32)],
        compiler_params=_cparams(("arbitrary", "arbitrary")),
        name="adaln_mod",
    )(cb, ada_w, ada_b.reshape(depth, 1, n6))


DIN_TM = 512
DIN_TN = 1024


def _diff_in_kernel(h_ref, g_ref, sc_ref, sh_ref, w_ref, tab_ref, o_ref, u_ref, *, nqb):
    j = pl.program_id(1)
    nh = DIN_TN // PAIR

    @pl.when(j == 0)
    def _():
        u_ref[...] = _norm_mod(h_ref[...], g_ref[...], sc_ref[0], sh_ref[0]).astype(BF16)

    def slab(hh):
        return jnp.dot(u_ref[...], w_ref[0, :, hh * PAIR:(hh + 1) * PAIR], preferred_element_type=F32)

    @pl.when(j < 2 * nqb)
    def _():
        c = tab_ref[0, 0]
        sa = tab_ref[0, 1]
        sb = tab_ref[0, 2]
        for hh in range(nh):
            acc = slab(hh)
            for half in range(2):
                y = _rope128(acc[:, half * LANES:(half + 1) * LANES], c, sa, sb, DIFF_HEAD_DIM // 8)
                o_ref[0, hh, :, half * LANES:(half + 1) * LANES] = y.astype(BF16)

    @pl.when(j >= 2 * nqb)
    def _():
        for hh in range(nh):
            o_ref[0, hh] = slab(hh).astype(BF16)


def _diff_in_call(h, g, scale, shift, w_bf, layer, tab, seq):
    n, d = h.shape
    nout = w_bf.shape[2]
    third = nout // 3
    nqb = third // DIN_TN
    heads = third // PAIR
    hpb = DIN_TN // PAIR
    tiles_per_b = seq // DIN_TM
    return pl.pallas_call(
        functools.partial(_diff_in_kernel, nqb=nqb),
        grid=(n // DIN_TM, nout // DIN_TN),
        in_specs=[
            pl.BlockSpec((DIN_TM, d), lambda i, j: (i, 0)),
            pl.BlockSpec((1, d), lambda i, j: (0, 0)),
            pl.BlockSpec((1, 1, d), lambda i, j: (i // tiles_per_b, 0, 0)),
            pl.BlockSpec((1, 1, d), lambda i, j: (i // tiles_per_b, 0, 0)),
            pl.BlockSpec((1, d, DIN_TN), lambda i, j: (layer, 0, j)),
            pl.BlockSpec((1, 3, DIN_TM, LANES), lambda i, j: (jnp.minimum(j // nqb, 1), 0, i, 0)),
        ],
        out_specs=pl.BlockSpec((1, hpb, DIN_TM, PAIR), lambda i, j: (j // nqb, j % nqb, i, 0)),
        out_shape=jax.ShapeDtypeStruct((3, heads, n, PAIR), BF16),
        scratch_shapes=[pltpu.VMEM((DIN_TM, d), BF16)],
        compiler_params=_cparams(("parallel", "arbitrary")),
        name="diff_in",
    )(h, g, scale, shift, w_bf, tab)


def _chunk_mask():
    r = lax.broadcasted_iota(jnp.int32, (ATT_T, ATT_T), 0) // CHUNK
    c = lax.broadcasted_iota(jnp.int32, (ATT_T, ATT_T), 1) // CHUNK
    return c <= r


def _qk(q, k):
    return lax.dot_general(q, k, (((1,), (1,)), ((), ())), preferred_element_type=F32)


def _fold_lanes(x):
    return [x[:, c * LANES:(c + 1) * LANES] for c in range(ATT_T // LANES)]


def _two_pass_softmax(qi, n_chain, score_fn, value_fn, s_ref, acc_ref):
    mask = _chunk_mask()

    def pass1(kt, mxs, masked):
        new = []
        for e in range(n_chain):
            s = score_fn(e, kt)
            if masked:
                s = jnp.where(mask, s, NEG_BIG)
            s_ref[e, kt] = s
            mx = mxs[e]
            for part in _fold_lanes(s):
                mx = jnp.maximum(mx, part)
            new.append(mx)
        return tuple(new)

    mx0 = jnp.full((ATT_T, LANES), NEG_BIG, F32)
    mxs = lax.fori_loop(0, qi, lambda kt, c: pass1(kt, c, False), (mx0,) * n_chain)
    mxs = pass1(qi, mxs, True)
    ms = [jnp.max(mx, axis=-1, keepdims=True) for mx in mxs]
    acc_ref[...] = jnp.zeros_like(acc_ref)

    def pass2(kt, sums):
        new = []
        for e in range(n_chain):
            p = jnp.exp2(s_ref[e, kt] - ms[e])
            sm = sums[e]
            for part in _fold_lanes(p):
                sm = sm + part
            new.append(sm)
            acc_ref[e] += jnp.dot(p.astype(BF16), value_fn(e, kt), preferred_element_type=F32)
        return tuple(new)

    sums = lax.fori_loop(0, qi + 1, pass2, (jnp.zeros((ATT_T, LANES), F32),) * n_chain)
    return [jnp.sum(sm, axis=-1, keepdims=True) for sm in sums]


DIFF_HEADS_PER_STEP = 4


def _diff_attn_kernel(q_ref, k_ref, v_ref, lam_ref, g_ref, o_ref, s_ref, acc_ref, *, lambda_init):
    qi = pl.program_id(1)
    heads = q_ref.shape[1]
    hps = DIFF_HEADS_PER_STEP
    lam = lam_ref[...]
    lam_full = (jnp.exp(jnp.sum(lam[0:1] * lam[1:2], axis=-1, keepdims=True))
                - jnp.exp(jnp.sum(lam[2:3] * lam[3:4], axis=-1, keepdims=True)) + lambda_init)

    def group_body(gi, carry):
        def score(e, kt):
            h = gi * hps + e // 2
            sl = slice((e % 2) * LANES, (e % 2 + 1) * LANES)
            k0 = pl.multiple_of(kt * ATT_T, ATT_T)
            return _qk(q_ref[0, h, :, sl], k_ref[0, h, pl.ds(k0, ATT_T), sl])

        def value(e, kt):
            k0 = pl.multiple_of(kt * ATT_T, ATT_T)
            return v_ref[0, gi * hps + e // 2, pl.ds(k0, ATT_T), :]

        ls = _two_pass_softmax(qi, 2 * hps, score, value, s_ref, acc_ref)
        for n in range(hps):
            o = acc_ref[2 * n] / ls[2 * n] - lam_full * (acc_ref[2 * n + 1] / ls[2 * n + 1])
            o = _rms(o, g_ref[...]) * (1.0 - lambda_init)
            o_ref[gi * hps + n] = o.astype(BF16)
        return carry

    lax.fori_loop(0, heads // hps, group_body, 0)


def _diff_attn_call(qkv, lam, g, lambda_init, batch, seq):
    _, heads, n, _ = qkv.shape
    nq = seq // ATT_T
    kv_spec = lambda which: pl.BlockSpec((1, heads, seq, PAIR), lambda b, qi: (which, 0, b, 0),
                                         pipeline_mode=pl.Buffered(1))
    return pl.pallas_call(
        functools.partial(_diff_attn_kernel, lambda_init=lambda_init),
        grid=(batch, nq),
        in_specs=[
            pl.BlockSpec((1, heads, ATT_T, PAIR), lambda b, qi: (0, 0, b * nq + qi, 0)),
            kv_spec(1),
            kv_spec(2),
            pl.BlockSpec((4, DIFF_HEAD_DIM), lambda b, qi: (0, 0)),
            pl.BlockSpec((1, PAIR), lambda b, qi: (0, 0)),
        ],
        out_specs=pl.BlockSpec((heads, ATT_T, PAIR), lambda b, qi: (0, b * nq + qi, 0)),
        out_shape=jax.ShapeDtypeStruct((heads, n, PAIR), BF16),
        scratch_shapes=[pltpu.VMEM((2 * DIFF_HEADS_PER_STEP, nq, ATT_T, ATT_T), F32),
                        pltpu.VMEM((2 * DIFF_HEADS_PER_STEP, ATT_T, PAIR), F32)],
        compiler_params=_cparams(("parallel", "arbitrary")),
        name="diff_attn",
    )(qkv, qkv, qkv, lam, g)


MLA_TM = 256


def _mla_in_kernel(h_ref, g_ref, sc_ref, sh_ref, win_ref, gq_ref, gkv_ref, wuq_ref, wukv_ref, tab_ref,
                   proj_ref, kr_ref, *, scale):
    u = _norm_mod(h_ref[...], g_ref[...], sc_ref[0], sh_ref[0]).astype(BF16)
    lat = jnp.dot(u, win_ref[0], preferred_element_type=F32)
    qn = _rms(lat[:, 0:MLA_Q_RANK], gq_ref[...]).astype(BF16)
    kvn = _rms(lat[:, MLA_Q_RANK:MLA_Q_RANK + MLA_KV_RANK], gkv_ref[...]).astype(BF16)
    half = MLA_ROPE // 2
    kr = lat[:, MLA_Q_RANK + MLA_KV_RANK:MLA_Q_RANK + MLA_KV_RANK + LANES]
    kr_ref[...] = _rope128(kr, tab_ref[1, 0], tab_ref[1, 1], tab_ref[1, 2], half).astype(BF16)

    npair = proj_ref.shape[1]
    width = npair * PAIR
    q = jnp.dot(qn, wuq_ref[0], preferred_element_type=F32)
    c = tab_ref[0, 0]
    sa = tab_ref[0, 1]
    sb = tab_ref[0, 2]
    for gp in range(npair):
        proj_ref[0, gp] = (q[:, gp * PAIR:(gp + 1) * PAIR] * scale).astype(BF16)
        for a in range(2):
            lo = width + gp * PAIR + a * LANES
            y = _rope128(q[:, lo:lo + LANES], c, sa, sb, half)
            proj_ref[1, gp, :, a * LANES:(a + 1) * LANES] = y.astype(BF16)
    kv = jnp.dot(kvn, wukv_ref[0], preferred_element_type=F32)
    for gp in range(npair):
        proj_ref[2, gp] = kv[:, gp * PAIR:(gp + 1) * PAIR].astype(BF16)
        proj_ref[3, gp] = kv[:, width + gp * PAIR:width + (gp + 1) * PAIR].astype(BF16)


def _mla_in_call(h, g, scale_a, shift_a, w_in, gq, gkv, w_uq, w_ukv, layer, tab, seq, scale):
    n, d = h.shape
    lat_w = w_in.shape[2]
    up_w = w_uq.shape[2]
    npair = up_w // (2 * PAIR)
    tiles_per_b = seq // MLA_TM
    const = lambda shape: pl.BlockSpec(shape, lambda i: (layer,) + (0,) * (len(shape) - 1),
                                       pipeline_mode=pl.Buffered(1))
    return pl.pallas_call(
        functools.partial(_mla_in_kernel, scale=scale),
        grid=(n // MLA_TM,),
        in_specs=[
            pl.BlockSpec((MLA_TM, d), lambda i: (i, 0)),
            pl.BlockSpec((1, d), lambda i: (0, 0)),
            pl.BlockSpec((1, 1, d), lambda i: (i // tiles_per_b, 0, 0)),
            pl.BlockSpec((1, 1, d), lambda i: (i // tiles_per_b, 0, 0)),
            const((1, d, lat_w)),
            pl.BlockSpec((1, MLA_Q_RANK), lambda i: (0, 0)),
            pl.BlockSpec((1, MLA_KV_RANK), lambda i: (0, 0)),
            const((1, MLA_Q_RANK, up_w)),
            const((1, MLA_KV_RANK, up_w)),
            pl.BlockSpec((2, 3, MLA_TM, LANES), lambda i: (0, 0, i, 0)),
        ],
        out_specs=[
            pl.BlockSpec((4, npair, MLA_TM, PAIR), lambda i: (0, 0, i, 0)),
            pl.BlockSpec((MLA_TM, LANES), lambda i: (i, 0)),
        ],
        out_shape=[
            jax.ShapeDtypeStruct((4, npair, n, PAIR), BF16),
            jax.ShapeDtypeStruct((n, LANES), BF16),
        ],
        compiler_params=_cparams(("parallel",)),
        name="mla_in",
    )(h, g, scale_a, shift_a, w_in, gq, gkv, w_uq, w_ukv, tab)


MLA_PAIRS_PER_STEP = 4


def _mla_attn_kernel(q_ref, kv_ref, kr_ref, o_ref, qc_ref, s_ref, acc_ref):
    qi = pl.program_id(1)
    npair = q_ref.shape[1]
    pps = MLA_PAIRS_PER_STEP
    lane_sl = [slice(a * LANES, (a + 1) * LANES) for a in range(2)]

    def group_body(gi, carry):
        for e in range(2 * pps):
            gp = gi * pps + e // 2
            sl = lane_sl[e % 2]
            qc_ref[e] = jnp.concatenate([q_ref[0, gp, :, sl], q_ref[1, gp, :, sl]], axis=-1)

        def score(e, kt):
            k0 = pl.multiple_of(kt * ATT_T, ATT_T)
            kc = jnp.concatenate([kv_ref[0, gi * pps + e // 2, pl.ds(k0, ATT_T), lane_sl[e % 2]],
                                  kr_ref[pl.ds(k0, ATT_T), :]], axis=-1)
            return _qk(qc_ref[e], kc)

        def value(e, kt):
            k0 = pl.multiple_of(kt * ATT_T, ATT_T)
            return kv_ref[1, gi * pps + e // 2, pl.ds(k0, ATT_T), lane_sl[e % 2]]

        ls = _two_pass_softmax(qi, 2 * pps, score, value, s_ref, acc_ref)
        for e in range(2 * pps):
            o_ref[gi * pps + e // 2, :, lane_sl[e % 2]] = (acc_ref[e] / ls[e]).astype(BF16)
        return carry

    lax.fori_loop(0, npair // pps, group_body, 0)


def _mla_attn_call(proj, kr, batch, seq):
    _, npair, n, _ = proj.shape
    nq = seq // ATT_T
    return pl.pallas_call(
        _mla_attn_kernel,
        grid=(batch, nq),
        in_specs=[
            pl.BlockSpec((2, npair, ATT_T, PAIR), lambda b, qi: (0, 0, b * nq + qi, 0)),
            pl.BlockSpec((2, npair, seq, PAIR), lambda b, qi: (1, 0, b, 0), pipeline_mode=pl.Buffered(1)),
            pl.BlockSpec((seq, LANES), lambda b, qi: (b, 0), pipeline_mode=pl.Buffered(1)),
        ],
        out_specs=pl.BlockSpec((npair, ATT_T, PAIR), lambda b, qi: (0, b * nq + qi, 0)),
        out_shape=jax.ShapeDtypeStruct((npair, n, PAIR), BF16),
        scratch_shapes=[pltpu.VMEM((2 * MLA_PAIRS_PER_STEP, ATT_T, PAIR), BF16),
                        pltpu.VMEM((2 * MLA_PAIRS_PER_STEP, nq, ATT_T, ATT_T), F32),
                        pltpu.VMEM((2 * MLA_PAIRS_PER_STEP, ATT_T, LANES), F32)],
        compiler_params=_cparams(("parallel", "arbitrary")),
        name="mla_attn",
    )(proj, proj, kr)


AOUT_TM = 512
AOUT_ROW_BLOCKS = 2


def _attn_out_kernel(o_ref, w_ref, h_ref, gate_ref, g_ref, sc_ref, sh_ref, wr_ref, br_ref,
                     hn_ref, u_ref, idx_ref, gw_ref):
    ns = o_ref.shape[0]
    tm = h_ref.shape[0] // AOUT_ROW_BLOCKS
    for r in range(AOUT_ROW_BLOCKS):
        rs = slice(r * tm, (r + 1) * tm)
        y = jnp.dot(o_ref[0, rs, :], w_ref[0, 0:PAIR, :], preferred_element_type=F32)
        for s in range(1, ns):
            y = y + jnp.dot(o_ref[s, rs, :], w_ref[0, s * PAIR:(s + 1) * PAIR, :], preferred_element_type=F32)
        hn = h_ref[rs, :] + gate_ref[0] * y
        hn_ref[rs, :] = hn
        u = _norm_mod(hn, g_ref[...], sc_ref[0], sh_ref[0])
        u_ref[rs, :] = _pack_halves(u)
        u_hi = u.astype(BF16)
        u_lo = (u - u_hi.astype(F32)).astype(BF16)
        logits = jnp.dot(jnp.concatenate([u_hi, u_lo, u_hi], axis=-1), wr_ref[0],
                         preferred_element_type=F32) + br_ref[0]
        ne = logits.shape[1]
        eio = lax.broadcasted_iota(jnp.int32, (tm, ne), 1)
        lane = lax.broadcasted_iota(jnp.int32, (tm, LANES), 1)
        idx_out = jnp.zeros((tm, LANES), jnp.int32)
        val_out = jnp.zeros((tm, LANES), F32)
        l = logits
        v0 = None
        den = None
        for k in range(TOP_K):
            m = jnp.max(l, axis=-1, keepdims=True)
            i = jnp.min(jnp.where(l == m, eio, ne), axis=-1, keepdims=True)
            l = jnp.where(eio == i, -jnp.inf, l)
            if k == 0:
                v0 = m
            e = jnp.exp(m - v0)
            den = e if den is None else den + e
            idx_out = jnp.where(lane == k, i, idx_out)
            val_out = jnp.where(lane == k, e, val_out)
        idx_ref[rs, :] = idx_out
        gw_ref[rs, :] = val_out / den


def _attn_out_call(o, w_out, layer, h, gate, g, scale, shift, w_router, b_router, seq):
    ns, n, _ = o.shape
    d = h.shape[1]
    ne = w_router.shape[2]
    tiles_per_b = seq // AOUT_TM
    perb = lambda: pl.BlockSpec((1, 1, d), lambda i: (i // tiles_per_b, 0, 0))
    return pl.pallas_call(
        _attn_out_kernel,
        grid=(n // AOUT_TM,),
        in_specs=[
            pl.BlockSpec((ns, AOUT_TM, PAIR), lambda i: (0, i, 0)),
            pl.BlockSpec((1, d, d), lambda i: (layer, 0, 0), pipeline_mode=pl.Buffered(1)),
            pl.BlockSpec((AOUT_TM, d), lambda i: (i, 0)),
            perb(),
            pl.BlockSpec((1, d), lambda i: (0, 0)),
            perb(),
            perb(),
            pl.BlockSpec((1, 3 * d, ne), lambda i: (0, 0, 0)),
            pl.BlockSpec((1, 1, ne), lambda i: (0, 0, 0)),
        ],
        out_specs=[
            pl.BlockSpec((AOUT_TM, d), lambda i: (i, 0)),
            pl.BlockSpec((AOUT_TM, d // 2), lambda i: (i, 0)),
            pl.BlockSpec((AOUT_TM, LANES), lambda i: (i, 0)),
            pl.BlockSpec((AOUT_TM, LANES), lambda i: (i, 0)),
        ],
        out_shape=[
            jax.ShapeDtypeStruct((n, d), F32),
            jax.ShapeDtypeStruct((n, d // 2), jnp.uint32),
            jax.ShapeDtypeStruct((n, LANES), jnp.int32),
            jax.ShapeDtypeStruct((n, LANES), F32),
        ],
        compiler_params=_cparams(("parallel",)),
        name="attn_out",
    )(o, w_out, h, gate, g, scale, shift, w_router, b_router)


def _route_kernel(idx_ref, dest_ref, pt_ref, carry_ref, ps_ref):
    ph = pl.program_id(0)
    t = pl.program_id(1)
    nt = pl.num_programs(1)
    tb = idx_ref.shape[0]
    idx = idx_ref[...]
    eio = lax.broadcasted_iota(jnp.int32, (tb, LANES), 1)
    sel = [eio == idx[:, k:k + 1] for k in range(TOP_K)]
    oh = jnp.where(sel[0], 1.0, 0.0)
    for k in range(1, TOP_K):
        oh = oh + jnp.where(sel[k], 1.0, 0.0)

    @pl.when(t == 0)
    def _():
        carry_ref[...] = jnp.zeros_like(carry_ref)

    @pl.when((ph == 0) & (t == nt - 1))
    def _():
        cnt = carry_ref[...] + jnp.sum(oh, axis=0, keepdims=True)
        tiles = jnp.floor((cnt + (EXPERT_TM - 0.5)) * (1.0 / EXPERT_TM))
        r = lax.broadcasted_iota(jnp.int32, (LANES, LANES), 0)
        c = lax.broadcasted_iota(jnp.int32, (LANES, LANES), 1)
        upper = jnp.where(r < c, 1.0, 0.0).astype(BF16)
        t8 = jnp.broadcast_to(tiles, (SUBLANES, LANES))
        start = jnp.dot(t8.astype(BF16), upper, preferred_element_type=F32)
        ps_ref[...] = start[0:1] * EXPERT_TM
        row = lax.broadcasted_iota(jnp.int32, (SUBLANES, LANES), 0)
        pt_ref[...] = jnp.where(row == 0, t8, jnp.broadcast_to(cnt, (SUBLANES, LANES)))

    @pl.when(ph == 1)
    def _():
        r = lax.broadcasted_iota(jnp.int32, (tb, tb), 0)
        c = lax.broadcasted_iota(jnp.int32, (tb, tb), 1)
        lower = jnp.where(c < r, 1.0, 0.0).astype(BF16)
        before = jnp.dot(lower, oh.astype(BF16), preferred_element_type=F32)
        base = before + carry_ref[...] + ps_ref[...]
        out = jnp.zeros((tb, LANES), jnp.int32)
        for k in range(TOP_K):
            dk = jnp.sum(jnp.where(sel[k], base, 0.0), axis=-1, keepdims=True)
            out = jnp.where(eio == k, dk.astype(jnp.int32), out)
        dest_ref[...] = out

    carry_ref[...] += jnp.sum(oh, axis=0, keepdims=True)


def _route_call(idx):
    n = idx.shape[0]
    return pl.pallas_call(
        _route_kernel,
        grid=(2, n // ROUTE_TB),
        in_specs=[pl.BlockSpec((ROUTE_TB, LANES), lambda p, t: (t, 0))],
        out_specs=[
            pl.BlockSpec((ROUTE_TB, LANES), lambda p, t: (p * t, 0)),
            pl.BlockSpec((SUBLANES, LANES), lambda p, t: (0, 0)),
        ],
        out_shape=[
            jax.ShapeDtypeStruct((n, LANES), jnp.int32),
            jax.ShapeDtypeStruct((SUBLANES, LANES), F32),
        ],
        scratch_shapes=[pltpu.VMEM((1, LANES), F32), pltpu.VMEM((1, LANES), F32)],
        compiler_params=_cparams(("arbitrary", "arbitrary")),
        name="route",
    )(idx)


DISPATCH_ZROWS = 512
DISPATCH_TAIL = 64
ROW_DMA_UNROLL = 4
assert DISPATCH_ZROWS <= EXPERT_TM < 2 * DISPATCH_ZROWS and EXPERT_TM % DISPATCH_TAIL == 0


def _dispatch_kernel(dest_ref, pstart_ref, cnt_ref, pend_ref, u_ref, xb_ref, zbuf_ref, sem, zsem):
    i = pl.program_id(0)
    tt = u_ref.shape[0]
    rows = xb_ref.shape[0]
    ne = pstart_ref.shape[0]

    def zero_copy(pos, nrows):
        return pltpu.make_async_copy(zbuf_ref.at[pl.ds(0, nrows)], xb_ref.at[pl.ds(pos, nrows)], zsem)

    def pad_fill(act):
        def expert_body(e, carry):
            pos = pstart_ref[e] + cnt_ref[e]
            npad = pend_ref[e] - pos
            head = jnp.minimum(npad, (-pos) & (SUBLANES - 1))

            def head_body(r, c):
                act(zero_copy(pos + r, 1))
                return c

            lax.fori_loop(0, head, head_body, 0)
            base = pos + head
            rem = npad - head
            bit = DISPATCH_ZROWS
            while bit >= SUBLANES:
                off = rem - (rem & (2 * bit - 1))

                @pl.when((rem & bit) != 0)
                def _(off=off, bit=bit):
                    act(zero_copy(pl.multiple_of(base + off, SUBLANES), bit))

                bit //= 2
            return carry

        lax.fori_loop(0, ne, expert_body, 0)
        tail0 = pend_ref[ne - 1]

        def tail_body(b, c):
            act(zero_copy(pl.multiple_of(tail0 + b * DISPATCH_TAIL, DISPATCH_TAIL), DISPATCH_TAIL))
            return c

        lax.fori_loop(0, (rows - tail0) // DISPATCH_TAIL, tail_body, 0)

    @pl.when(i == 0)
    def _():
        zbuf_ref[...] = jnp.zeros_like(zbuf_ref)
        pad_fill(lambda c: c.start())

    def row_copy(t, d):
        return pltpu.make_async_copy(u_ref.at[pl.ds(t, 1)], xb_ref.at[pl.ds(d, 1)], sem)

    def issue(t, carry):
        for k in range(TOP_K):
            row_copy(t, dest_ref[(i * tt + t) * TOP_K + k]).start()
        return carry

    lax.fori_loop(0, tt, issue, 0, unroll=ROW_DMA_UNROLL)
    pltpu.make_async_copy(xb_ref.at[pl.ds(0, TOP_K * tt)], xb_ref.at[pl.ds(0, TOP_K * tt)], sem).wait()

    @pl.when(i == 0)
    def _():
        pad_fill(lambda c: c.wait())


def _dispatch_call(dest_flat, pstart, cnt, pend, u, rows):
    n, d = u.shape
    return pl.pallas_call(
        _dispatch_kernel,
        grid_spec=pltpu.PrefetchScalarGridSpec(
            num_scalar_prefetch=4,
            grid=(n // DISPATCH_TT,),
            in_specs=[pl.BlockSpec((DISPATCH_TT, d), lambda i, *_: (i, 0))],
            out_specs=pl.BlockSpec(memory_space=pl.ANY),
            scratch_shapes=[pltpu.VMEM((DISPATCH_ZROWS, d), u.dtype),
                            pltpu.SemaphoreType.DMA(()), pltpu.SemaphoreType.DMA(())],
        ),
        out_shape=jax.ShapeDtypeStruct((rows, d), u.dtype),
        compiler_params=_cparams(("arbitrary",), has_side_effects=True),
        name="dispatch",
    )(dest_flat, pstart, cnt, pend, u)


def _expert_kernel(te_ref, first_ref, nu_ref, x_ref, wg_ref, wu_ref, bg_ref, bu_ref, wd_ref, bd_ref, o_ref,
                   xbf_ref, wgb_ref, wub_ref, wdb_ref, act_ref):
    del te_ref
    i = pl.program_id(0)
    j = pl.program_id(1)
    nf = act_ref.shape[0]
    half = x_ref.shape[1]

    @pl.when(i < nu_ref[0])
    def _():
        @pl.when(j == 0)
        def _():
            lo, hi = _unpack_halves(x_ref[...])
            xbf_ref[:, 0:half] = lo.astype(BF16)
            xbf_ref[:, half:2 * half] = hi.astype(BF16)

        rb = x_ref.shape[0] // EXPERT_ROW_BLOCKS

        def hidden_chunk(wg, wu):
            for r in range(EXPERT_ROW_BLOCKS):
                rs = slice(r * rb, (r + 1) * rb)
                x = xbf_ref[rs, :]
                gate = jnp.dot(x, wg, preferred_element_type=F32) + bg_ref[0, 0]
                up = jnp.dot(x, wu, preferred_element_type=F32) + bu_ref[0, 0]
                gate = jnp.minimum(gate, SWIGLU_LIMIT)
                up = jnp.clip(up, -SWIGLU_LIMIT, SWIGLU_LIMIT)
                glu = gate * _sigmoid(gate * SWIGLU_ALPHA)
                act_ref[j, rs, :] = ((up + 1.0) * glu).astype(BF16)

        @pl.when(first_ref[i] == 1)
        def _():
            wg = wg_ref[0, 0].astype(BF16)
            wu = wu_ref[0, 0].astype(BF16)
            wgb_ref[j] = wg
            wub_ref[j] = wu
            wdb_ref[j] = wd_ref[0, 0].astype(BF16)
            hidden_chunk(wg, wu)

        @pl.when(first_ref[i] != 1)
        def _():
            hidden_chunk(wgb_ref[j], wub_ref[j])

        @pl.when(j == nf - 1)
        def _():
            for r in range(EXPERT_ROW_BLOCKS):
                rs = slice(r * rb, (r + 1) * rb)
                y = jnp.dot(act_ref[0, rs, :], wdb_ref[0], preferred_element_type=F32)
                for c in range(1, nf):
                    y = y + jnp.dot(act_ref[c, rs, :], wdb_ref[c], preferred_element_type=F32)
                o_ref[rs, :] = _pack_halves(y + bd_ref[0, 0])

    @pl.when((i >= nu_ref[0]) & (j == 0))
    def _():
        o_ref[...] = jnp.zeros_like(o_ref)


def _expert_call(tile_e, first, n_used, xb, w_gu, b_gu, w_down, b_down, layer):
    rows, dw = xb.shape
    d = 2 * dw
    ne = w_gu.shape[1]
    f = w_down.shape[2]
    nf = f // EXPERT_TF
    n_tiles = rows // EXPERT_TM

    def tile(i, nu):
        return jnp.minimum(i, nu[0] - 1)

    def fcol(i, j, nu):
        return jnp.where(i < nu[0], j, nf - 1)

    def wcol(i, j, fi, nu):
        return jnp.where((i < nu[0]) & (fi[tile(i, nu)] == 1), j, nf - 1)

    return pl.pallas_call(
        _expert_kernel,
        grid_spec=pltpu.PrefetchScalarGridSpec(
            num_scalar_prefetch=3,
            grid=(n_tiles, nf),
            in_specs=[
                pl.BlockSpec((EXPERT_TM, dw), lambda i, j, te, fi, nu: (tile(i, nu), 0)),
                pl.BlockSpec((1, 1, d, EXPERT_TF),
                             lambda i, j, te, fi, nu: (layer, te[tile(i, nu)], 0, wcol(i, j, fi, nu))),
                pl.BlockSpec((1, 1, d, EXPERT_TF),
                             lambda i, j, te, fi, nu: (layer, te[tile(i, nu)], 0, nf + wcol(i, j, fi, nu))),
                pl.BlockSpec((1, 1, 1, EXPERT_TF),
                             lambda i, j, te, fi, nu: (layer, te[tile(i, nu)], 0, fcol(i, j, nu))),
                pl.BlockSpec((1, 1, 1, EXPERT_TF),
                             lambda i, j, te, fi, nu: (layer, te[tile(i, nu)], 0, nf + fcol(i, j, nu))),
                pl.BlockSpec((1, 1, EXPERT_TF, d),
                             lambda i, j, te, fi, nu: (layer, te[tile(i, nu)], wcol(i, j, fi, nu), 0)),
                pl.BlockSpec((1, 1, 1, d), lambda i, j, te, fi, nu: (layer, te[tile(i, nu)], 0, 0)),
            ],
            out_specs=pl.BlockSpec((EXPERT_TM, dw), lambda i, j, te, fi, nu: (i, 0)),
            scratch_shapes=[pltpu.VMEM((EXPERT_TM, d), BF16),
                            pltpu.VMEM((nf, d, EXPERT_TF), BF16),
                            pltpu.VMEM((nf, d, EXPERT_TF), BF16),
                            pltpu.VMEM((nf, EXPERT_TF, d), BF16),
                            pltpu.VMEM((nf, EXPERT_TM, EXPERT_TF), BF16)],
        ),
        out_shape=jax.ShapeDtypeStruct((rows, dw), jnp.uint32),
        compiler_params=_cparams(("arbitrary", "arbitrary")),
        name="experts",
    )(tile_e, first, n_used, xb, w_gu, w_gu, b_gu.reshape(b_gu.shape[0], ne, 1, 2 * f),
      b_gu.reshape(b_gu.shape[0], ne, 1, 2 * f), w_down, b_down.reshape(b_down.shape[0], ne, 1, d))


def _combine_kernel(dest_ref, yb_ref, gw_ref, h_ref, gate_ref, g_ref, o_ref, buf_ref, sem, *, final):
    i = pl.program_id(0)
    nsteps = pl.num_programs(0)
    tt = h_ref.shape[0]
    slot = i % 2

    def issue(tile, sl):
        def body(t, carry):
            for k in range(TOP_K):
                d = dest_ref[(tile * tt + t) * TOP_K + k]
                pltpu.make_async_copy(yb_ref.at[pl.ds(d, 1)], buf_ref.at[sl, pl.ds(k * tt + t, 1)],
                                      sem.at[sl]).start()
            return carry
        lax.fori_loop(0, tt, body, 0, unroll=ROW_DMA_UNROLL)

    @pl.when(i == 0)
    def _():
        issue(0, 0)

    @pl.when(i + 1 < nsteps)
    def _():
        issue(i + 1, 1 - slot)

    pltpu.make_async_copy(yb_ref.at[pl.ds(0, TOP_K * tt)], buf_ref.at[slot], sem.at[slot]).wait()
    gw = gw_ref[...]
    y_lo, y_hi = None, None
    for k in range(TOP_K):
        lo, hi = _unpack_halves(buf_ref[slot, k * tt:(k + 1) * tt])
        wk = gw[:, k:k + 1]
        y_lo = wk * lo if y_lo is None else y_lo + wk * lo
        y_hi = wk * hi if y_hi is None else y_hi + wk * hi
    hn = h_ref[...] + gate_ref[0] * jnp.concatenate([y_lo, y_hi], axis=-1)
    if final:
        hn = _rms(hn, g_ref[...])
    o_ref[...] = hn


def _combine_call(dest_flat, yb, gw, h, gate, g_final, seq, final):
    n, d = h.shape
    tiles_per_b = seq // COMBINE_TT
    return pl.pallas_call(
        functools.partial(_combine_kernel, final=final),
        grid_spec=pltpu.PrefetchScalarGridSpec(
            num_scalar_prefetch=1,
            grid=(n // COMBINE_TT,),
            in_specs=[
                pl.BlockSpec(memory_space=pl.ANY),
                pl.BlockSpec((COMBINE_TT, LANES), lambda i, dest: (i, 0)),
                pl.BlockSpec((COMBINE_TT, d), lambda i, dest: (i, 0)),
                pl.BlockSpec((1, 1, d), lambda i, dest: (i // tiles_per_b, 0, 0)),
                pl.BlockSpec((1, d), lambda i, dest: (0, 0)),
            ],
            out_specs=pl.BlockSpec((COMBINE_TT, d), lambda i, dest: (i, 0)),
            scratch_shapes=[pltpu.VMEM((2, TOP_K * COMBINE_TT, d // 2), jnp.uint32),
                            pltpu.SemaphoreType.DMA((2,))],
        ),
        out_shape=jax.ShapeDtypeStruct((n, d), F32),
        compiler_params=_cparams(("arbitrary",)),
        name="combine",
    )(dest_flat, yb, gw, h, gate, g_final)


def _rope_tables(positions, rot_dim, scale):
    half = rot_dim // 2
    inv_freq = ROPE_THETA ** (-jnp.arange(0, rot_dim, 2, dtype=F32) / rot_dim)
    ang = positions.reshape(-1).astype(F32)[:, None] * inv_freq
    cos, sin = jnp.cos(ang), jnp.sin(ang)
    n = ang.shape[0]
    c = jnp.concatenate([cos, cos, jnp.ones((n, LANES - rot_dim), F32)], axis=1)
    sa = jnp.concatenate([-sin, jnp.zeros((n, LANES - half), F32)], axis=1)
    sb = jnp.concatenate([jnp.zeros((n, half), F32), sin, jnp.zeros((n, LANES - rot_dim), F32)], axis=1)
    t = jnp.stack([c, sa, sb])
    return jnp.stack([t * scale, t])


def _moe(layer, h, u, idx, gw, gate_f, g_final, final, seq, w_gu, b_gu, w_down, b_down):
    n, d = h.shape
    rows = -(-(n * TOP_K + N_EXPERTS * (EXPERT_TM - 1)) // EXPERT_TM) * EXPERT_TM
    n_tiles = rows // EXPERT_TM
    dest, stats = _route_call(idx)
    dest_flat = dest[:, :TOP_K].reshape(-1)
    cnt = stats[1, :N_EXPERTS].astype(jnp.int32)
    tile_end = jnp.cumsum(stats[0, :N_EXPERTS].astype(jnp.int32))
    pend = tile_end * EXPERT_TM
    pstart = jnp.concatenate([jnp.zeros((1,), jnp.int32), pend[:-1]])
    tile_e = jnp.minimum(jnp.searchsorted(tile_end, jnp.arange(n_tiles, dtype=jnp.int32), side='right'),
                         N_EXPERTS - 1).astype(jnp.int32)
    first = jnp.concatenate([jnp.ones((1,), jnp.int32), (tile_e[1:] != tile_e[:-1]).astype(jnp.int32)])
    n_used = tile_end[-1:].astype(jnp.int32)
    xb = _dispatch_call(dest_flat, pstart, cnt, pend, u, rows)
    yb = _expert_call(tile_e, first, n_used, xb, w_gu, b_gu, w_down, b_down, layer)
    return _combine_call(dest_flat, yb, gw, h, gate_f, g_final, seq, final)


def kernel(x, c, positions, ada_w, ada_b, mix_norm_g, ffn_norm_g, final_norm_g, diff_w_in, diff_lambda, diff_subln_g, diff_w_out, mla_w_in, mla_q_norm_g, mla_kv_norm_g, mla_w_uq, mla_w_ukv, mla_w_out, moe_w_router, moe_b_router, moe_w_gate_up, moe_b_gate_up, moe_w_down, moe_b_down):
    batch, seq, d = x.shape
    depth = ada_w.shape[0]
    n = batch * seq
    assert seq % ATT_T == 0 and ATT_T % CHUNK == 0 and d % PAIR == 0
    assert n % DISPATCH_TT == 0 and n % ROUTE_TB == 0 and seq % DIN_TM == 0

    diff_scale = DIFF_HEAD_DIM ** -0.5 * LOG2E
    mla_scale = (MLA_NOPE + MLA_ROPE) ** -0.5 * LOG2E
    tab_d = _rope_tables(positions, DIFF_HEAD_DIM // 4, diff_scale)
    tab_m = _rope_tables(positions, MLA_ROPE, mla_scale)

    mod = _mod_call(c, ada_w, ada_b)
    mod = mod.reshape(depth, batch, N_MOD, 1, d)

    diff_w_in_bf = diff_w_in.astype(BF16)
    diff_w_out_bf = diff_w_out.astype(BF16)
    mla_w_out_bf = mla_w_out.astype(BF16)
    nm = mla_w_in.shape[0]
    heads_m = mla_w_out.shape[1] // MLA_V
    lat_pad = LANES - MLA_ROPE
    mla_w_in_bf = jnp.pad(mla_w_in, ((0, 0), (0, 0), (0, lat_pad))).astype(BF16)
    wq = mla_w_uq.reshape(nm, MLA_Q_RANK, heads_m, MLA_NOPE + MLA_ROPE)
    wq_rope = jnp.pad(wq[..., MLA_NOPE:], ((0, 0), (0, 0), (0, 0), (0, LANES - MLA_ROPE)))
    mla_w_uq_bf = jnp.concatenate([wq[..., :MLA_NOPE].reshape(nm, MLA_Q_RANK, -1),
                                   wq_rope.reshape(nm, MLA_Q_RANK, -1)], axis=-1).astype(BF16)
    wkv = mla_w_ukv.reshape(nm, MLA_KV_RANK, heads_m, MLA_NOPE + MLA_V)
    mla_w_ukv_bf = jnp.concatenate([wkv[..., :MLA_NOPE].reshape(nm, MLA_KV_RANK, -1),
                                    wkv[..., MLA_NOPE:].reshape(nm, MLA_KV_RANK, -1)], axis=-1).astype(BF16)

    h = x.reshape(n, d)
    for i in range(depth):
        shift_a, scale_a, gate_a, shift_f, scale_f, gate_f = [mod[i, :, m] for m in range(N_MOD)]
        g_mix = mix_norm_g[i].reshape(1, d)
        g_ffn = ffn_norm_g[i].reshape(1, d)
        j = i // N_MIXERS
        if i % N_MIXERS == 0:
            lambda_init = 0.8 - 0.6 * math.exp(-0.3 * i)
            qkv = _diff_in_call(h, g_mix, scale_a, shift_a, diff_w_in_bf, j, tab_d, seq)
            o = _diff_attn_call(qkv, diff_lambda[j], diff_subln_g[j].reshape(1, -1), lambda_init, batch, seq)
            w_out = diff_w_out_bf
        else:
            proj, kr = _mla_in_call(h, g_mix, scale_a, shift_a, mla_w_in_bf, mla_q_norm_g[j].reshape(1, -1),
                                    mla_kv_norm_g[j].reshape(1, -1), mla_w_uq_bf, mla_w_ukv_bf, j, tab_m, seq,
                                    mla_scale)
            o = _mla_attn_call(proj, kr, batch, seq)
            w_out = mla_w_out_bf
        wr_hi = moe_w_router[i:i + 1].astype(BF16)
        wr_lo = (moe_w_router[i:i + 1] - wr_hi.astype(F32)).astype(BF16)
        wr3 = jnp.concatenate([wr_hi, wr_hi, wr_lo], axis=1)
        h, u, idx, gw = _attn_out_call(o, w_out, j, h, gate_a, g_ffn, scale_f, shift_f,
                                       wr3, moe_b_router[i:i + 1].reshape(1, 1, -1), seq)
        h = _moe(i, h, u, idx, gw, gate_f, final_norm_g.reshape(1, d), i == depth - 1, seq,
                 moe_w_gate_up, moe_b_gate_up, moe_w_down, moe_b_down)
    return h.reshape(batch, seq, d)
```

```python
import functools
import math

import jax
import jax.numpy as jnp
from jax import lax
from jax.experimental import pallas as pl
from jax.experimental.pallas import tpu as pltpu

F32 = jnp.float32
BF16 = jnp.bfloat16

CHUNK = 64
N_MIXERS = 2
ROPE_THETA = 500000.0
RMS_EPS = 1e-6
N_MOD = 6
DIFF_HEAD_DIM = 128
MLA_V = 128
MLA_Q_RANK = 512
MLA_KV_RANK = 512
MLA_NOPE = 128
MLA_ROPE = 64
N_EXPERTS = 32
TOP_K = 4
SWIGLU_LIMIT = 7.0
SWIGLU_ALPHA = 1.702

LANES = 128
SUBLANES = 8
V7X_VMEM_BYTES = 64 * 1024 * 1024
VMEM_LIMIT = V7X_VMEM_BYTES - 8 * 1024 * 1024

NEG_BIG = -1e30
LOG2E = math.log2(math.e)

ATT_T = 256
PAIR = 256
EXPERT_TM = 576
EXPERT_TF = 512
EXPERT_ROW_BLOCKS = 2
ROUTE_TB = 512
DISPATCH_TT = 512
COMBINE_TT = 256


def _cparams(sem, **kw):
    return pltpu.CompilerParams(dimension_semantics=sem, vmem_limit_bytes=VMEM_LIMIT, **kw)


def _sigmoid(x):
    return 1.0 / (1.0 + jnp.exp(-x))


def _rms(x, g):
    ms = jnp.mean(x * x, axis=-1, keepdims=True)
    return x * lax.rsqrt(ms + RMS_EPS) * g


def _norm_mod(x, g, scale, shift):
    return _rms(x, g) * (1.0 + scale) + shift


def _pack_halves(x):
    w = x.shape[-1] // 2
    lo = lax.bitcast_convert_type(x[:, :w].astype(BF16).astype(F32), jnp.uint32)
    hi = lax.bitcast_convert_type(x[:, w:].astype(BF16).astype(F32), jnp.uint32)
    return (lo >> 16) | (hi & jnp.uint32(0xFFFF0000))


def _unpack_halves(words):
    lo = lax.bitcast_convert_type(words << 16, F32)
    hi = lax.bitcast_convert_type(words & jnp.uint32(0xFFFF0000), F32)
    return lo, hi


def _rope128(x, c, sa, sb, half):
    return x * c + pltpu.roll(x, LANES - half, 1) * sa + pltpu.roll(x, half, 1) * sb


MOD_TN = 512
MOD_KC = 64


def _mod_kernel(cb_ref, w_ref, b_ref, o_ref, cs_ref):
    nb = cb_ref.shape[0]
    d = cb_ref.shape[1]
    nj = MOD_TN // LANES

    @pl.when((pl.program_id(0) == 0) & (pl.program_id(1) == 0))
    def _():
        c = cb_ref[...]
        cs_ref[...] = c * _sigmoid(c)

    def body(kc, accs):
        k0 = pl.multiple_of(kc * MOD_KC, MOD_KC)
        w = w_ref[0, pl.ds(k0, MOD_KC), :]
        out = []
        for b in range(nb):
            cb = cs_ref[b, pl.ds(k0, MOD_KC), :].reshape(MOD_KC // SUBLANES, SUBLANES, LANES)
            row = []
            for jj in range(nj):
                w3 = w[:, jj * LANES:(jj + 1) * LANES].reshape(MOD_KC // SUBLANES, SUBLANES, LANES)
                row.append(accs[b][jj] + jnp.sum(w3 * cb, axis=0))
            out.append(tuple(row))
        return tuple(out)

    zero = jnp.zeros((SUBLANES, LANES), F32)
    init = tuple(tuple(zero for _ in range(nj)) for _ in range(nb))
    accs = lax.fori_loop(0, d // MOD_KC, body, init)
    for b in range(nb):
        for jj in range(nj):
            r = jnp.sum(accs[b][jj], axis=0, keepdims=True)
            o_ref[0, b:b + 1, jj * LANES:(jj + 1) * LANES] = r + b_ref[0, :, jj * LANES:(jj + 1) * LANES]


def _mod_call(c, ada_w, ada_b):
    depth, d, n6 = ada_w.shape
    nb = c.shape[0]
    cb = jnp.broadcast_to(c[:, :, None], (nb, d, LANES))
    return pl.pallas_call(
        _mod_kernel,
        grid=(depth, n6 // MOD_TN),
        in_specs=[
            pl.BlockSpec((nb, d, LANES), lambda l, j: (0, 0, 0)),
            pl.BlockSpec((1, d, MOD_TN), lambda l, j: (l, 0, j)),
            pl.BlockSpec((1, 1, MOD_TN), lambda l, j: (l, 0, j)),
        ],
        out_specs=pl.BlockSpec((1, nb, MOD_TN), lambda l, j: (l, 0, j)),
        out_shape=jax.ShapeDtypeStruct((depth, nb, n6), F32),
        scratch_shapes=[pltpu.VMEM((nb, d, LANES), F32)],
        compiler_params=_cparams(("arbitrary", "arbitrary")),
        name="adaln_mod",
    )(cb, ada_w, ada_b.reshape(depth, 1, n6))


DIN_TM = 512
DIN_TN = 1024


def _diff_in_kernel(h_ref, g_ref, sc_ref, sh_ref, w_ref, tab_ref, o_ref, u_ref, *, nqb):
    j = pl.program_id(1)
    nh = DIN_TN // PAIR

    @pl.when(j == 0)
    def _():
        u_ref[...] = _norm_mod(h_ref[...], g_ref[...], sc_ref[0], sh_ref[0]).astype(BF16)

    def slab(hh):
        return jnp.dot(u_ref[...], w_ref[0, :, hh * PAIR:(hh + 1) * PAIR], preferred_element_type=F32)

    @pl.when(j < 2 * nqb)
    def _():
        c = tab_ref[0, 0]
        sa = tab_ref[0, 1]
        sb = tab_ref[0, 2]
        for hh in range(nh):
            acc = slab(hh)
            for half in range(2):
                y = _rope128(acc[:, half * LANES:(half + 1) * LANES], c, sa, sb, DIFF_HEAD_DIM // 8)
                o_ref[0, hh, :, half * LANES:(half + 1) * LANES] = y.astype(BF16)

    @pl.when(j >= 2 * nqb)
    def _():
        for hh in range(nh):
            o_ref[0, hh] = slab(hh).astype(BF16)


def _diff_in_call(h, g, scale, shift, w_bf, layer, tab, seq):
    n, d = h.shape
    nout = w_bf.shape[2]
    third = nout // 3
    nqb = third // DIN_TN
    heads = third // PAIR
    hpb = DIN_TN // PAIR
    tiles_per_b = seq // DIN_TM
    return pl.pallas_call(
        functools.partial(_diff_in_kernel, nqb=nqb),
        grid=(n // DIN_TM, nout // DIN_TN),
        in_specs=[
            pl.BlockSpec((DIN_TM, d), lambda i, j: (i, 0)),
            pl.BlockSpec((1, d), lambda i, j: (0, 0)),
            pl.BlockSpec((1, 1, d), lambda i, j: (i // tiles_per_b, 0, 0)),
            pl.BlockSpec((1, 1, d), lambda i, j: (i // tiles_per_b, 0, 0)),
            pl.BlockSpec((1, d, DIN_TN), lambda i, j: (layer, 0, j)),
            pl.BlockSpec((1, 3, DIN_TM, LANES), lambda i, j: (jnp.minimum(j // nqb, 1), 0, i, 0)),
        ],
        out_specs=pl.BlockSpec((1, hpb, DIN_TM, PAIR), lambda i, j: (j // nqb, j % nqb, i, 0)),
        out_shape=jax.ShapeDtypeStruct((3, heads, n, PAIR), BF16),
        scratch_shapes=[pltpu.VMEM((DIN_TM, d), BF16)],
        compiler_params=_cparams(("parallel", "arbitrary")),
        name="diff_in",
    )(h, g, scale, shift, w_bf, tab)


def _chunk_mask():
    r = lax.broadcasted_iota(jnp.int32, (ATT_T, ATT_T), 0) // CHUNK
    c = lax.broadcasted_iota(jnp.int32, (ATT_T, ATT_T), 1) // CHUNK
    return c <= r


def _qk(q, k):
    return lax.dot_general(q, k, (((1,), (1,)), ((), ())), preferred_element_type=F32)


def _fold_lanes(x):
    return [x[:, c * LANES:(c + 1) * LANES] for c in range(ATT_T // LANES)]


def _two_pass_softmax(qi, n_chain, score_fn, value_fn, s_ref, acc_ref):
    mask = _chunk_mask()

    def pass1(kt, mxs, masked):
        new = []
        for e in range(n_chain):
            s = score_fn(e, kt)
            if masked:
                s = jnp.where(mask, s, NEG_BIG)
            s_ref[e, kt] = s
            mx = mxs[e]
            for part in _fold_lanes(s):
                mx = jnp.maximum(mx, part)
            new.append(mx)
        return tuple(new)

    mx0 = jnp.full((ATT_T, LANES), NEG_BIG, F32)
    mxs = lax.fori_loop(0, qi, lambda kt, c: pass1(kt, c, False), (mx0,) * n_chain)
    mxs = pass1(qi, mxs, True)
    ms = [jnp.max(mx, axis=-1, keepdims=True) for mx in mxs]
    acc_ref[...] = jnp.zeros_like(acc_ref)

    def pass2(kt, sums):
        new = []
        for e in range(n_chain):
            p = jnp.exp2(s_ref[e, kt] - ms[e])
            sm = sums[e]
            for part in _fold_lanes(p):
                sm = sm + part
            new.append(sm)
            acc_ref[e] += jnp.dot(p.astype(BF16), value_fn(e, kt), preferred_element_type=F32)
        return tuple(new)

    sums = lax.fori_loop(0, qi + 1, pass2, (jnp.zeros((ATT_T, LANES), F32),) * n_chain)
    return [jnp.sum(sm, axis=-1, keepdims=True) for sm in sums]


DIFF_HEADS_PER_STEP = 4


def _diff_attn_kernel(q_ref, k_ref, v_ref, lam_ref, g_ref, o_ref, s_ref, acc_ref, *, lambda_init):
    qi = pl.program_id(1)
    heads = q_ref.shape[1]
    hps = DIFF_HEADS_PER_STEP
    lam = lam_ref[...]
    lam_full = (jnp.exp(jnp.sum(lam[0:1] * lam[1:2], axis=-1, keepdims=True))
                - jnp.exp(jnp.sum(lam[2:3] * lam[3:4], axis=-1, keepdims=True)) + lambda_init)

    def group_body(gi, carry):
        def score(e, kt):
            h = gi * hps + e // 2
            sl = slice((e % 2) * LANES, (e % 2 + 1) * LANES)
            k0 = pl.multiple_of(kt * ATT_T, ATT_T)
            return _qk(q_ref[0, h, :, sl], k_ref[0, h, pl.ds(k0, ATT_T), sl])

        def value(e, kt):
            k0 = pl.multiple_of(kt * ATT_T, ATT_T)
            return v_ref[0, gi * hps + e // 2, pl.ds(k0, ATT_T), :]

        ls = _two_pass_softmax(qi, 2 * hps, score, value, s_ref, acc_ref)
        for n in range(hps):
            o = acc_ref[2 * n] / ls[2 * n] - lam_full * (acc_ref[2 * n + 1] / ls[2 * n + 1])
            o = _rms(o, g_ref[...]) * (1.0 - lambda_init)
            o_ref[gi * hps + n] = o.astype(BF16)
        return carry

    lax.fori_loop(0, heads // hps, group_body, 0)


def _diff_attn_call(qkv, lam, g, lambda_init, batch, seq):
    _, heads, n, _ = qkv.shape
    nq = seq // ATT_T
    kv_spec = lambda which: pl.BlockSpec((1, heads, seq, PAIR), lambda b, qi: (which, 0, b, 0),
                                         pipeline_mode=pl.Buffered(1))
    return pl.pallas_call(
        functools.partial(_diff_attn_kernel, lambda_init=lambda_init),
        grid=(batch, nq),
        in_specs=[
            pl.BlockSpec((1, heads, ATT_T, PAIR), lambda b, qi: (0, 0, b * nq + qi, 0)),
            kv_spec(1),
            kv_spec(2),
            pl.BlockSpec((4, DIFF_HEAD_DIM), lambda b, qi: (0, 0)),
            pl.BlockSpec((1, PAIR), lambda b, qi: (0, 0)),
        ],
        out_specs=pl.BlockSpec((heads, ATT_T, PAIR), lambda b, qi: (0, b * nq + qi, 0)),
        out_shape=jax.ShapeDtypeStruct((heads, n, PAIR), BF16),
        scratch_shapes=[pltpu.VMEM((2 * DIFF_HEADS_PER_STEP, nq, ATT_T, ATT_T), F32),
                        pltpu.VMEM((2 * DIFF_HEADS_PER_STEP, ATT_T, PAIR), F32)],
        compiler_params=_cparams(("parallel", "arbitrary")),
        name="diff_attn",
    )(qkv, qkv, qkv, lam, g)


MLA_TM = 256


def _mla_in_kernel(h_ref, g_ref, sc_ref, sh_ref, win_ref, gq_ref, gkv_ref, wuq_ref, wukv_ref, tab_ref,
                   proj_ref, kr_ref, *, scale):
    u = _norm_mod(h_ref[...], g_ref[...], sc_ref[0], sh_ref[0]).astype(BF16)
    lat = jnp.dot(u, win_ref[0], preferred_element_type=F32)
    qn = _rms(lat[:, 0:MLA_Q_RANK], gq_ref[...]).astype(BF16)
    kvn = _rms(lat[:, MLA_Q_RANK:MLA_Q_RANK + MLA_KV_RANK], gkv_ref[...]).astype(BF16)
    half = MLA_ROPE // 2
    kr = lat[:, MLA_Q_RANK + MLA_KV_RANK:MLA_Q_RANK + MLA_KV_RANK + LANES]
    kr_ref[...] = _rope128(kr, tab_ref[1, 0], tab_ref[1, 1], tab_ref[1, 2], half).astype(BF16)

    npair = proj_ref.shape[1]
    width = npair * PAIR
    q = jnp.dot(qn, wuq_ref[0], preferred_element_type=F32)
    c = tab_ref[0, 0]
    sa = tab_ref[0, 1]
    sb = tab_ref[0, 2]
    for gp in range(npair):
        proj_ref[0, gp] = (q[:, gp * PAIR:(gp + 1) * PAIR] * scale).astype(BF16)
        for a in range(2):
            lo = width + gp * PAIR + a * LANES
            y = _rope128(q[:, lo:lo + LANES], c, sa, sb, half)
            proj_ref[1, gp, :, a * LANES:(a + 1) * LANES] = y.astype(BF16)
    kv = jnp.dot(kvn, wukv_ref[0], preferred_element_type=F32)
    for gp in range(npair):
        proj_ref[2, gp] = kv[:, gp * PAIR:(gp + 1) * PAIR].astype(BF16)
        proj_ref[3, gp] = kv[:, width + gp * PAIR:width + (gp + 1) * PAIR].astype(BF16)


def _mla_in_call(h, g, scale_a, shift_a, w_in, gq, gkv, w_uq, w_ukv, layer, tab, seq, scale):
    n, d = h.shape
    lat_w = w_in.shape[2]
    up_w = w_uq.shape[2]
    npair = up_w // (2 * PAIR)
    tiles_per_b = seq // MLA_TM
    const = lambda shape: pl.BlockSpec(shape, lambda i: (layer,) + (0,) * (len(shape) - 1),
                                       pipeline_mode=pl.Buffered(1))
    return pl.pallas_call(
        functools.partial(_mla_in_kernel, scale=scale),
        grid=(n // MLA_TM,),
        in_specs=[
            pl.BlockSpec((MLA_TM, d), lambda i: (i, 0)),
            pl.BlockSpec((1, d), lambda i: (0, 0)),
            pl.BlockSpec((1, 1, d), lambda i: (i // tiles_per_b, 0, 0)),
            pl.BlockSpec((1, 1, d), lambda i: (i // tiles_per_b, 0, 0)),
            const((1, d, lat_w)),
            pl.BlockSpec((1, MLA_Q_RANK), lambda i: (0, 0)),
            pl.BlockSpec((1, MLA_KV_RANK), lambda i: (0, 0)),
            const((1, MLA_Q_RANK, up_w)),
            const((1, MLA_KV_RANK, up_w)),
            pl.BlockSpec((2, 3, MLA_TM, LANES), lambda i: (0, 0, i, 0)),
        ],
        out_specs=[
            pl.BlockSpec((4, npair, MLA_TM, PAIR), lambda i: (0, 0, i, 0)),
            pl.BlockSpec((MLA_TM, LANES), lambda i: (i, 0)),
        ],
        out_shape=[
            jax.ShapeDtypeStruct((4, npair, n, PAIR), BF16),
            jax.ShapeDtypeStruct((n, LANES), BF16),
        ],
        compiler_params=_cparams(("parallel",)),
        name="mla_in",
    )(h, g, scale_a, shift_a, w_in, gq, gkv, w_uq, w_ukv, tab)


MLA_PAIRS_PER_STEP = 4


def _mla_attn_kernel(q_ref, kv_ref, kr_ref, o_ref, qc_ref, s_ref, acc_ref):
    qi = pl.program_id(1)
    npair = q_ref.shape[1]
    pps = MLA_PAIRS_PER_STEP
    lane_sl = [slice(a * LANES, (a + 1) * LANES) for a in range(2)]

    def group_body(gi, carry):
        for e in range(2 * pps):
            gp = gi * pps + e // 2
            sl = lane_sl[e % 2]
            qc_ref[e] = jnp.concatenate([q_ref[0, gp, :, sl], q_ref[1, gp, :, sl]], axis=-1)

        def score(e, kt):
            k0 = pl.multiple_of(kt * ATT_T, ATT_T)
            kc = jnp.concatenate([kv_ref[0, gi * pps + e // 2, pl.ds(k0, ATT_T), lane_sl[e % 2]],
                                  kr_ref[pl.ds(k0, ATT_T), :]], axis=-1)
            return _qk(qc_ref[e], kc)

        def value(e, kt):
            k0 = pl.multiple_of(kt * ATT_T, ATT_T)
            return kv_ref[1, gi * pps + e // 2, pl.ds(k0, ATT_T), lane_sl[e % 2]]

        ls = _two_pass_softmax(qi, 2 * pps, score, value, s_ref, acc_ref)
        for e in range(2 * pps):
            o_ref[gi * pps + e // 2, :, lane_sl[e % 2]] = (acc_ref[e] / ls[e]).astype(BF16)
        return carry

    lax.fori_loop(0, npair // pps, group_body, 0)


def _mla_attn_call(proj, kr, batch, seq):
    _, npair, n, _ = proj.shape
    nq = seq // ATT_T
    return pl.pallas_call(
        _mla_attn_kernel,
        grid=(batch, nq),
        in_specs=[
            pl.BlockSpec((2, npair, ATT_T, PAIR), lambda b, qi: (0, 0, b * nq + qi, 0)),
            pl.BlockSpec((2, npair, seq, PAIR), lambda b, qi: (1, 0, b, 0), pipeline_mode=pl.Buffered(1)),
            pl.BlockSpec((seq, LANES), lambda b, qi: (b, 0), pipeline_mode=pl.Buffered(1)),
        ],
        out_specs=pl.BlockSpec((npair, ATT_T, PAIR), lambda b, qi: (0, b * nq + qi, 0)),
        out_shape=jax.ShapeDtypeStruct((npair, n, PAIR), BF16),
        scratch_shapes=[pltpu.VMEM((2 * MLA_PAIRS_PER_STEP, ATT_T, PAIR), BF16),
                        pltpu.VMEM((2 * MLA_PAIRS_PER_STEP, nq, ATT_T, ATT_T), F32),
                        pltpu.VMEM((2 * MLA_PAIRS_PER_STEP, ATT_T, LANES), F32)],
        compiler_params=_cparams(("parallel", "arbitrary")),
        name="mla_attn",
    )(proj, proj, kr)


AOUT_TM = 512
AOUT_ROW_BLOCKS = 2


def _attn_out_kernel(o_ref, w_ref, h_ref, gate_ref, g_ref, sc_ref, sh_ref, wr_ref, br_ref,
                     hn_ref, u_ref, idx_ref, gw_ref):
    ns = o_ref.shape[0]
    tm = h_ref.shape[0] // AOUT_ROW_BLOCKS
    for r in range(AOUT_ROW_BLOCKS):
        rs = slice(r * tm, (r + 1) * tm)
        y = jnp.dot(o_ref[0, rs, :], w_ref[0, 0:PAIR, :], preferred_element_type=F32)
        for s in range(1, ns):
            y = y + jnp.dot(o_ref[s, rs, :], w_ref[0, s * PAIR:(s + 1) * PAIR, :], preferred_element_type=F32)
        hn = h_ref[rs, :] + gate_ref[0] * y
        hn_ref[rs, :] = hn
        u = _norm_mod(hn, g_ref[...], sc_ref[0], sh_ref[0])
        u_ref[rs, :] = _pack_halves(u)
        u_hi = u.astype(BF16)
        u_lo = (u - u_hi.astype(F32)).astype(BF16)
        logits = jnp.dot(jnp.concatenate([u_hi, u_lo, u_hi], axis=-1), wr_ref[0],
                         preferred_element_type=F32) + br_ref[0]
        ne = logits.shape[1]
        eio = lax.broadcasted_iota(jnp.int32, (tm, ne), 1)
        lane = lax.broadcasted_iota(jnp.int32, (tm, LANES), 1)
        idx_out = jnp.zeros((tm, LANES), jnp.int32)
        val_out = jnp.zeros((tm, LANES), F32)
        l = logits
        v0 = None
        den = None
        for k in range(TOP_K):
            m = jnp.max(l, axis=-1, keepdims=True)
            i = jnp.min(jnp.where(l == m, eio, ne), axis=-1, keepdims=True)
            l = jnp.where(eio == i, -jnp.inf, l)
            if k == 0:
                v0 = m
            e = jnp.exp(m - v0)
            den = e if den is None else den + e
            idx_out = jnp.where(lane == k, i, idx_out)
            val_out = jnp.where(lane == k, e, val_out)
        idx_ref[rs, :] = idx_out
        gw_ref[rs, :] = val_out / den


def _attn_out_call(o, w_out, layer, h, gate, g, scale, shift, w_router, b_router, seq):
    ns, n, _ = o.shape
    d = h.shape[1]
    ne = w_router.shape[2]
    tiles_per_b = seq // AOUT_TM
    perb = lambda: pl.BlockSpec((1, 1, d), lambda i: (i // tiles_per_b, 0, 0))
    return pl.pallas_call(
        _attn_out_kernel,
        grid=(n // AOUT_TM,),
        in_specs=[
            pl.BlockSpec((ns, AOUT_TM, PAIR), lambda i: (0, i, 0)),
            pl.BlockSpec((1, d, d), lambda i: (layer, 0, 0), pipeline_mode=pl.Buffered(1)),
            pl.BlockSpec((AOUT_TM, d), lambda i: (i, 0)),
            perb(),
            pl.BlockSpec((1, d), lambda i: (0, 0)),
            perb(),
            perb(),
            pl.BlockSpec((1, 3 * d, ne), lambda i: (0, 0, 0)),
            pl.BlockSpec((1, 1, ne), lambda i: (0, 0, 0)),
        ],
        out_specs=[
            pl.BlockSpec((AOUT_TM, d), lambda i: (i, 0)),
            pl.BlockSpec((AOUT_TM, d // 2), lambda i: (i, 0)),
            pl.BlockSpec((AOUT_TM, LANES), lambda i: (i, 0)),
            pl.BlockSpec((AOUT_TM, LANES), lambda i: (i, 0)),
        ],
        out_shape=[
            jax.ShapeDtypeStruct((n, d), F32),
            jax.ShapeDtypeStruct((n, d // 2), jnp.uint32),
            jax.ShapeDtypeStruct((n, LANES), jnp.int32),
            jax.ShapeDtypeStruct((n, LANES), F32),
        ],
        compiler_params=_cparams(("parallel",)),
        name="attn_out",
    )(o, w_out, h, gate, g, scale, shift, w_router, b_router)


def _route_kernel(idx_ref, dest_ref, pt_ref, carry_ref, ps_ref):
    ph = pl.program_id(0)
    t = pl.program_id(1)
    nt = pl.num_programs(1)
    tb = idx_ref.shape[0]
    idx = idx_ref[...]
    eio = lax.broadcasted_iota(jnp.int32, (tb, LANES), 1)
    sel = [eio == idx[:, k:k + 1] for k in range(TOP_K)]
    oh = jnp.where(sel[0], 1.0, 0.0)
    for k in range(1, TOP_K):
        oh = oh + jnp.where(sel[k], 1.0, 0.0)

    @pl.when(t == 0)
    def _():
        carry_ref[...] = jnp.zeros_like(carry_ref)

    @pl.when((ph == 0) & (t == nt - 1))
    def _():
        cnt = carry_ref[...] + jnp.sum(oh, axis=0, keepdims=True)
        tiles = jnp.floor((cnt + (EXPERT_TM - 0.5)) * (1.0 / EXPERT_TM))
        r = lax.broadcasted_iota(jnp.int32, (LANES, LANES), 0)
        c = lax.broadcasted_iota(jnp.int32, (LANES, LANES), 1)
        upper = jnp.where(r < c, 1.0, 0.0).astype(BF16)
        t8 = jnp.broadcast_to(tiles, (SUBLANES, LANES))
        start = jnp.dot(t8.astype(BF16), upper, preferred_element_type=F32)
        ps_ref[...] = start[0:1] * EXPERT_TM
        row = lax.broadcasted_iota(jnp.int32, (SUBLANES, LANES), 0)
        pt_ref[...] = jnp.where(row == 0, t8, jnp.broadcast_to(cnt, (SUBLANES, LANES)))

    @pl.when(ph == 1)
    def _():
        r = lax.broadcasted_iota(jnp.int32, (tb, tb), 0)
        c = lax.broadcasted_iota(jnp.int32, (tb, tb), 1)
        lower = jnp.where(c < r, 1.0, 0.0).astype(BF16)
        before = jnp.dot(lower, oh.astype(BF16), preferred_element_type=F32)
        base = before + carry_ref[...] + ps_ref[...]
        out = jnp.zeros((tb, LANES), jnp.int32)
        for k in range(TOP_K):
            dk = jnp.sum(jnp.where(sel[k], base, 0.0), axis=-1, keepdims=True)
            out = jnp.where(eio == k, dk.astype(jnp.int32), out)
        dest_ref[...] = out

    carry_ref[...] += jnp.sum(oh, axis=0, keepdims=True)


def _route_call(idx):
    n = idx.shape[0]
    return pl.pallas_call(
        _route_kernel,
        grid=(2, n // ROUTE_TB),
        in_specs=[pl.BlockSpec((ROUTE_TB, LANES), lambda p, t: (t, 0))],
        out_specs=[
            pl.BlockSpec((ROUTE_TB, LANES), lambda p, t: (p * t, 0)),
            pl.BlockSpec((SUBLANES, LANES), lambda p, t: (0, 0)),
        ],
        out_shape=[
            jax.ShapeDtypeStruct((n, LANES), jnp.int32),
            jax.ShapeDtypeStruct((SUBLANES, LANES), F32),
        ],
        scratch_shapes=[pltpu.VMEM((1, LANES), F32), pltpu.VMEM((1, LANES), F32)],
        compiler_params=_cparams(("arbitrary", "arbitrary")),
        name="route",
    )(idx)


DISPATCH_ZROWS = 512
DISPATCH_TAIL = 64
ROW_DMA_UNROLL = 4
assert DISPATCH_ZROWS <= EXPERT_TM < 2 * DISPATCH_ZROWS and EXPERT_TM % DISPATCH_TAIL == 0


def _dispatch_kernel(dest_ref, pstart_ref, cnt_ref, pend_ref, u_ref, xb_ref, zbuf_ref, sem, zsem):
    i = pl.program_id(0)
    tt = u_ref.shape[0]
    rows = xb_ref.shape[0]
    ne = pstart_ref.shape[0]

    def zero_copy(pos, nrows):
        return pltpu.make_async_copy(zbuf_ref.at[pl.ds(0, nrows)], xb_ref.at[pl.ds(pos, nrows)], zsem)

    def pad_fill(act):
        def expert_body(e, carry):
            pos = pstart_ref[e] + cnt_ref[e]
            npad = pend_ref[e] - pos
            head = jnp.minimum(npad, (-pos) & (SUBLANES - 1))

            def head_body(r, c):
                act(zero_copy(pos + r, 1))
                return c

            lax.fori_loop(0, head, head_body, 0)
            base = pos + head
            rem = npad - head
            bit = DISPATCH_ZROWS
            while bit >= SUBLANES:
                off = rem - (rem & (2 * bit - 1))

                @pl.when((rem & bit) != 0)
                def _(off=off, bit=bit):
                    act(zero_copy(pl.multiple_of(base + off, SUBLANES), bit))

                bit //= 2
            return carry

        lax.fori_loop(0, ne, expert_body, 0)
        tail0 = pend_ref[ne - 1]

        def tail_body(b, c):
            act(zero_copy(pl.multiple_of(tail0 + b * DISPATCH_TAIL, DISPATCH_TAIL), DISPATCH_TAIL))
            return c

        lax.fori_loop(0, (rows - tail0) // DISPATCH_TAIL, tail_body, 0)

    @pl.when(i == 0)
    def _():
        zbuf_ref[...] = jnp.zeros_like(zbuf_ref)
        pad_fill(lambda c: c.start())

    def row_copy(t, d):
        return pltpu.make_async_copy(u_ref.at[pl.ds(t, 1)], xb_ref.at[pl.ds(d, 1)], sem)

    def issue(t, carry):
        for k in range(TOP_K):
            row_copy(t, dest_ref[(i * tt + t) * TOP_K + k]).start(priority=k % 2)
        return carry

    lax.fori_loop(0, tt, issue, 0, unroll=ROW_DMA_UNROLL)
    pltpu.make_async_copy(xb_ref.at[pl.ds(0, TOP_K * tt)], xb_ref.at[pl.ds(0, TOP_K * tt)], sem).wait()

    @pl.when(i == 0)
    def _():
        pad_fill(lambda c: c.wait())


def _dispatch_call(dest_flat, pstart, cnt, pend, u, rows):
    n, d = u.shape
    return pl.pallas_call(
        _dispatch_kernel,
        grid_spec=pltpu.PrefetchScalarGridSpec(
            num_scalar_prefetch=4,
            grid=(n // DISPATCH_TT,),
            in_specs=[pl.BlockSpec((DISPATCH_TT, d), lambda i, *_: (i, 0))],
            out_specs=pl.BlockSpec(memory_space=pl.ANY),
            scratch_shapes=[pltpu.VMEM((DISPATCH_ZROWS, d), u.dtype),
                            pltpu.SemaphoreType.DMA(()), pltpu.SemaphoreType.DMA(())],
        ),
        out_shape=jax.ShapeDtypeStruct((rows, d), u.dtype),
        compiler_params=_cparams(("arbitrary",), has_side_effects=True),
        name="dispatch",
    )(dest_flat, pstart, cnt, pend, u)


def _expert_kernel(te_ref, first_ref, nu_ref, x_ref, wg_ref, wu_ref, bg_ref, bu_ref, wd_ref, bd_ref, o_ref,
                   xbf_ref, wgb_ref, wub_ref, wdb_ref, act_ref):
    del te_ref
    i = pl.program_id(0)
    j = pl.program_id(1)
    nf = act_ref.shape[0]
    half = x_ref.shape[1]

    @pl.when(i < nu_ref[0])
    def _():
        @pl.when(j == 0)
        def _():
            lo, hi = _unpack_halves(x_ref[...])
            xbf_ref[:, 0:half] = lo.astype(BF16)
            xbf_ref[:, half:2 * half] = hi.astype(BF16)

        rb = x_ref.shape[0] // EXPERT_ROW_BLOCKS

        def hidden_chunk(wg, wu):
            for r in range(EXPERT_ROW_BLOCKS):
                rs = slice(r * rb, (r + 1) * rb)
                x = xbf_ref[rs, :]
                gate = jnp.dot(x, wg, preferred_element_type=F32) + bg_ref[0, 0]
                up = jnp.dot(x, wu, preferred_element_type=F32) + bu_ref[0, 0]
                gate = jnp.minimum(gate, SWIGLU_LIMIT)
                up = jnp.clip(up, -SWIGLU_LIMIT, SWIGLU_LIMIT)
                glu = gate * _sigmoid(gate * SWIGLU_ALPHA)
                act_ref[j, rs, :] = ((up + 1.0) * glu).astype(BF16)

        @pl.when(first_ref[i] == 1)
        def _():
            wg = wg_ref[0, 0].astype(BF16)
            wu = wu_ref[0, 0].astype(BF16)
            wgb_ref[j] = wg
            wub_ref[j] = wu
            wdb_ref[j] = wd_ref[0, 0].astype(BF16)
            hidden_chunk(wg, wu)

        @pl.when(first_ref[i] != 1)
        def _():
            hidden_chunk(wgb_ref[j], wub_ref[j])

        @pl.when(j == nf - 1)
        def _():
            for r in range(EXPERT_ROW_BLOCKS):
                rs = slice(r * rb, (r + 1) * rb)
                y = jnp.dot(act_ref[0, rs, :], wdb_ref[0], preferred_element_type=F32)
                for c in range(1, nf):
                    y = y + jnp.dot(act_ref[c, rs, :], wdb_ref[c], preferred_element_type=F32)
                o_ref[rs, :] = _pack_halves(y + bd_ref[0, 0])

    @pl.when((i >= nu_ref[0]) & (j == 0))
    def _():
        o_ref[...] = jnp.zeros_like(o_ref)


def _expert_call(tile_e, first, n_used, xb, w_gu, b_gu, w_down, b_down, layer):
    rows, dw = xb.shape
    d = 2 * dw
    ne = w_gu.shape[1]
    f = w_down.shape[2]
    nf = f // EXPERT_TF
    n_tiles = rows // EXPERT_TM

    def tile(i, nu):
        return jnp.minimum(i, nu[0] - 1)

    def fcol(i, j, nu):
        return jnp.where(i < nu[0], j, nf - 1)

    def wcol(i, j, fi, nu):
        return jnp.where((i < nu[0]) & (fi[tile(i, nu)] == 1), j, nf - 1)

    return pl.pallas_call(
        _expert_kernel,
        grid_spec=pltpu.PrefetchScalarGridSpec(
            num_scalar_prefetch=3,
            grid=(n_tiles, nf),
            in_specs=[
                pl.BlockSpec((EXPERT_TM, dw), lambda i, j, te, fi, nu: (tile(i, nu), 0)),
                pl.BlockSpec((1, 1, d, EXPERT_TF),
                             lambda i, j, te, fi, nu: (layer, te[tile(i, nu)], 0, wcol(i, j, fi, nu))),
                pl.BlockSpec((1, 1, d, EXPERT_TF),
                             lambda i, j, te, fi, nu: (layer, te[tile(i, nu)], 0, nf + wcol(i, j, fi, nu))),
                pl.BlockSpec((1, 1, 1, EXPERT_TF),
                             lambda i, j, te, fi, nu: (layer, te[tile(i, nu)], 0, fcol(i, j, nu))),
                pl.BlockSpec((1, 1, 1, EXPERT_TF),
                             lambda i, j, te, fi, nu: (layer, te[tile(i, nu)], 0, nf + fcol(i, j, nu))),
                pl.BlockSpec((1, 1, EXPERT_TF, d),
                             lambda i, j, te, fi, nu: (layer, te[tile(i, nu)], wcol(i, j, fi, nu), 0)),
                pl.BlockSpec((1, 1, 1, d), lambda i, j, te, fi, nu: (layer, te[tile(i, nu)], 0, 0)),
            ],
            out_specs=pl.BlockSpec((EXPERT_TM, dw), lambda i, j, te, fi, nu: (i, 0)),
            scratch_shapes=[pltpu.VMEM((EXPERT_TM, d), BF16),
                            pltpu.VMEM((nf, d, EXPERT_TF), BF16),
                            pltpu.VMEM((nf, d, EXPERT_TF), BF16),
                            pltpu.VMEM((nf, EXPERT_TF, d), BF16),
                            pltpu.VMEM((nf, EXPERT_TM, EXPERT_TF), BF16)],
        ),
        out_shape=jax.ShapeDtypeStruct((rows, dw), jnp.uint32),
        compiler_params=_cparams(("arbitrary", "arbitrary")),
        name="experts",
    )(tile_e, first, n_used, xb, w_gu, w_gu, b_gu.reshape(b_gu.shape[0], ne, 1, 2 * f),
      b_gu.reshape(b_gu.shape[0], ne, 1, 2 * f), w_down, b_down.reshape(b_down.shape[0], ne, 1, d))


def _combine_kernel(dest_ref, yb_ref, gw_ref, h_ref, gate_ref, g_ref, o_ref, buf_ref, sem, *, final):
    i = pl.program_id(0)
    nsteps = pl.num_programs(0)
    tt = h_ref.shape[0]
    slot = i % 2

    def issue(tile, sl):
        def body(t, carry):
            for k in range(TOP_K):
                d = dest_ref[(tile * tt + t) * TOP_K + k]
                pltpu.make_async_copy(yb_ref.at[pl.ds(d, 1)], buf_ref.at[sl, pl.ds(k * tt + t, 1)],
                                      sem.at[sl]).start(priority=k % 2)
            return carry
        lax.fori_loop(0, tt, body, 0, unroll=ROW_DMA_UNROLL)

    @pl.when(i == 0)
    def _():
        issue(0, 0)

    @pl.when(i + 1 < nsteps)
    def _():
        issue(i + 1, 1 - slot)

    pltpu.make_async_copy(yb_ref.at[pl.ds(0, TOP_K * tt)], buf_ref.at[slot], sem.at[slot]).wait()
    gw = gw_ref[...]
    y_lo, y_hi = None, None
    for k in range(TOP_K):
        lo, hi = _unpack_halves(buf_ref[slot, k * tt:(k + 1) * tt])
        wk = gw[:, k:k + 1]
        y_lo = wk * lo if y_lo is None else y_lo + wk * lo
        y_hi = wk * hi if y_hi is None else y_hi + wk * hi
    hn = h_ref[...] + gate_ref[0] * jnp.concatenate([y_lo, y_hi], axis=-1)
    if final:
        hn = _rms(hn, g_ref[...])
    o_ref[...] = hn


def _combine_call(dest_flat, yb, gw, h, gate, g_final, seq, final):
    n, d = h.shape
    tiles_per_b = seq // COMBINE_TT
    return pl.pallas_call(
        functools.partial(_combine_kernel, final=final),
        grid_spec=pltpu.PrefetchScalarGridSpec(
            num_scalar_prefetch=1,
            grid=(n // COMBINE_TT,),
            in_specs=[
                pl.BlockSpec(memory_space=pl.ANY),
                pl.BlockSpec((COMBINE_TT, LANES), lambda i, dest: (i, 0)),
                pl.BlockSpec((COMBINE_TT, d), lambda i, dest: (i, 0)),
                pl.BlockSpec((1, 1, d), lambda i, dest: (i // tiles_per_b, 0, 0)),
                pl.BlockSpec((1, d), lambda i, dest: (0, 0)),
            ],
            out_specs=pl.BlockSpec((COMBINE_TT, d), lambda i, dest: (i, 0)),
            scratch_shapes=[pltpu.VMEM((2, TOP_K * COMBINE_TT, d // 2), jnp.uint32),
                            pltpu.SemaphoreType.DMA((2,))],
        ),
        out_shape=jax.ShapeDtypeStruct((n, d), F32),
        compiler_params=_cparams(("arbitrary",)),
        name="combine",
    )(dest_flat, yb, gw, h, gate, g_final)


def _rope_tables(positions, rot_dim, scale):
    half = rot_dim // 2
    inv_freq = ROPE_THETA ** (-jnp.arange(0, rot_dim, 2, dtype=F32) / rot_dim)
    ang = positions.reshape(-1).astype(F32)[:, None] * inv_freq
    cos, sin = jnp.cos(ang), jnp.sin(ang)
    n = ang.shape[0]
    c = jnp.concatenate([cos, cos, jnp.ones((n, LANES - rot_dim), F32)], axis=1)
    sa = jnp.concatenate([-sin, jnp.zeros((n, LANES - half), F32)], axis=1)
    sb = jnp.concatenate([jnp.zeros((n, half), F32), sin, jnp.zeros((n, LANES - rot_dim), F32)], axis=1)
    t = jnp.stack([c, sa, sb])
    return jnp.stack([t * scale, t])


def _moe(layer, h, u, idx, gw, gate_f, g_final, final, seq, w_gu, b_gu, w_down, b_down):
    n, d = h.shape
    rows = -(-(n * TOP_K + N_EXPERTS * (EXPERT_TM - 1)) // EXPERT_TM) * EXPERT_TM
    n_tiles = rows // EXPERT_TM
    dest, stats = _route_call(idx)
    dest_flat = dest[:, :TOP_K].reshape(-1)
    cnt = stats[1, :N_EXPERTS].astype(jnp.int32)
    tile_end = jnp.cumsum(stats[0, :N_EXPERTS].astype(jnp.int32))
    pend = tile_end * EXPERT_TM
    pstart = jnp.concatenate([jnp.zeros((1,), jnp.int32), pend[:-1]])
    tile_e = jnp.minimum(jnp.searchsorted(tile_end, jnp.arange(n_tiles, dtype=jnp.int32), side='right'),
                         N_EXPERTS - 1).astype(jnp.int32)
    first = jnp.concatenate([jnp.ones((1,), jnp.int32), (tile_e[1:] != tile_e[:-1]).astype(jnp.int32)])
    n_used = tile_end[-1:].astype(jnp.int32)
    xb = _dispatch_call(dest_flat, pstart, cnt, pend, u, rows)
    yb = _expert_call(tile_e, first, n_used, xb, w_gu, b_gu, w_down, b_down, layer)
    return _combine_call(dest_flat, yb, gw, h, gate_f, g_final, seq, final)


def kernel(x, c, positions, ada_w, ada_b, mix_norm_g, ffn_norm_g, final_norm_g, diff_w_in, diff_lambda, diff_subln_g, diff_w_out, mla_w_in, mla_q_norm_g, mla_kv_norm_g, mla_w_uq, mla_w_ukv, mla_w_out, moe_w_router, moe_b_router, moe_w_gate_up, moe_b_gate_up, moe_w_down, moe_b_down):
    batch, seq, d = x.shape
    depth = ada_w.shape[0]
    n = batch * seq
    assert seq % ATT_T == 0 and ATT_T % CHUNK == 0 and d % PAIR == 0
    assert n % DISPATCH_TT == 0 and n % ROUTE_TB == 0 and seq % DIN_TM == 0

    diff_scale = DIFF_HEAD_DIM ** -0.5 * LOG2E
    mla_scale = (MLA_NOPE + MLA_ROPE) ** -0.5 * LOG2E
    tab_d = _rope_tables(positions, DIFF_HEAD_DIM // 4, diff_scale)
    tab_m = _rope_tables(positions, MLA_ROPE, mla_scale)

    mod = _mod_call(c, ada_w, ada_b)
    mod = mod.reshape(depth, batch, N_MOD, 1, d)

    diff_w_in_bf = diff_w_in.astype(BF16)
    diff_w_out_bf = diff_w_out.astype(BF16)
    mla_w_out_bf = mla_w_out.astype(BF16)
    nm = mla_w_in.shape[0]
    heads_m = mla_w_out.shape[1] // MLA_V
    lat_pad = LANES - MLA_ROPE
    mla_w_in_bf = jnp.pad(mla_w_in, ((0, 0), (0, 0), (0, lat_pad))).astype(BF16)
    wq = mla_w_uq.reshape(nm, MLA_Q_RANK, heads_m, MLA_NOPE + MLA_ROPE)
    wq_rope = jnp.pad(wq[..., MLA_NOPE:], ((0, 0), (0, 0), (0, 0), (0, LANES - MLA_ROPE)))
    mla_w_uq_bf = jnp.concatenate([wq[..., :MLA_NOPE].reshape(nm, MLA_Q_RANK, -1),
                                   wq_rope.reshape(nm, MLA_Q_RANK, -1)], axis=-1).astype(BF16)
    wkv = mla_w_ukv.reshape(nm, MLA_KV_RANK, heads_m, MLA_NOPE + MLA_V)
    mla_w_ukv_bf = jnp.concatenate([wkv[..., :MLA_NOPE].reshape(nm, MLA_KV_RANK, -1),
                                    wkv[..., MLA_NOPE:].reshape(nm, MLA_KV_RANK, -1)], axis=-1).astype(BF16)

    h = x.reshape(n, d)
    for i in range(depth):
        shift_a, scale_a, gate_a, shift_f, scale_f, gate_f = [mod[i, :, m] for m in range(N_MOD)]
        g_mix = mix_norm_g[i].reshape(1, d)
        g_ffn = ffn_norm_g[i].reshape(1, d)
        j = i // N_MIXERS
        if i % N_MIXERS == 0:
            lambda_init = 0.8 - 0.6 * math.exp(-0.3 * i)
            qkv = _diff_in_call(h, g_mix, scale_a, shift_a, diff_w_in_bf, j, tab_d, seq)
            o = _diff_attn_call(qkv, diff_lambda[j], diff_subln_g[j].reshape(1, -1), lambda_init, batch, seq)
            w_out = diff_w_out_bf
        else:
            proj, kr = _mla_in_call(h, g_mix, scale_a, shift_a, mla_w_in_bf, mla_q_norm_g[j].reshape(1, -1),
                                    mla_kv_norm_g[j].reshape(1, -1), mla_w_uq_bf, mla_w_ukv_bf, j, tab_m, seq,
                                    mla_scale)
            o = _mla_attn_call(proj, kr, batch, seq)
            w_out = mla_w_out_bf
        wr_hi = moe_w_router[i:i + 1].astype(BF16)
        wr_lo = (moe_w_router[i:i + 1] - wr_hi.astype(F32)).astype(BF16)
        wr3 = jnp.concatenate([wr_hi, wr_hi, wr_lo], axis=1)
        h, u, idx, gw = _attn_out_call(o, w_out, j, h, gate_a, g_ffn, scale_f, shift_f,
                                       wr3, moe_b_router[i:i + 1].reshape(1, 1, -1), seq)
        h = _moe(i, h, u, idx, gw, gate_f, final_norm_g.reshape(1, d), i == depth - 1, seq,
                 moe_w_gate_up, moe_b_gate_up, moe_w_down, moe_b_down)
    return h.reshape(batch, seq, d)
```
